```python
import math
import jax, jax.numpy as jnp
from jax import lax
import numpy as np

D_MODEL = 1024
BATCH = 4
SEQ = 4096
DEPTH = 1

SSM_CH = 16
D_SSM = 512
SSM_GROUPS = D_SSM // SSM_CH
SSM_STATE = 64
N_HEADS = 8
HEAD_DIM = 64
D_ATTN = N_HEADS * HEAD_DIM
D_MIX = D_SSM + D_ATTN
IDX_HEADS = 8
IDX_DIM = 64
TOPK_MAX = 256
Q_BLOCK = 128
D_FF = 2816
CONV_W = 3
ROPE_THETA = 10000.0
EPS = 1e-6
D_IN_PROJ = D_SSM + 3 * D_ATTN + IDX_HEADS * IDX_DIM + IDX_DIM + IDX_HEADS

kernel_name = "hymba_s5_dsa_convffn_adaln"


def _rmsnorm(x, g):
    xf = x.astype(jnp.float32)
    return xf * lax.rsqrt(jnp.mean(xf * xf, axis=-1, keepdims=True) + EPS) * g.astype(jnp.float32)


def _modulate(xn, shift, scale):
    return xn * (1.0 + scale[:, None, :].astype(jnp.float32)) + shift[:, None, :].astype(jnp.float32)


def _rope(x, pos):
    d = x.shape[-1]
    inv = ROPE_THETA ** (-jnp.arange(0, d, 2, dtype=jnp.float32) / d)
    ang = pos.astype(jnp.float32)[..., None] * inv
    cos = jnp.cos(ang)[:, :, None, :]
    sin = jnp.sin(ang)[:, :, None, :]
    xf = x.astype(jnp.float32)
    x1, x2 = xf[..., : d // 2], xf[..., d // 2:]
    return jnp.concatenate([x1 * cos - x2 * sin, x1 * sin + x2 * cos], axis=-1)


def _s5_mixer(u, log_dt, a_re, a_im, b_re, b_im, c_re, c_im, d_skip, w_glu, b_glu):
    bsz, L, _ = u.shape
    uf = u.astype(jnp.float32).reshape(bsz, L, SSM_GROUPS, SSM_CH)
    dt = jnp.exp(log_dt.astype(jnp.float32))[:, None]
    lr = a_re.astype(jnp.float32)
    li = a_im.astype(jnp.float32)
    mag = jnp.exp(lr * dt)
    lb_re = mag * jnp.cos(li * dt)
    lb_im = mag * jnp.sin(li * dt)
    den = lr * lr + li * li
    nr, ni = lb_re - 1.0, lb_im
    coef_re = (nr * lr + ni * li) / den
    coef_im = (ni * lr - nr * li) / den
    br = b_re.astype(jnp.float32)
    bi = b_im.astype(jnp.float32)
    bb_re = coef_re[..., None] * br - coef_im[..., None] * bi
    bb_im = coef_re[..., None] * bi + coef_im[..., None] * br
    bu_re = jnp.einsum('blgc,gpc->blgp', uf, bb_re)
    bu_im = jnp.einsum('blgc,gpc->blgp', uf, bb_im)
    a_re_t = jnp.broadcast_to(lb_re, bu_re.shape)
    a_im_t = jnp.broadcast_to(lb_im, bu_im.shape)

    def combine(e1, e2):
        a1r, a1i, b1r, b1i = e1
        a2r, a2i, b2r, b2i = e2
        return (a2r * a1r - a2i * a1i,
                a2r * a1i + a2i * a1r,
                a2r * b1r - a2i * b1i + b2r,
                a2r * b1i + a2i * b1r + b2i)

    _, _, xr, xi = lax.associative_scan(combine, (a_re_t, a_im_t, bu_re, bu_im), axis=1)
    y = (jnp.einsum('blgp,gcp->blgc', xr, c_re.astype(jnp.float32))
         - jnp.einsum('blgp,gcp->blgc', xi, c_im.astype(jnp.float32))
         + d_skip.astype(jnp.float32).reshape(SSM_GROUPS, SSM_CH) * uf)
    y = jax.nn.gelu(y.reshape(bsz, L, D_SSM))
    return y * jax.nn.sigmoid(y @ w_glu.astype(jnp.float32) + b_glu.astype(jnp.float32))


def _dsa_mixer(q, k, v, qi, ki, wi):
    bsz, L = q.shape[0], q.shape[1]
    topk = min(TOPK_MAX, L // 4)
    nblk = L // Q_BLOCK
    scale = 1.0 / math.sqrt(HEAD_DIM)
    key_pos = jnp.arange(L, dtype=jnp.int32)

    def to_blocks(t):
        t = t.reshape((bsz, nblk, Q_BLOCK) + t.shape[2:])
        return jnp.moveaxis(t, 1, 0)

    def one_block(args):
        qb, qib, wib, start = args
        qpos = start + jnp.arange(Q_BLOCK, dtype=jnp.int32)
        s_h = jnp.einsum('bthd,bsd->bths', qib, ki)
        s_idx = jnp.einsum('bths,bth->bts', jax.nn.relu(s_h), wib)
        causal = key_pos[None, :] <= qpos[:, None]
        s_idx = jnp.where(causal[None], s_idx, -jnp.inf)
        _, sel = lax.top_k(s_idx, topk)
        valid = sel <= qpos[None, :, None]
        kg = jax.vmap(lambda kk, ii: kk[ii])(k, sel)
        vg = jax.vmap(lambda vv, ii: vv[ii])(v, sel)
        logits = jnp.einsum('bthd,btkhd->bhtk', qb, kg) * scale
        logits = jnp.where(valid[:, None], logits, -jnp.inf)
        p = jax.nn.softmax(logits, axis=-1)
        return jnp.einsum('bhtk,btkhd->bthd', p, vg.astype(jnp.float32))

    starts = jnp.arange(nblk, dtype=jnp.int32) * Q_BLOCK
    out = lax.map(one_block, (to_blocks(q), to_blocks(qi), to_blocks(wi), starts))
    out = jnp.moveaxis(out, 0, 1)
    return out.reshape(bsz, L, D_ATTN)


def _causal_dwconv(h, w, b):
    L = h.shape[1]
    hp = jnp.pad(h, ((0, 0), (CONV_W - 1, 0), (0, 0)))
    out = b.astype(jnp.float32)
    for j in range(CONV_W):
        out = out + w[j].astype(jnp.float32) * hp[:, j:j + L, :]
    return out


def setup_inputs(seed: int = 0) -> dict:
    key = jax.random.key(seed)
    ks = jax.random.split(key, 32)
    f32 = jnp.float32
    nrm = lambda k, shape, s: (jax.random.normal(k, shape, f32) * s)
    Dp = DEPTH
    x = jax.random.normal(ks[0], (BATCH, SEQ, D_MODEL), f32)
    c = jax.random.normal(ks[1], (BATCH, D_MODEL), f32)
    positions = jnp.broadcast_to(jnp.arange(SEQ, dtype=jnp.int32), (BATCH, SEQ))
    w_ada = nrm(ks[2], (Dp, D_MODEL, 6 * D_MODEL), 0.5 * D_MODEL ** -0.5)
    b_ada = nrm(ks[3], (Dp, 6 * D_MODEL), 0.02)
    norm_mix = 1.0 + nrm(ks[4], (Dp, D_MODEL), 0.02)
    w_in = nrm(ks[5], (Dp, D_MODEL, D_IN_PROJ), D_MODEL ** -0.5)
    ssm_log_dt = jax.random.uniform(ks[6], (Dp, SSM_GROUPS), f32, math.log(1e-3), math.log(1e-1))
    ssm_a_re = -0.5 + nrm(ks[7], (Dp, SSM_GROUPS, SSM_STATE), 0.01)
    ssm_a_im = (math.pi * jnp.arange(SSM_STATE, dtype=f32))[None, None, :] + nrm(ks[8], (Dp, SSM_GROUPS, SSM_STATE), 0.01)
    ssm_b_re = nrm(ks[9], (Dp, SSM_GROUPS, SSM_STATE, SSM_CH), (2 * SSM_CH) ** -0.5)
    ssm_b_im = nrm(ks[10], (Dp, SSM_GROUPS, SSM_STATE, SSM_CH), (2 * SSM_CH) ** -0.5)
    ssm_c_re = nrm(ks[11], (Dp, SSM_GROUPS, SSM_CH, SSM_STATE), (2 * SSM_STATE) ** -0.5)
    ssm_c_im = nrm(ks[12], (Dp, SSM_GROUPS, SSM_CH, SSM_STATE), (2 * SSM_STATE) ** -0.5)
    ssm_d = nrm(ks[13], (Dp, D_SSM), 1.0)
    ssm_w_glu = nrm(ks[14], (Dp, D_SSM, D_SSM), D_SSM ** -0.5)
    ssm_b_glu = nrm(ks[15], (Dp, D_SSM), 0.01)
    q_norm = 1.0 + nrm(ks[16], (Dp, HEAD_DIM), 0.02)
    k_norm = 1.0 + nrm(ks[17], (Dp, HEAD_DIM), 0.02)
    out_norm_ssm = 1.0 + nrm(ks[18], (Dp, D_SSM), 0.02)
    out_norm_attn = 1.0 + nrm(ks[19], (Dp, D_ATTN), 0.02)
    w_out = nrm(ks[20], (Dp, D_MIX, D_MODEL), D_MIX ** -0.5)
    norm_ffn = 1.0 + nrm(ks[21], (Dp, D_MODEL), 0.02)
    w_up = nrm(ks[22], (Dp, D_MODEL, 2 * D_FF), D_MODEL ** -0.5)
    conv_w = nrm(ks[23], (Dp, CONV_W, 2 * D_FF), CONV_W ** -0.5)
    conv_b = nrm(ks[24], (Dp, 2 * D_FF), 0.01)
    w_down = nrm(ks[25], (Dp, D_FF, D_MODEL), D_FF ** -0.5)
    return {"x": x, "c": c, "positions": positions, "w_ada": w_ada, "b_ada": b_ada,
            "norm_mix": norm_mix, "w_in": w_in, "ssm_log_dt": ssm_log_dt,
            "ssm_a_re": ssm_a_re, "ssm_a_im": ssm_a_im, "ssm_b_re": ssm_b_re,
            "ssm_b_im": ssm_b_im, "ssm_c_re": ssm_c_re, "ssm_c_im": ssm_c_im,
            "ssm_d": ssm_d, "ssm_w_glu": ssm_w_glu, "ssm_b_glu": ssm_b_glu,
            "q_norm": q_norm, "k_norm": k_norm, "out_norm_ssm": out_norm_ssm,
            "out_norm_attn": out_norm_attn, "w_out": w_out, "norm_ffn": norm_ffn,
            "w_up": w_up, "conv_w": conv_w, "conv_b": conv_b, "w_down": w_down}


def reference(x, c, positions, w_ada, b_ada, norm_mix, w_in, ssm_log_dt, ssm_a_re, ssm_a_im,
              ssm_b_re, ssm_b_im, ssm_c_re, ssm_c_im, ssm_d, ssm_w_glu, ssm_b_glu,
              q_norm, k_norm, out_norm_ssm, out_norm_attn, w_out, norm_ffn, w_up,
              conv_w, conv_b, w_down):
    bsz, L, _ = x.shape
    h = x.astype(jnp.float32)
    cs = jax.nn.silu(c.astype(jnp.float32))
    for l in range(DEPTH):
        mod = cs @ w_ada[l].astype(jnp.float32) + b_ada[l].astype(jnp.float32)
        sh1, sc1, g1, sh2, sc2, g2 = jnp.split(mod, 6, axis=-1)

        hn = _modulate(_rmsnorm(h, norm_mix[l]), sh1, sc1)
        proj = hn @ w_in[l].astype(jnp.float32)
        o = 0
        u = proj[..., o:o + D_SSM]; o += D_SSM
        q = proj[..., o:o + D_ATTN].reshape(bsz, L, N_HEADS, HEAD_DIM); o += D_ATTN
        k = proj[..., o:o + D_ATTN].reshape(bsz, L, N_HEADS, HEAD_DIM); o += D_ATTN
        v = proj[..., o:o + D_ATTN].reshape(bsz, L, N_HEADS, HEAD_DIM); o += D_ATTN
        qi = proj[..., o:o + IDX_HEADS * IDX_DIM].reshape(bsz, L, IDX_HEADS, IDX_DIM); o += IDX_HEADS * IDX_DIM
        ki = proj[..., o:o + IDX_DIM]; o += IDX_DIM
        wi = proj[..., o:o + IDX_HEADS] * (IDX_HEADS ** -0.5 * IDX_DIM ** -0.5)

        y_ssm = _s5_mixer(u, ssm_log_dt[l], ssm_a_re[l], ssm_a_im[l], ssm_b_re[l], ssm_b_im[l],
                          ssm_c_re[l], ssm_c_im[l], ssm_d[l], ssm_w_glu[l], ssm_b_glu[l])

        qr = _rope(_rmsnorm(q, q_norm[l]), positions)
        kr = _rope(_rmsnorm(k, k_norm[l]), positions)
        qir = _rope(qi, positions)
        kir = _rope(ki[:, :, None, :], positions)[:, :, 0, :]
        y_attn = _dsa_mixer(qr, kr, v, qir, kir, wi)

        y_mix = jnp.concatenate([_rmsnorm(y_ssm, out_norm_ssm[l]),
                                 _rmsnorm(y_attn, out_norm_attn[l])], axis=-1)
        h = h + g1[:, None, :] * (y_mix @ w_out[l].astype(jnp.float32))

        hn2 = _modulate(_rmsnorm(h, norm_ffn[l]), sh2, sc2)
        up = _causal_dwconv(hn2 @ w_up[l].astype(jnp.float32), conv_w[l], conv_b[l])
        gate, val = up[..., :D_FF], up[..., D_FF:]
        h = h + g2[:, None, :] * ((jax.nn.gelu(gate) * val) @ w_down[l].astype(jnp.float32))
    return h.astype(x.dtype)
```

```python
import functools
import math

import jax
import jax.numpy as jnp
from jax import lax
from jax.experimental import pallas as pl
from jax.experimental.pallas import tpu as pltpu

EPS = 1e-6
ROPE_THETA = 10000.0
IDX_HEADS = 8
IDX_DIM = 64
TOPK_MAX = 256
CONV_W = 3

LANES = 128
BF16_SUBLANES = 16
SSM_CHUNK = LANES
VMEM_LIMIT = 56 * 1024 * 1024
NEG_BIG = -1e30
INT_MIN = -(2 ** 31)

F32 = jnp.float32
BF16 = jnp.bfloat16
HIGHEST = lax.Precision.HIGHEST


def _params(*semantics):
    return pltpu.CompilerParams(dimension_semantics=semantics, vmem_limit_bytes=VMEM_LIMIT)


def _gelu(x):
    return 0.5 * x * (1.0 + jnp.tanh(math.sqrt(2.0 / math.pi) * (x + 0.044715 * (x * x * x))))


def _adaln_kernel(ct_ref, w_ref, b_ref, o_ref):
    ct = ct_ref[...]
    cs = ct * jax.nn.sigmoid(ct)
    w = w_ref[...]
    for b in range(ct.shape[1]):
        o_ref[b:b + 1, :] = jnp.sum(w * cs[:, b:b + 1], axis=0, keepdims=True) + b_ref[...]


def _adaln(c, w, bias):
    bsz, d = c.shape
    n = w.shape[1]
    tn = 1024 if n % 1024 == 0 else n
    return pl.pallas_call(
        _adaln_kernel,
        grid=(n // tn,),
        in_specs=[pl.BlockSpec((d, bsz), lambda j: (0, 0)),
                  pl.BlockSpec((d, tn), lambda j: (0, j)),
                  pl.BlockSpec((1, tn), lambda j: (0, j))],
        out_specs=pl.BlockSpec((bsz, tn), lambda j: (0, j)),
        out_shape=jax.ShapeDtypeStruct((bsz, n), F32),
        compiler_params=_params("parallel"),
        name="adaln",
    )(c.T, w, bias.reshape(1, n))


def _rmsnorm_rows(x, g):
    return x * lax.rsqrt(jnp.mean(x * x, axis=-1, keepdims=True) + EPS) * g


def _rope_t(x3, cos, sin):
    half = x3.shape[1] // 2
    x1, x2 = x3[:, :half, :], x3[:, half:, :]
    return jnp.concatenate([x1 * cos - x2 * sin, x1 * sin + x2 * cos], axis=1)


def _in_proj_kernel(d_ssm, d_attn, dh, wi_scale,
                    x_ref, g_ref, sh_ref, sc_ref, wt_ref, wn_ref, pos_ref, inv_ref, qg_ref, kg_ref,
                    ut_ref, q_ref, kt_ref, v_ref, qi_ref, kit_ref, wi_ref):
    hn = _rmsnorm_rows(x_ref[...], g_ref[...]) * (1.0 + sc_ref[0]) + sh_ref[0]
    hb = hn.astype(BF16)
    tm = hb.shape[0]

    def proj_t(r0, rows):
        return lax.dot_general(wt_ref[r0:r0 + rows, :], hb, (((1,), (1,)), ((), ())),
                               preferred_element_type=F32)

    ang = inv_ref[...] * pos_ref[...].astype(F32)
    cos, sin = jnp.cos(ang)[None], jnp.sin(ang)[None]

    def head_norm(x3, gcol):
        return x3 * lax.rsqrt(jnp.mean(x3 * x3, axis=1, keepdims=True) + EPS) * gcol[None]

    ut_ref[...] = proj_t(0, d_ssm)
    o = d_ssm
    q3 = proj_t(o, d_attn).reshape(d_attn // dh, dh, tm)
    q3 = _rope_t(head_norm(q3, qg_ref[...]), cos, sin) * (dh ** -0.5)
    q_ref[...] = q3.reshape(d_attn, tm).T.astype(BF16)
    o += d_attn
    k3 = proj_t(o, d_attn).reshape(d_attn // dh, dh, tm)
    kt_ref[0] = _rope_t(head_norm(k3, kg_ref[...]), cos, sin).reshape(d_attn, tm).astype(BF16)
    o += d_attn
    n_qi = IDX_HEADS * IDX_DIM
    qi3 = _rope_t(proj_t(o, n_qi).reshape(IDX_HEADS, IDX_DIM, tm), cos, sin)
    qi_ref[...] = qi3.reshape(n_qi, tm).T.astype(BF16)
    o += n_qi
    ki = _rope_t(proj_t(o, IDX_DIM)[None], cos, sin)[0].astype(BF16)
    kit_ref[0] = jnp.concatenate([ki, ki], axis=0)

    pn = jnp.dot(hb, wn_ref[...], preferred_element_type=F32)
    v_ref[...] = pn[:, :d_attn].astype(BF16)
    wi_ref[...] = pn[:, d_attn:] * wi_scale


def _in_proj(x2, g, sh, sc, w_in, pos_row, q_norm, k_norm, seq, tm, d_ssm, d_attn, dh):
    n, d = x2.shape
    bsz = n // seq
    tiles_per_batch = seq // tm
    n_qi = IDX_HEADS * IDX_DIM
    o_v = d_ssm + 2 * d_attn
    o_qi = o_v + d_attn
    o_ki = o_qi + n_qi
    o_wi = o_ki + IDX_DIM
    w = w_in.astype(BF16)
    wt = jnp.concatenate([w[:, :o_v], w[:, o_qi:o_wi]], axis=1).T
    wn = jnp.concatenate([w[:, o_v:o_qi], w[:, o_wi:],
                          jnp.zeros((d, LANES - IDX_HEADS), BF16)], axis=1)
    rt = wt.shape[0]
    inv = (ROPE_THETA ** (-jnp.arange(0, dh, 2, dtype=F32) / dh)).reshape(dh // 2, 1)
    wi_scale = IDX_HEADS ** -0.5 * IDX_DIM ** -0.5

    row = lambda i: (i, 0)
    const = lambda i: (0, 0)
    per_batch = lambda i: (i // tiles_per_batch, 0, 0)
    return pl.pallas_call(
        functools.partial(_in_proj_kernel, d_ssm, d_attn, dh, wi_scale),
        grid=(n // tm,),
        in_specs=[pl.BlockSpec((tm, d), row),
                  pl.BlockSpec((1, d), const),
                  pl.BlockSpec((1, 1, d), per_batch),
                  pl.BlockSpec((1, 1, d), per_batch),
                  pl.BlockSpec((rt, d), const),
                  pl.BlockSpec((d, d_attn + LANES), const),
                  pl.BlockSpec((1, tm), lambda i: (0, i)),
                  pl.BlockSpec((dh // 2, 1), const),
                  pl.BlockSpec((dh, 1), const),
                  pl.BlockSpec((dh, 1), const)],
        out_specs=[pl.BlockSpec((d_ssm, tm), lambda i: (0, i)),
                   pl.BlockSpec((tm, d_attn), row),
                   pl.BlockSpec((1, d_attn, tm), lambda i: (i, 0, 0)),
                   pl.BlockSpec((tm, d_attn), row),
                   pl.BlockSpec((tm, n_qi), row),
                   pl.BlockSpec((1, 2 * IDX_DIM, tm), lambda i: (i, 0, 0)),
                   pl.BlockSpec((tm, LANES), row)],
        out_shape=[jax.ShapeDtypeStruct((d_ssm, n), F32),
                   jax.ShapeDtypeStruct((n, d_attn), BF16),
                   jax.ShapeDtypeStruct((n // tm, d_attn, tm), BF16),
                   jax.ShapeDtypeStruct((n, d_attn), BF16),
                   jax.ShapeDtypeStruct((n, n_qi), BF16),
                   jax.ShapeDtypeStruct((n // tm, 2 * IDX_DIM, tm), BF16),
                   jax.ShapeDtypeStruct((n, LANES), F32)],
        compiler_params=_params("parallel"),
        name="in_proj",
    )(x2, g.reshape(1, d), sh.reshape(bsz, 1, d), sc.reshape(bsz, 1, d), wt, wn, pos_row, inv,
      q_norm.reshape(dh, 1), k_norm.reshape(dh, 1))


def _cmul(ar, ai, br, bi):
    return ar * br - ai * bi, ar * bi + ai * br


def _s5_kernel(chunks_per_batch,
               ut_ref, ldt_ref, ar_row_ref, ai_row_ref, ar_col_ref, ai_col_ref,
               btr_ref, bti_ref, cr_ref, ci_ref, ctr_ref, cti_ref, d_ref,
               y_ref, k2_scr, tm_scr, w_scr, v_scr):
    nc, p, t = btr_ref.shape[1], btr_ref.shape[2], SSM_CHUNK
    dt = jnp.exp(ldt_ref[0])
    lr, li = ar_row_ref[0], ai_row_ref[0]
    mag = jnp.exp(lr * dt)
    lbr, lbi = mag * jnp.cos(li * dt), mag * jnp.sin(li * dt)
    den = lr * lr + li * li
    nr, ni = lbr - 1.0, lbi
    coef_r, coef_i = (nr * lr + ni * li) / den, (ni * lr - nr * li) / den
    bbr, bbi = _cmul(coef_r, coef_i, btr_ref[0], bti_ref[0])

    def powers(rate, freq, n):
        m = jnp.exp(rate * n)
        return m * jnp.cos(freq * n), m * jnp.sin(freq * n)

    rate_c, freq_c = ar_col_ref[0] * dt, ai_col_ref[0] * dt
    rate_r, freq_r = lr * dt, li * dt
    tau = lax.broadcasted_iota(jnp.int32, (1, t), 1).astype(F32)
    pr0, pi0 = powers(rate_c, freq_c, tau)
    pr1, pi1 = powers(rate_c, freq_c, tau + 1.0)
    back = (t - 1) - lax.broadcasted_iota(jnp.int32, (t, 1), 0)
    qr, qi = powers(rate_r, freq_r, back.astype(F32))

    cr, ci = cr_ref[0], ci_ref[0]
    rows = []
    for c in range(nc):
        gr, gi = _cmul(cr[c:c + 1], ci[c:c + 1], bbr, bbi)
        rows.append(jnp.concatenate([gr, -gi], axis=1))
    k2_scr[...] = jnp.dot(jnp.concatenate(rows, axis=0), jnp.concatenate([pr0, pi0], axis=0),
                          precision=HIGHEST, preferred_element_type=F32)

    causal = (lax.broadcasted_iota(jnp.int32, (t, t), 1) >= lax.broadcasted_iota(jnp.int32, (t, t), 0))

    def build(cp, carry):
        r0 = pl.multiple_of(cp * t, t)
        for c in range(nc):
            krow = jnp.broadcast_to(k2_scr[pl.ds(c * nc + cp, 1), :], (t, t))
            tile = pltpu.roll(krow, 0, 1, stride=1, stride_axis=0)
            tm_scr[pl.ds(r0, t), c * t:(c + 1) * t] = jnp.where(causal, tile, 0.0).astype(BF16)
        return carry

    lax.fori_loop(0, nc, build, 0)

    a = jnp.concatenate([ut_ref[c] for c in range(nc)], axis=1)
    y = jnp.dot(a.astype(BF16), tm_scr[...], preferred_element_type=F32)

    for c in range(nc):
        wr, wi = _cmul(qr, qi, bbr[c:c + 1], bbi[c:c + 1])
        w_scr[c * t:(c + 1) * t, :] = jnp.concatenate([wr, wi], axis=1)
    s = jnp.dot(a, w_scr[...], precision=HIGHEST, preferred_element_type=F32)

    r = s.shape[0]
    kidx = lax.broadcasted_iota(jnp.int32, (r, 1), 0) % chunks_per_batch

    def times_lam_pow(z, n):
        zr, zi = powers(rate_r, freq_r, float(n))
        return (jnp.concatenate([zr, zr], axis=1) * z
                + jnp.concatenate([-zi, zi], axis=1) * pltpu.roll(z, p, 1))

    step = 1
    while step < chunks_per_batch:
        prev = pltpu.roll(s, step, 0)
        s = s + jnp.where(kidx >= step, times_lam_pow(prev, t * step), 0.0)
        step *= 2
    e = jnp.where(kidx >= 1, pltpu.roll(s, 1, 0), 0.0)

    ctr, cti = ctr_ref[0], cti_ref[0]
    for c in range(nc):
        vr, vi = _cmul(ctr[:, c:c + 1], cti[:, c:c + 1], pr1, pi1)
        v_scr[:, c * t:(c + 1) * t] = jnp.concatenate([vr, -vi], axis=0)
    y = y + jnp.dot(e, v_scr[...], precision=HIGHEST, preferred_element_type=F32)

    for c in range(nc):
        y_ref[c] = y[:, c * t:(c + 1) * t] + d_ref[c] * ut_ref[c]


def _s5(ut, seq, log_dt, a_re, a_im, b_re, b_im, c_re, c_im, d_skip):
    d_ssm, n = ut.shape
    g, p = a_re.shape
    nc = b_re.shape[2]
    t = SSM_CHUNK
    r = n // t
    ut3 = ut.reshape(d_ssm, r, t)
    grp = lambda i: (i, 0, 0)
    row_spec = pl.BlockSpec((1, 1, p), grp)
    col_spec = pl.BlockSpec((1, p, 1), grp)
    cp_spec = pl.BlockSpec((1, nc, p), grp)
    pc_spec = pl.BlockSpec((1, p, nc), grp)
    yt3 = pl.pallas_call(
        functools.partial(_s5_kernel, seq // t),
        grid=(g,),
        in_specs=[pl.BlockSpec((nc, r, t), grp),
                  pl.BlockSpec((1, 1, 1), grp),
                  row_spec, row_spec, col_spec, col_spec,
                  cp_spec, cp_spec, cp_spec, cp_spec, pc_spec, pc_spec,
                  pl.BlockSpec((nc, 1, 1), grp)],
        out_specs=pl.BlockSpec((nc, r, t), grp),
        out_shape=jax.ShapeDtypeStruct((d_ssm, r, t), F32),
        scratch_shapes=[pltpu.VMEM((nc * nc, t), F32),
                        pltpu.VMEM((nc * t, nc * t), BF16),
                        pltpu.VMEM((nc * t, 2 * p), F32),
                        pltpu.VMEM((2 * p, nc * t), F32)],
        compiler_params=_params("parallel"),
        name="s5",
    )(ut3, log_dt.reshape(g, 1, 1),
      a_re.reshape(g, 1, p), a_im.reshape(g, 1, p), a_re.reshape(g, p, 1), a_im.reshape(g, p, 1),
      jnp.swapaxes(b_re, 1, 2), jnp.swapaxes(b_im, 1, 2), c_re, c_im,
      jnp.swapaxes(c_re, 1, 2), jnp.swapaxes(c_im, 1, 2), d_skip.reshape(d_ssm, 1, 1))
    return yt3.reshape(d_ssm, n)


def _dsa_kernel(topk, tq, kc, dh,
                q_ref, qi_ref, wi_ref, kt_ref, kit_ref, v_ref, o_ref,
                qm_scr, qim_scr, key_scr, bias_scr, m_scr, l_scr, acc_scr):
    i = pl.program_id(1)
    n_heads = qm_scr.shape[0]
    n_chunks = (i * tq + tq + kc - 1) // kc
    low_half = lax.broadcasted_iota(jnp.int32, (tq, LANES), 1) < dh

    for h in range(n_heads):
        keep = low_half if h % 2 == 0 else jnp.logical_not(low_half)
        sl = slice((h // 2) * LANES, (h // 2 + 1) * LANES)
        qm_scr[h] = jnp.where(keep, q_ref[:, sl], 0).astype(BF16)
    for h in range(IDX_HEADS):
        keep = low_half if h % 2 == 0 else jnp.logical_not(low_half)
        sl = slice((h // 2) * LANES, (h // 2 + 1) * LANES)
        qim_scr[h] = jnp.where(keep, qi_ref[:, sl], 0).astype(BF16)

    qpos = i * tq + lax.broadcasted_iota(jnp.int32, (tq, 1), 0)
    lane = lax.broadcasted_iota(jnp.int32, (1, kc), 1)

    def index_chunk(c, carry):
        kit = kit_ref[c]
        wi = wi_ref[...]
        score = jnp.zeros((tq, kc), F32)
        for h in range(IDX_HEADS):
            s = jnp.dot(qim_scr[h], kit, preferred_element_type=F32)
            score = score + wi[:, h:h + 1] * jnp.maximum(s, 0.0)
        bits = lax.bitcast_convert_type(score, jnp.int32)
        key = jnp.where(bits >= 0, bits, bits ^ jnp.int32(0x7FFFFFFF))
        key_scr[c] = jnp.where(c * kc + lane <= qpos, key, INT_MIN)
        return carry

    lax.fori_loop(0, n_chunks, index_chunk, 0)

    def search_bit(it, base):
        cand = base + lax.shift_left(jnp.int32(1), 31 - it)

        def count_chunk(c, cnt):
            ge = (key_scr[c] >= cand).astype(F32)
            return cnt + sum(ge[:, j * LANES:(j + 1) * LANES] for j in range(kc // LANES))

        cnt = lax.fori_loop(0, n_chunks, count_chunk, jnp.zeros((tq, LANES), F32))
        return jnp.where(jnp.sum(cnt, axis=1, keepdims=True) >= topk, cand, base)

    thr = lax.fori_loop(0, 32, search_bit, jnp.full((tq, 1), INT_MIN, jnp.int32))
    thr = jnp.maximum(thr, INT_MIN + 1)

    def bias_chunk(c, carry):
        bias_scr[c] = jnp.where(key_scr[c] >= thr, 0.0, NEG_BIG)
        return carry

    lax.fori_loop(0, n_chunks, bias_chunk, 0)

    m_scr[...] = jnp.full(m_scr.shape, NEG_BIG, F32)
    l_scr[...] = jnp.zeros(l_scr.shape, F32)
    acc_scr[...] = jnp.zeros(acc_scr.shape, F32)

    def attend_chunk(c, carry):
        bias = bias_scr[c]
        r0 = pl.multiple_of(c * kc, kc)
        for j in range(n_heads // 2):
            sl = slice(j * LANES, (j + 1) * LANES)
            kt = kt_ref[c, sl, :]
            vv = v_ref[pl.ds(r0, kc), sl]
            alphas, pvs = [], []
            for h in (2 * j, 2 * j + 1):
                s = jnp.dot(qm_scr[h], kt, preferred_element_type=F32) + bias
                m_old = m_scr[h]
                m_new = jnp.maximum(m_old, jnp.max(s, axis=1, keepdims=True))
                alpha = jnp.exp(m_old - m_new)
                pexp = jnp.exp(s - m_new)
                l_scr[h] = alpha * l_scr[h] + jnp.sum(pexp, axis=1, keepdims=True)
                m_scr[h] = m_new
                alphas.append(alpha)
                pvs.append(jnp.dot(pexp.astype(BF16), vv, preferred_element_type=F32))
            acc_scr[j] = (jnp.where(low_half, alphas[0], alphas[1]) * acc_scr[j]
                          + jnp.where(low_half, pvs[0], pvs[1]))
        return carry

    lax.fori_loop(0, n_chunks, attend_chunk, 0)

    for j in range(n_heads // 2):
        denom = jnp.where(low_half, l_scr[2 * j], l_scr[2 * j + 1])
        o_ref[:, j * LANES:(j + 1) * LANES] = acc_scr[j] / denom


def _dsa(q, qi, wi, kt3, kit3, v, seq, tq, dh):
    n, d_attn = q.shape
    bsz = n // seq
    kc = kt3.shape[2]
    n_heads = d_attn // dh
    assert 2 * dh == LANES and 2 * IDX_DIM == LANES and n_heads % 2 == 0 and IDX_HEADS % 2 == 0
    topk = min(TOPK_MAX, seq // 4)
    blocks_per_batch = seq // tq
    chunks_per_batch = seq // kc
    qrow = lambda b, i: (b * blocks_per_batch + i, 0)
    return pl.pallas_call(
        functools.partial(_dsa_kernel, topk, tq, kc, dh),
        grid=(bsz, blocks_per_batch),
        in_specs=[pl.BlockSpec((tq, d_attn), qrow),
                  pl.BlockSpec((tq, IDX_HEADS * IDX_DIM), qrow),
                  pl.BlockSpec((tq, LANES), qrow),
                  pl.BlockSpec((chunks_per_batch, d_attn, kc), lambda b, i: (b, 0, 0)),
                  pl.BlockSpec((chunks_per_batch, 2 * IDX_DIM, kc), lambda b, i: (b, 0, 0)),
                  pl.BlockSpec((seq, d_attn), lambda b, i: (b, 0))],
        out_specs=pl.BlockSpec((tq, d_attn), qrow),
        out_shape=jax.ShapeDtypeStruct((n, d_attn), F32),
        scratch_shapes=[pltpu.VMEM((n_heads, tq, LANES), BF16),
                        pltpu.VMEM((IDX_HEADS, tq, LANES), BF16),
                        pltpu.VMEM((chunks_per_batch, tq, kc), jnp.int32),
                        pltpu.VMEM((chunks_per_batch, tq, kc), F32),
                        pltpu.VMEM((n_heads, tq, 1), F32),
                        pltpu.VMEM((n_heads, tq, 1), F32),
                        pltpu.VMEM((n_heads // 2, tq, LANES), F32)],
        compiler_params=_params("parallel", "arbitrary"),
        name="dsa",
    )(q, qi, wi, kt3, kit3, v)


def _mix_out_kernel(yt_ref, ya_ref, x_ref, g1_ref, sh_ref, sc_ref, gs_ref, ga_ref, wglu_ref, bglu_ref,
                    wos_ref, woa_ref, gf_ref, h_ref, hn_ref):
    y = _gelu(yt_ref[...])
    z = jnp.dot(wglu_ref[...], y.astype(BF16), preferred_element_type=F32) + bglu_ref[...]
    y = y * jax.nn.sigmoid(z)
    y = y * lax.rsqrt(jnp.mean(y * y, axis=0, keepdims=True) + EPS) * gs_ref[...]
    ya = _rmsnorm_rows(ya_ref[...], ga_ref[...])
    mix = (jnp.dot(y.T.astype(BF16), wos_ref[...], preferred_element_type=F32)
           + jnp.dot(ya.astype(BF16), woa_ref[...], preferred_element_type=F32))
    h = x_ref[...] + g1_ref[0] * mix
    h_ref[...] = h
    hn_ref[...] = (_rmsnorm_rows(h, gf_ref[...]) * (1.0 + sc_ref[0]) + sh_ref[0]).astype(BF16)


def _mix_out(yt, ya, x2, g1, sh2, sc2, gs, ga, w_glu, b_glu, w_out, gf, seq, tm):
    d_ssm, n = yt.shape
    d = x2.shape[1]
    d_attn = ya.shape[1]
    bsz = n // seq
    tiles_per_batch = seq // tm
    row = lambda i: (i, 0)
    const = lambda i: (0, 0)
    per_batch = lambda i: (i // tiles_per_batch, 0, 0)
    mod_spec = pl.BlockSpec((1, 1, d), per_batch)
    w_out = w_out.astype(BF16)
    return pl.pallas_call(
        _mix_out_kernel,
        grid=(n // tm,),
        in_specs=[pl.BlockSpec((d_ssm, tm), lambda i: (0, i)),
                  pl.BlockSpec((tm, d_attn), row),
                  pl.BlockSpec((tm, d), row),
                  mod_spec, mod_spec, mod_spec,
                  pl.BlockSpec((d_ssm, 1), const),
                  pl.BlockSpec((1, d_attn), const),
                  pl.BlockSpec((d_ssm, d_ssm), const),
                  pl.BlockSpec((d_ssm, 1), const),
                  pl.BlockSpec((d_ssm, d), const),
                  pl.BlockSpec((d_attn, d), const),
                  pl.BlockSpec((1, d), const)],
        out_specs=[pl.BlockSpec((tm, d), row), pl.BlockSpec((tm, d), row)],
        out_shape=[jax.ShapeDtypeStruct((n, d), F32), jax.ShapeDtypeStruct((n, d), BF16)],
        compiler_params=_params("parallel"),
        name="mix_out",
    )(yt, ya, x2, g1.reshape(bsz, 1, d), sh2.reshape(bsz, 1, d), sc2.reshape(bsz, 1, d),
      gs.reshape(d_ssm, 1), ga.reshape(1, d_attn), w_glu.T.astype(BF16), b_glu.reshape(d_ssm, 1),
      w_out[:d_ssm], w_out[d_ssm:], gf.reshape(1, d))


def _ffn_kernel(tiles_per_batch,
                hn_ref, halo_ref, h_ref, g2_ref, wg_ref, wv_ref, cwg_ref, cwv_ref, cbg_ref, cbv_ref, wd_ref,
                o_ref, ext_scr, acc_scr):
    i, j = pl.program_id(0), pl.program_id(1)
    tm = hn_ref.shape[0]
    halo_rows = halo_ref.shape[0]
    hn = hn_ref[...]
    halo = jnp.where(i % tiles_per_batch == 0, 0, halo_ref[...]).astype(BF16)

    def conv_up(w_ref, cw_ref, cb_ref):
        ext_scr[0:halo_rows, :] = jnp.dot(halo, w_ref[...], preferred_element_type=F32)
        cur = jnp.dot(hn, w_ref[...], preferred_element_type=F32)
        ext_scr[halo_rows:, :] = cur
        cw = cw_ref[...]
        out = cb_ref[...] + cw[CONV_W - 1:CONV_W] * cur
        for back in range(1, CONV_W):
            tap = CONV_W - 1 - back
            out = out + cw[tap:tap + 1] * ext_scr[pl.ds(halo_rows - back, tm), :]
        return out

    gate = conv_up(wg_ref, cwg_ref, cbg_ref)
    val = conv_up(wv_ref, cwv_ref, cbv_ref)
    act = (_gelu(gate) * val).astype(BF16)
    part = jnp.dot(act, wd_ref[...], preferred_element_type=F32)

    @pl.when(j == 0)
    def _():
        acc_scr[...] = part

    @pl.when(j > 0)
    def _():
        acc_scr[...] += part

    @pl.when(j == pl.num_programs(1) - 1)
    def _():
        o_ref[...] = h_ref[...] + g2_ref[0] * acc_scr[...]


def _ffn(hn2, h1, g2, w_up, conv_w, conv_b, w_down, seq, tm, tf):
    n, d = h1.shape
    d_ff = w_down.shape[0]
    bsz = n // seq
    nf = d_ff // tf
    tiles_per_batch = seq // tm
    halo_rows = BF16_SUBLANES
    assert halo_rows >= CONV_W - 1 and tm % halo_rows == 0
    halo_per_tile = tm // halo_rows
    row = lambda i, j: (i, 0)
    gate_col = lambda i, j: (0, j)
    val_col = lambda i, j: (0, nf + j)
    w_up = w_up.astype(BF16)
    return pl.pallas_call(
        functools.partial(_ffn_kernel, tiles_per_batch),
        grid=(n // tm, nf),
        in_specs=[pl.BlockSpec((tm, d), row),
                  pl.BlockSpec((halo_rows, d), lambda i, j: (jnp.maximum(i * halo_per_tile - 1, 0), 0)),
                  pl.BlockSpec((tm, d), row),
                  pl.BlockSpec((1, 1, d), lambda i, j: (i // tiles_per_batch, 0, 0)),
                  pl.BlockSpec((d, tf), gate_col),
                  pl.BlockSpec((d, tf), val_col),
                  pl.BlockSpec((CONV_W, tf), gate_col),
                  pl.BlockSpec((CONV_W, tf), val_col),
                  pl.BlockSpec((1, tf), gate_col),
                  pl.BlockSpec((1, tf), val_col),
                  pl.BlockSpec((tf, d), lambda i, j: (j, 0))],
        out_specs=pl.BlockSpec((tm, d), row),
        out_shape=jax.ShapeDtypeStruct((n, d), F32),
        scratch_shapes=[pltpu.VMEM((halo_rows + tm, tf), F32),
                        pltpu.VMEM((tm, d), F32)],
        compiler_params=_params("parallel", "arbitrary"),
        name="ffn",
    )(hn2, hn2, h1, g2.reshape(bsz, 1, d), w_up, w_up, conv_w, conv_w,
      conv_b.reshape(1, 2 * d_ff), conv_b.reshape(1, 2 * d_ff), w_down.astype(BF16))


def _tile(total, want):
    t = min(total, want)
    assert total % t == 0
    return t


def kernel(x, c, positions, w_ada, b_ada, norm_mix, w_in, ssm_log_dt, ssm_a_re, ssm_a_im, ssm_b_re, ssm_b_im, ssm_c_re, ssm_c_im, ssm_d, ssm_w_glu, ssm_b_glu, q_norm, k_norm, out_norm_ssm, out_norm_attn, w_out, norm_ffn, w_up, conv_w, conv_b, w_down):
    bsz, seq, d = x.shape
    depth = w_ada.shape[0]
    n = bsz * seq
    d_ssm = out_norm_ssm.shape[1]
    d_attn = out_norm_attn.shape[1]
    dh = q_norm.shape[1]
    d_ff = w_down.shape[1]
    assert w_in.shape[2] == d_ssm + 3 * d_attn + IDX_HEADS * IDX_DIM + IDX_DIM + IDX_HEADS
    assert seq % SSM_CHUNK == 0

    t_proj = _tile(seq, 512)
    t_q = _tile(seq, 128)
    t_mix = _tile(seq, 512)
    t_ffn = _tile(seq, 1024)
    t_ff = _tile(d_ff, 256)

    h = x.astype(F32).reshape(n, d)
    pos_row = positions.reshape(1, n)
    for l in range(depth):
        mod = _adaln(c.astype(F32), w_ada[l].astype(F32), b_ada[l].astype(F32))
        sh1, sc1, g1, sh2, sc2, g2 = jnp.split(mod, 6, axis=-1)
        ut, q, kt3, v, qi, kit3, wi = _in_proj(h, norm_mix[l], sh1, sc1, w_in[l], pos_row, q_norm[l], k_norm[l],
                                               seq, t_proj, d_ssm, d_attn, dh)
        yt = _s5(ut, seq, ssm_log_dt[l], ssm_a_re[l], ssm_a_im[l], ssm_b_re[l], ssm_b_im[l],
                 ssm_c_re[l], ssm_c_im[l], ssm_d[l])
        ya = _dsa(q, qi, wi, kt3, kit3, v, seq, t_q, dh)
        h1, hn2 = _mix_out(yt, ya, h, g1, sh2, sc2, out_norm_ssm[l], out_norm_attn[l], ssm_w_glu[l],
                           ssm_b_glu[l], w_out[l], norm_ffn[l], seq, t_mix)
        h = _ffn(hn2, h1, g2, w_up[l], conv_w[l], conv_b[l], w_down[l], seq, t_ffn, t_ff)
    return h.reshape(bsz, seq, d).astype(x.dtype)
```

```python
import functools
import math

import jax
import jax.numpy as jnp
from jax import lax
from jax.experimental import pallas as pl
from jax.experimental.pallas import tpu as pltpu

EPS = 1e-6
ROPE_THETA = 10000.0
IDX_HEADS = 8
IDX_DIM = 64
TOPK_MAX = 256
CONV_W = 3

LANES = 128
BF16_SUBLANES = 16
SSM_CHUNK = LANES
VMEM_LIMIT = 56 * 1024 * 1024
NEG_BIG = -1e30

F32 = jnp.float32
BF16 = jnp.bfloat16
HIGHEST = lax.Precision.HIGHEST


def _params(*semantics):
    return pltpu.CompilerParams(dimension_semantics=semantics, vmem_limit_bytes=VMEM_LIMIT)


def _gelu(x):
    return 0.5 * x * (1.0 + jnp.tanh(math.sqrt(2.0 / math.pi) * (x + 0.044715 * (x * x * x))))


def _adaln_kernel(ct_ref, w_ref, b_ref, o_ref):
    ct = ct_ref[...]
    cs = ct * jax.nn.sigmoid(ct)
    w = w_ref[...]
    for b in range(ct.shape[1]):
        o_ref[b:b + 1, :] = jnp.sum(w * cs[:, b:b + 1], axis=0, keepdims=True) + b_ref[...]


def _adaln(c, w, bias):
    bsz, d = c.shape
    n = w.shape[1]
    tn = 1024 if n % 1024 == 0 else n
    return pl.pallas_call(
        _adaln_kernel,
        grid=(n // tn,),
        in_specs=[pl.BlockSpec((d, bsz), lambda j: (0, 0)),
                  pl.BlockSpec((d, tn), lambda j: (0, j)),
                  pl.BlockSpec((1, tn), lambda j: (0, j))],
        out_specs=pl.BlockSpec((bsz, tn), lambda j: (0, j)),
        out_shape=jax.ShapeDtypeStruct((bsz, n), F32),
        compiler_params=_params("parallel"),
        name="adaln",
    )(c.T, w, bias.reshape(1, n))


def _rmsnorm_rows(x, g):
    return x * lax.rsqrt(jnp.mean(x * x, axis=-1, keepdims=True) + EPS) * g


def _rope_t(x3, cos, sin):
    half = x3.shape[1] // 2
    x1, x2 = x3[:, :half, :], x3[:, half:, :]
    return jnp.concatenate([x1 * cos - x2 * sin, x1 * sin + x2 * cos], axis=1)


def _in_proj_kernel(d_ssm, d_attn, dh, wi_scale,
                    x_ref, g_ref, sh_ref, sc_ref, wt_ref, pos_ref, inv_ref, qg_ref, kg_ref,
                    ut_ref, qt_ref, k_ref, vt_ref, qit_ref, ki_ref, wit_ref):
    hn = _rmsnorm_rows(x_ref[...], g_ref[...]) * (1.0 + sc_ref[0]) + sh_ref[0]
    hb = hn.astype(BF16)
    tm = hb.shape[0]

    def proj_t(r0, rows):
        return lax.dot_general(wt_ref[r0:r0 + rows, :], hb, (((1,), (1,)), ((), ())),
                               preferred_element_type=F32)

    ang = inv_ref[...] * pos_ref[...].astype(F32)
    cos, sin = jnp.cos(ang)[None], jnp.sin(ang)[None]

    def head_norm(x3, gcol):
        return x3 * lax.rsqrt(jnp.mean(x3 * x3, axis=1, keepdims=True) + EPS) * gcol[None]

    ut_ref[...] = proj_t(0, d_ssm)
    o = d_ssm
    q3 = proj_t(o, d_attn).reshape(d_attn // dh, dh, tm)
    q3 = _rope_t(head_norm(q3, qg_ref[...]), cos, sin) * (dh ** -0.5)
    qt_ref[...] = q3.reshape(d_attn, tm).astype(BF16)
    o += d_attn
    k3 = proj_t(o, d_attn).reshape(d_attn // dh, dh, tm)
    k_ref[...] = _rope_t(head_norm(k3, kg_ref[...]), cos, sin).reshape(d_attn, tm).T.astype(BF16)
    o += d_attn
    vt_ref[0] = proj_t(o, d_attn).astype(BF16)
    o += d_attn
    n_qi = IDX_HEADS * IDX_DIM
    qi3 = _rope_t(proj_t(o, n_qi).reshape(IDX_HEADS, IDX_DIM, tm), cos, sin)
    qit_ref[...] = qi3.reshape(n_qi, tm).astype(BF16)
    o += n_qi
    ki = _rope_t(proj_t(o, IDX_DIM)[None], cos, sin)[0]
    ki_ref[...] = jnp.concatenate([ki, ki], axis=0).T.astype(BF16)
    o += IDX_DIM
    wit_ref[...] = proj_t(o, BF16_SUBLANES)[:IDX_HEADS] * wi_scale


def _in_proj(x2, g, sh, sc, w_in, pos_row, q_norm, k_norm, seq, tm, d_ssm, d_attn, dh):
    n, d = x2.shape
    bsz = n // seq
    tiles_per_batch = seq // tm
    n_qi = IDX_HEADS * IDX_DIM
    wt = jnp.concatenate([w_in.astype(BF16).T, jnp.zeros((BF16_SUBLANES - IDX_HEADS, d), BF16)], axis=0)
    rt = wt.shape[0]
    inv = (ROPE_THETA ** (-jnp.arange(0, dh, 2, dtype=F32) / dh)).reshape(dh // 2, 1)
    wi_scale = IDX_HEADS ** -0.5 * IDX_DIM ** -0.5

    row = lambda i: (i, 0)
    col = lambda i: (0, i)
    const = lambda i: (0, 0)
    per_batch = lambda i: (i // tiles_per_batch, 0, 0)
    return pl.pallas_call(
        functools.partial(_in_proj_kernel, d_ssm, d_attn, dh, wi_scale),
        grid=(n // tm,),
        in_specs=[pl.BlockSpec((tm, d), row),
                  pl.BlockSpec((1, d), const),
                  pl.BlockSpec((1, 1, d), per_batch),
                  pl.BlockSpec((1, 1, d), per_batch),
                  pl.BlockSpec((rt, d), const),
                  pl.BlockSpec((1, tm), col),
                  pl.BlockSpec((dh // 2, 1), const),
                  pl.BlockSpec((dh, 1), const),
                  pl.BlockSpec((dh, 1), const)],
        out_specs=[pl.BlockSpec((d_ssm, tm), col),
                   pl.BlockSpec((d_attn, tm), col),
                   pl.BlockSpec((tm, d_attn), row),
                   pl.BlockSpec((1, d_attn, tm), lambda i: (i, 0, 0)),
                   pl.BlockSpec((n_qi, tm), col),
                   pl.BlockSpec((tm, 2 * IDX_DIM), row),
                   pl.BlockSpec((IDX_HEADS, tm), col)],
        out_shape=[jax.ShapeDtypeStruct((d_ssm, n), F32),
                   jax.ShapeDtypeStruct((d_attn, n), BF16),
                   jax.ShapeDtypeStruct((n, d_attn), BF16),
                   jax.ShapeDtypeStruct((n // tm, d_attn, tm), BF16),
                   jax.ShapeDtypeStruct((n_qi, n), BF16),
                   jax.ShapeDtypeStruct((n, 2 * IDX_DIM), BF16),
                   jax.ShapeDtypeStruct((IDX_HEADS, n), F32)],
        compiler_params=_params("parallel"),
        name="in_proj",
    )(x2, g.reshape(1, d), sh.reshape(bsz, 1, d), sc.reshape(bsz, 1, d), wt, pos_row, inv,
      q_norm.reshape(dh, 1), k_norm.reshape(dh, 1))


def _cmul(ar, ai, br, bi):
    return ar * br - ai * bi, ar * bi + ai * br


def _s5_kernel(chunks_per_batch,
               ut_ref, ldt_ref, ar_row_ref, ai_row_ref, ar_col_ref, ai_col_ref,
               btr_ref, bti_ref, cr_ref, ci_ref, ctr_ref, cti_ref, d_ref,
               y_ref, k2_scr, tm_scr, w_scr, v_scr):
    nc, p, t = btr_ref.shape[1], btr_ref.shape[2], SSM_CHUNK
    dt = jnp.exp(ldt_ref[0])
    lr, li = ar_row_ref[0], ai_row_ref[0]
    mag = jnp.exp(lr * dt)
    lbr, lbi = mag * jnp.cos(li * dt), mag * jnp.sin(li * dt)
    den = lr * lr + li * li
    nr, ni = lbr - 1.0, lbi
    coef_r, coef_i = (nr * lr + ni * li) / den, (ni * lr - nr * li) / den
    bbr, bbi = _cmul(coef_r, coef_i, btr_ref[0], bti_ref[0])

    def powers(rate, freq, n):
        m = jnp.exp(rate * n)
        return m * jnp.cos(freq * n), m * jnp.sin(freq * n)

    rate_c, freq_c = ar_col_ref[0] * dt, ai_col_ref[0] * dt
    rate_r, freq_r = lr * dt, li * dt
    tau = lax.broadcasted_iota(jnp.int32, (1, t), 1).astype(F32)
    pr0, pi0 = powers(rate_c, freq_c, tau)
    pr1, pi1 = powers(rate_c, freq_c, tau + 1.0)
    back = (t - 1) - lax.broadcasted_iota(jnp.int32, (t, 1), 0)
    qr, qi = powers(rate_r, freq_r, back.astype(F32))

    cr, ci = cr_ref[0], ci_ref[0]
    rows = []
    for c in range(nc):
        gr, gi = _cmul(cr[c:c + 1], ci[c:c + 1], bbr, bbi)
        rows.append(jnp.concatenate([gr, -gi], axis=1))
    k2_scr[...] = jnp.dot(jnp.concatenate(rows, axis=0), jnp.concatenate([pr0, pi0], axis=0),
                          precision=HIGHEST, preferred_element_type=F32)

    causal = (lax.broadcasted_iota(jnp.int32, (t, t), 1) >= lax.broadcasted_iota(jnp.int32, (t, t), 0))

    def build(cp, carry):
        r0 = pl.multiple_of(cp * t, t)
        for c in range(nc):
            krow = jnp.broadcast_to(k2_scr[pl.ds(c * nc + cp, 1), :], (t, t))
            tile = pltpu.roll(krow, 0, 1, stride=1, stride_axis=0)
            tm_scr[pl.ds(r0, t), c * t:(c + 1) * t] = jnp.where(causal, tile, 0.0).astype(BF16)
        return carry

    lax.fori_loop(0, nc, build, 0)

    a = jnp.concatenate([ut_ref[c] for c in range(nc)], axis=1)
    y = jnp.dot(a.astype(BF16), tm_scr[...], preferred_element_type=F32)

    for c in range(nc):
        wr, wi = _cmul(qr, qi, bbr[c:c + 1], bbi[c:c + 1])
        w_scr[c * t:(c + 1) * t, :] = jnp.concatenate([wr, wi], axis=1)
    s = jnp.dot(a, w_scr[...], precision=HIGHEST, preferred_element_type=F32)

    r = s.shape[0]
    kidx = lax.broadcasted_iota(jnp.int32, (r, 1), 0) % chunks_per_batch

    def times_lam_pow(z, n):
        zr, zi = powers(rate_r, freq_r, float(n))
        return (jnp.concatenate([zr, zr], axis=1) * z
                + jnp.concatenate([-zi, zi], axis=1) * pltpu.roll(z, p, 1))

    step = 1
    while step < chunks_per_batch:
        prev = pltpu.roll(s, step, 0)
        s = s + jnp.where(kidx >= step, times_lam_pow(prev, t * step), 0.0)
        step *= 2
    e = jnp.where(kidx >= 1, pltpu.roll(s, 1, 0), 0.0)

    ctr, cti = ctr_ref[0], cti_ref[0]
    for c in range(nc):
        vr, vi = _cmul(ctr[:, c:c + 1], cti[:, c:c + 1], pr1, pi1)
        v_scr[:, c * t:(c + 1) * t] = jnp.concatenate([vr, -vi], axis=0)
    y = y + jnp.dot(e, v_scr[...], precision=HIGHEST, preferred_element_type=F32)

    for c in range(nc):
        y_ref[c] = y[:, c * t:(c + 1) * t] + d_ref[c] * ut_ref[c]


def _s5(ut, seq, log_dt, a_re, a_im, b_re, b_im, c_re, c_im, d_skip):
    d_ssm, n = ut.shape
    g, p = a_re.shape
    nc = b_re.shape[2]
    t = SSM_CHUNK
    r = n // t
    ut3 = ut.reshape(d_ssm, r, t)
    grp = lambda i: (i, 0, 0)
    row_spec = pl.BlockSpec((1, 1, p), grp)
    col_spec = pl.BlockSpec((1, p, 1), grp)
    cp_spec = pl.BlockSpec((1, nc, p), grp)
    pc_spec = pl.BlockSpec((1, p, nc), grp)
    yt3 = pl.pallas_call(
        functools.partial(_s5_kernel, seq // t),
        grid=(g,),
        in_specs=[pl.BlockSpec((nc, r, t), grp),
                  pl.BlockSpec((1, 1, 1), grp),
                  row_spec, row_spec, col_spec, col_spec,
                  cp_spec, cp_spec, cp_spec, cp_spec, pc_spec, pc_spec,
                  pl.BlockSpec((nc, 1, 1), grp)],
        out_specs=pl.BlockSpec((nc, r, t), grp),
        out_shape=jax.ShapeDtypeStruct((d_ssm, r, t), F32),
        scratch_shapes=[pltpu.VMEM((nc * nc, t), F32),
                        pltpu.VMEM((nc * t, nc * t), BF16),
                        pltpu.VMEM((nc * t, 2 * p), F32),
                        pltpu.VMEM((2 * p, nc * t), F32)],
        compiler_params=_params("parallel"),
        name="s5",
    )(ut3, log_dt.reshape(g, 1, 1),
      a_re.reshape(g, 1, p), a_im.reshape(g, 1, p), a_re.reshape(g, p, 1), a_im.reshape(g, p, 1),
      jnp.swapaxes(b_re, 1, 2), jnp.swapaxes(b_im, 1, 2), c_re, c_im,
      jnp.swapaxes(c_re, 1, 2), jnp.swapaxes(c_im, 1, 2), d_skip.reshape(d_ssm, 1, 1))
    return yt3.reshape(d_ssm, n)


def _float_key(x):
    bits = lax.bitcast_convert_type(x, jnp.int32)
    return jnp.where(bits >= 0, bits, bits ^ jnp.int32(0x7FFFFFFF))


def _key_float(k):
    return lax.bitcast_convert_type(jnp.where(k >= 0, k, k ^ jnp.int32(0x7FFFFFFF)), F32)


def _dsa_kernel(topk, seq, tq, kc, dh, value_passes, max_passes,
                qt_ref, qit_ref, wit_ref, k_ref, ki_ref, vt_ref, o_ref,
                qm_scr, qim_scr, i_scr, m_scr, l_scr, acc_scr):
    i = pl.program_id(1)
    n_heads = qm_scr.shape[0]
    n_chunks = (i * tq + tq + kc - 1) // kc
    low_rows = lax.broadcasted_iota(jnp.int32, (LANES, 1), 0) < dh

    def split_heads(src_ref, dst_scr, heads):
        for h in range(heads):
            keep = low_rows if h % 2 == 0 else jnp.logical_not(low_rows)
            rows = slice((h // 2) * LANES, (h // 2 + 1) * LANES)
            dst_scr[h] = jnp.where(keep, src_ref[rows, :].astype(F32), 0.0).astype(BF16)

    split_heads(qt_ref, qm_scr, n_heads)
    split_heads(qit_ref, qim_scr, IDX_HEADS)

    qpos = i * tq + lax.broadcasted_iota(jnp.int32, (1, tq), 1)
    krow = lax.broadcasted_iota(jnp.int32, (kc, 1), 0)
    wi = wit_ref[...]

    def index_chunk(c, carry):
        lo, hi = carry
        r0 = pl.multiple_of(c * kc, kc)
        ki = ki_ref[pl.ds(r0, kc), :]
        score = jnp.zeros((kc, tq), F32)
        for h in range(IDX_HEADS):
            s = jnp.dot(ki, qim_scr[h], preferred_element_type=F32)
            score = score + wi[h:h + 1, :] * jnp.maximum(s, 0.0)
        i_scr[c] = jnp.where(r0 + krow <= qpos, score, -jnp.inf)
        return (jnp.minimum(lo, jnp.min(score, axis=0, keepdims=True)),
                jnp.maximum(hi, jnp.max(score, axis=0, keepdims=True)))

    lo, hi = lax.fori_loop(0, n_chunks, index_chunk,
                           (jnp.full((1, tq), jnp.inf, F32), jnp.full((1, tq), -jnp.inf, F32)))

    def count(pred):
        def body(c, cnt):
            hit = jnp.where(pred(i_scr[c]), 1.0, 0.0)
            return cnt + jnp.sum(hit.reshape(kc // 8, 8, tq), axis=0)
        return jnp.sum(lax.fori_loop(0, n_chunks, body, jnp.zeros((8, tq), F32)), axis=0, keepdims=True)

    kf = float(topk)
    n_causal = (qpos + 1).astype(F32)

    def searching(st):
        it, _, _, _, active = st
        return jnp.logical_and(it < max_passes, jnp.max(active) > 0.0)

    def search_pass(st):
        it, lo, hi, c_lo, active = st
        mid_v = 0.5 * lo + 0.5 * hi
        ka, kb = _float_key(lo), _float_key(hi)
        mid_b = _key_float((ka >> 1) + (kb >> 1) + (ka & kb & 1))
        use_v = jnp.logical_and(it < value_passes, jnp.logical_and(mid_v > lo, mid_v < hi))
        mid = jnp.where(use_v, mid_v, mid_b)
        inside = jnp.logical_and(mid > lo, mid < hi)
        c = count(lambda x: x >= mid)
        upd = jnp.logical_and(active > 0.0, inside)
        up_lo = jnp.logical_and(upd, c >= kf)
        up_hi = jnp.logical_and(upd, c < kf)
        lo = jnp.where(up_lo, mid, lo)
        c_lo = jnp.where(up_lo, c, c_lo)
        hi = jnp.where(up_hi, mid, hi)
        active = jnp.where(jnp.logical_and(upd, c_lo > kf), 1.0, 0.0)
        return it + 1, lo, hi, c_lo, active

    active0 = jnp.where(jnp.logical_and(n_causal > kf, hi > lo), 1.0, 0.0)
    _, thr, _, c_thr, _ = lax.while_loop(searching, search_pass, (jnp.int32(0), lo, hi, n_causal, active0))
    tied = jnp.max(jnp.where(c_thr > kf, 1.0, 0.0)) > 0.0

    @pl.when(jnp.logical_not(tied))
    def _():
        def mask_chunk(c, carry):
            i_scr[c] = jnp.where(i_scr[c] >= thr, 0.0, NEG_BIG)
            return carry
        lax.fori_loop(0, n_chunks, mask_chunk, 0)

    @pl.when(tied)
    def _():
        above = count(lambda x: x > thr)
        room = jnp.where(c_thr > kf, kf - above, float(seq))
        tri = (lax.broadcasted_iota(jnp.int32, (kc, kc), 0)
               >= lax.broadcasted_iota(jnp.int32, (kc, kc), 1)).astype(BF16)

        def mask_chunk(c, seen):
            x = i_scr[c]
            eq = jnp.where(x == thr, 1.0, 0.0)
            rank = jnp.dot(tri, eq.astype(BF16), preferred_element_type=F32) + seen
            keep = jnp.logical_or(x > thr, jnp.logical_and(x == thr, rank <= room))
            i_scr[c] = jnp.where(keep, 0.0, NEG_BIG)
            return seen + jnp.sum(eq, axis=0, keepdims=True)
        lax.fori_loop(0, n_chunks, mask_chunk, jnp.zeros((1, tq), F32))

    m_scr[...] = jnp.full(m_scr.shape, NEG_BIG, F32)
    l_scr[...] = jnp.zeros(l_scr.shape, F32)
    acc_scr[...] = jnp.zeros(acc_scr.shape, F32)

    def attend_chunk(c, carry):
        r0 = pl.multiple_of(c * kc, kc)
        mask = i_scr[c]
        for j in range(n_heads // 2):
            kk = k_ref[pl.ds(r0, kc), j * LANES:(j + 1) * LANES]
            for h in (2 * j, 2 * j + 1):
                s = jnp.dot(kk, qm_scr[h], preferred_element_type=F32) + mask
                m_old = m_scr[h:h + 1, :]
                m_new = jnp.maximum(m_old, jnp.max(s, axis=0, keepdims=True))
                alpha = jnp.exp(m_old - m_new)
                p = jnp.exp(s - m_new)
                l_scr[h:h + 1, :] = alpha * l_scr[h:h + 1, :] + jnp.sum(p, axis=0, keepdims=True)
                m_scr[h:h + 1, :] = m_new
                pv = jnp.dot(vt_ref[c, h * dh:(h + 1) * dh, :], p.astype(BF16),
                             preferred_element_type=F32)
                acc_scr[h] = alpha * acc_scr[h] + pv
        return carry

    lax.fori_loop(0, n_chunks, attend_chunk, 0)

    for h in range(n_heads):
        o_ref[h * dh:(h + 1) * dh, :] = acc_scr[h] / l_scr[h:h + 1, :]


def _dsa(qt, qit, wit, k, ki, vt3, seq, tq, dh):
    d_attn, n = qt.shape
    bsz = n // seq
    kc = vt3.shape[2]
    n_heads = d_attn // dh
    n_qi = IDX_HEADS * IDX_DIM
    assert 2 * dh == LANES and 2 * IDX_DIM == LANES and n_heads % 2 == 0 and IDX_HEADS % 2 == 0
    topk = min(TOPK_MAX, seq // 4)
    blocks_per_batch = seq // tq
    chunks_per_batch = seq // kc
    value_passes = 12
    max_passes = value_passes + 33
    qcol = lambda b, i: (0, b * blocks_per_batch + i)
    return pl.pallas_call(
        functools.partial(_dsa_kernel, topk, seq, tq, kc, dh, value_passes, max_passes),
        grid=(bsz, blocks_per_batch),
        in_specs=[pl.BlockSpec((d_attn, tq), qcol),
                  pl.BlockSpec((n_qi, tq), qcol),
                  pl.BlockSpec((IDX_HEADS, tq), qcol),
                  pl.BlockSpec((seq, d_attn), lambda b, i: (b, 0)),
                  pl.BlockSpec((seq, 2 * IDX_DIM), lambda b, i: (b, 0)),
                  pl.BlockSpec((chunks_per_batch, d_attn, kc), lambda b, i: (b, 0, 0))],
        out_specs=pl.BlockSpec((d_attn, tq), qcol),
        out_shape=jax.ShapeDtypeStruct((d_attn, n), F32),
        scratch_shapes=[pltpu.VMEM((n_heads, LANES, tq), BF16),
                        pltpu.VMEM((IDX_HEADS, LANES, tq), BF16),
                        pltpu.VMEM((chunks_per_batch, kc, tq), F32),
                        pltpu.VMEM((n_heads, tq), F32),
                        pltpu.VMEM((n_heads, tq), F32),
                        pltpu.VMEM((n_heads, dh, tq), F32)],
        compiler_params=_params("parallel", "arbitrary"),
        name="dsa",
    )(qt, qit, wit, k, ki, vt3)


def _mix_out_kernel(yt_ref, yat_ref, x_ref, g1_ref, sh_ref, sc_ref, gs_ref, ga_ref, wglu_ref, bglu_ref,
                    wo_ref, gf_ref, h_ref, hn_ref):
    def norm_cols(v, g):
        return v * lax.rsqrt(jnp.mean(v * v, axis=0, keepdims=True) + EPS) * g

    y = _gelu(yt_ref[...])
    z = jnp.dot(wglu_ref[...], y.astype(BF16), preferred_element_type=F32) + bglu_ref[...]
    y = y * jax.nn.sigmoid(z)
    mix_t = jnp.concatenate([norm_cols(y, gs_ref[...]), norm_cols(yat_ref[...], ga_ref[...])], axis=0)
    mix = jnp.dot(mix_t.T.astype(BF16), wo_ref[...], preferred_element_type=F32)
    h = x_ref[...] + g1_ref[0] * mix
    h_ref[...] = h
    hn_ref[...] = (_rmsnorm_rows(h, gf_ref[...]) * (1.0 + sc_ref[0]) + sh_ref[0]).astype(BF16)


def _mix_out(yt, yat, x2, g1, sh2, sc2, gs, ga, w_glu, b_glu, w_out, gf, seq, tm):
    d_ssm, n = yt.shape
    d = x2.shape[1]
    d_attn = yat.shape[0]
    bsz = n // seq
    tiles_per_batch = seq // tm
    row = lambda i: (i, 0)
    col = lambda i: (0, i)
    const = lambda i: (0, 0)
    per_batch = lambda i: (i // tiles_per_batch, 0, 0)
    mod_spec = pl.BlockSpec((1, 1, d), per_batch)
    return pl.pallas_call(
        _mix_out_kernel,
        grid=(n // tm,),
        in_specs=[pl.BlockSpec((d_ssm, tm), col),
                  pl.BlockSpec((d_attn, tm), col),
                  pl.BlockSpec((tm, d), row),
                  mod_spec, mod_spec, mod_spec,
                  pl.BlockSpec((d_ssm, 1), const),
                  pl.BlockSpec((d_attn, 1), const),
                  pl.BlockSpec((d_ssm, d_ssm), const),
                  pl.BlockSpec((d_ssm, 1), const),
                  pl.BlockSpec((d_ssm + d_attn, d), const),
                  pl.BlockSpec((1, d), const)],
        out_specs=[pl.BlockSpec((tm, d), row), pl.BlockSpec((tm, d), row)],
        out_shape=[jax.ShapeDtypeStruct((n, d), F32), jax.ShapeDtypeStruct((n, d), BF16)],
        compiler_params=_params("parallel"),
        name="mix_out",
    )(yt, yat, x2, g1.reshape(bsz, 1, d), sh2.reshape(bsz, 1, d), sc2.reshape(bsz, 1, d),
      gs.reshape(d_ssm, 1), ga.reshape(d_attn, 1), w_glu.T.astype(BF16), b_glu.reshape(d_ssm, 1),
      w_out.astype(BF16), gf.reshape(1, d))


def _ffn_kernel(tiles_per_batch,
                hn_ref, halo_ref, h_ref, g2_ref, wg_ref, wv_ref, cwg_ref, cwv_ref, cbg_ref, cbv_ref, wd_ref,
                o_ref, ext_scr, acc_scr):
    i, j = pl.program_id(0), pl.program_id(1)
    tm = hn_ref.shape[0]
    halo_rows = halo_ref.shape[0]
    hn = hn_ref[...]
    halo = jnp.where(i % tiles_per_batch == 0, 0, halo_ref[...]).astype(BF16)

    def conv_up(w_ref, cw_ref, cb_ref):
        ext_scr[0:halo_rows, :] = jnp.dot(halo, w_ref[...], preferred_element_type=F32)
        cur = jnp.dot(hn, w_ref[...], preferred_element_type=F32)
        ext_scr[halo_rows:, :] = cur
        cw = cw_ref[...]
        out = cb_ref[...] + cw[CONV_W - 1:CONV_W] * cur
        for back in range(1, CONV_W):
            tap = CONV_W - 1 - back
            out = out + cw[tap:tap + 1] * ext_scr[pl.ds(halo_rows - back, tm), :]
        return out

    gate = conv_up(wg_ref, cwg_ref, cbg_ref)
    val = conv_up(wv_ref, cwv_ref, cbv_ref)
    act = (_gelu(gate) * val).astype(BF16)
    part = jnp.dot(act, wd_ref[...], preferred_element_type=F32)

    @pl.when(j == 0)
    def _():
        acc_scr[...] = part

    @pl.when(j > 0)
    def _():
        acc_scr[...] += part

    @pl.when(j == pl.num_programs(1) - 1)
    def _():
        o_ref[...] = h_ref[...] + g2_ref[0] * acc_scr[...]


def _ffn(hn2, h1, g2, w_up, conv_w, conv_b, w_down, seq, tm, tf):
    n, d = h1.shape
    d_ff = w_down.shape[0]
    bsz = n // seq
    nf = d_ff // tf
    tiles_per_batch = seq // tm
    halo_rows = BF16_SUBLANES
    assert halo_rows >= CONV_W - 1 and tm % halo_rows == 0
    halo_per_tile = tm // halo_rows
    row = lambda i, j: (i, 0)
    gate_col = lambda i, j: (0, j)
    val_col = lambda i, j: (0, nf + j)
    w_up = w_up.astype(BF16)
    return pl.pallas_call(
        functools.partial(_ffn_kernel, tiles_per_batch),
        grid=(n // tm, nf),
        in_specs=[pl.BlockSpec((tm, d), row),
                  pl.BlockSpec((halo_rows, d), lambda i, j: (jnp.maximum(i * halo_per_tile - 1, 0), 0)),
                  pl.BlockSpec((tm, d), row),
                  pl.BlockSpec((1, 1, d), lambda i, j: (i // tiles_per_batch, 0, 0)),
                  pl.BlockSpec((d, tf), gate_col),
                  pl.BlockSpec((d, tf), val_col),
                  pl.BlockSpec((CONV_W, tf), gate_col),
                  pl.BlockSpec((CONV_W, tf), val_col),
                  pl.BlockSpec((1, tf), gate_col),
                  pl.BlockSpec((1, tf), val_col),
                  pl.BlockSpec((tf, d), lambda i, j: (j, 0))],
        out_specs=pl.BlockSpec((tm, d), row),
        out_shape=jax.ShapeDtypeStruct((n, d), F32),
        scratch_shapes=[pltpu.VMEM((halo_rows + tm, tf), F32),
                        pltpu.VMEM((tm, d), F32)],
        compiler_params=_params("parallel", "arbitrary"),
        name="ffn",
    )(hn2, hn2, h1, g2.reshape(bsz, 1, d), w_up, w_up, conv_w, conv_w,
      conv_b.reshape(1, 2 * d_ff), conv_b.reshape(1, 2 * d_ff), w_down.astype(BF16))


def _tile(total, want):
    t = min(total, want)
    assert total % t == 0
    return t


def kernel(x, c, positions, w_ada, b_ada, norm_mix, w_in, ssm_log_dt, ssm_a_re, ssm_a_im, ssm_b_re, ssm_b_im, ssm_c_re, ssm_c_im, ssm_d, ssm_w_glu, ssm_b_glu, q_norm, k_norm, out_norm_ssm, out_norm_attn, w_out, norm_ffn, w_up, conv_w, conv_b, w_down):
    bsz, seq, d = x.shape
    depth = w_ada.shape[0]
    n = bsz * seq
    d_ssm = out_norm_ssm.shape[1]
    d_attn = out_norm_attn.shape[1]
    dh = q_norm.shape[1]
    d_ff = w_down.shape[1]
    assert w_in.shape[2] == d_ssm + 3 * d_attn + IDX_HEADS * IDX_DIM + IDX_DIM + IDX_HEADS
    assert seq % SSM_CHUNK == 0

    t_proj = _tile(seq, 512)
    t_q = _tile(seq, 256)
    t_mix = _tile(seq, 512)
    t_ffn = _tile(seq, 1024)
    t_ff = _tile(d_ff, 256)

    h = x.astype(F32).reshape(n, d)
    pos_row = positions.reshape(1, n)
    for l in range(depth):
        mod = _adaln(c.astype(F32), w_ada[l].astype(F32), b_ada[l].astype(F32))
        sh1, sc1, g1, sh2, sc2, g2 = jnp.split(mod, 6, axis=-1)
        ut, qt, k, vt3, qit, ki, wit = _in_proj(h, norm_mix[l], sh1, sc1, w_in[l], pos_row, q_norm[l], k_norm[l],
                                                seq, t_proj, d_ssm, d_attn, dh)
        yt = _s5(ut, seq, ssm_log_dt[l], ssm_a_re[l], ssm_a_im[l], ssm_b_re[l], ssm_b_im[l],
                 ssm_c_re[l], ssm_c_im[l], ssm_d[l])
        yat = _dsa(qt, qit, wit, k, ki, vt3, seq, t_q, dh)
        h1, hn2 = _mix_out(yt, yat, h, g1, sh2, sc2, out_norm_ssm[l], out_norm_attn[l], ssm_w_glu[l],
                           ssm_b_glu[l], w_out[l], norm_ffn[l], seq, t_mix)
        h = _ffn(hn2, h1, g2, w_up[l], conv_w[l], conv_b[l], w_down[l], seq, t_ffn, t_ff)
    return h.reshape(bsz, seq, d).astype(x.dtype)
```

```python
import functools
import math

import jax
import jax.numpy as jnp
from jax import lax
from jax.experimental import pallas as pl
from jax.experimental.pallas import tpu as pltpu

EPS = 1e-6
ROPE_THETA = 10000.0
IDX_HEADS = 8
IDX_DIM = 64
TOPK_MAX = 256
CONV_W = 3

LANES = 128
BF16_SUBLANES = 16
SSM_CHUNK = LANES
PART_ROWS = 32
VMEM_LIMIT = 56 * 1024 * 1024
NEG_BIG = -1e30
LOG2E = math.log2(math.e)

F32 = jnp.float32
BF16 = jnp.bfloat16
HIGHEST = lax.Precision.HIGHEST


def _params(*semantics):
    return pltpu.CompilerParams(dimension_semantics=semantics, vmem_limit_bytes=VMEM_LIMIT)


def _gelu(x):
    return 0.5 * x * (1.0 + jnp.tanh(math.sqrt(2.0 / math.pi) * (x + 0.044715 * (x * x * x))))


def _adaln_kernel(ct_ref, w_ref, b_ref, o_ref):
    ct = ct_ref[...]
    cs = ct * jax.nn.sigmoid(ct)
    w = w_ref[...]
    for b in range(ct.shape[1]):
        o_ref[b:b + 1, :] = jnp.sum(w * cs[:, b:b + 1], axis=0, keepdims=True) + b_ref[...]


def _adaln(c, w, bias):
    bsz, d = c.shape
    n = w.shape[1]
    tn = 1024 if n % 1024 == 0 else n
    return pl.pallas_call(
        _adaln_kernel,
        grid=(n // tn,),
        in_specs=[pl.BlockSpec((d, bsz), lambda j: (0, 0)),
                  pl.BlockSpec((d, tn), lambda j: (0, j)),
                  pl.BlockSpec((1, tn), lambda j: (0, j))],
        out_specs=pl.BlockSpec((bsz, tn), lambda j: (0, j)),
        out_shape=jax.ShapeDtypeStruct((bsz, n), F32),
        compiler_params=_params("parallel"),
        name="adaln",
    )(c.T, w, bias.reshape(1, n))


def _rmsnorm_rows(x, g):
    return x * lax.rsqrt(jnp.mean(x * x, axis=-1, keepdims=True) + EPS) * g


def _rope_t(x3, cos, sin):
    half = x3.shape[1] // 2
    x1, x2 = x3[:, :half, :], x3[:, half:, :]
    return jnp.concatenate([x1 * cos - x2 * sin, x1 * sin + x2 * cos], axis=1)


def _in_proj_kernel(d_ssm, d_attn, dh, wi_scale,
                    x_ref, g_ref, sh_ref, sc_ref, wt_ref, pos_ref, inv_ref, qg_ref, kg_ref,
                    ut_ref, qt_ref, k_ref, ksq_ref, vt_ref, qit_ref, ki_ref, wit_ref):
    hn = _rmsnorm_rows(x_ref[...], g_ref[...]) * (1.0 + sc_ref[0]) + sh_ref[0]
    hb = hn.astype(BF16)
    tm = hb.shape[0]

    def proj_t(r0, rows):
        return lax.dot_general(wt_ref[r0:r0 + rows, :], hb, (((1,), (1,)), ((), ())),
                               preferred_element_type=F32)

    ang = inv_ref[...] * pos_ref[...].astype(F32)
    cos, sin = jnp.cos(ang)[None], jnp.sin(ang)[None]

    def head_norm(x3, gcol):
        return x3 * lax.rsqrt(jnp.mean(x3 * x3, axis=1, keepdims=True) + EPS) * gcol[None]

    ut_ref[...] = proj_t(0, d_ssm)
    o = d_ssm
    n_heads = d_attn // dh
    q3 = proj_t(o, d_attn).reshape(n_heads, dh, tm)
    q3 = _rope_t(head_norm(q3, qg_ref[...]), cos, sin) * (dh ** -0.5 * LOG2E)
    qt_ref[...] = q3.reshape(d_attn, tm).astype(BF16)
    o += d_attn
    k3 = _rope_t(head_norm(proj_t(o, d_attn).reshape(n_heads, dh, tm), kg_ref[...]), cos, sin)
    k_ref[...] = k3.reshape(d_attn, tm).T.astype(BF16)
    ksq = jnp.max(jnp.sum(k3 * k3, axis=1), axis=1, keepdims=True)
    ksq_ref[...] = jnp.broadcast_to(ksq, ksq_ref.shape)
    o += d_attn
    v3 = proj_t(o, d_attn).reshape(n_heads, dh, tm)
    vt_ref[0] = jnp.concatenate([v3, jnp.ones((n_heads, BF16_SUBLANES, tm), F32)], axis=1).astype(BF16)
    o += d_attn
    n_qi = IDX_HEADS * IDX_DIM
    qi3 = _rope_t(proj_t(o, n_qi).reshape(IDX_HEADS, IDX_DIM, tm), cos, sin)
    qit_ref[...] = qi3.reshape(n_qi, tm).astype(BF16)
    o += n_qi
    ki = _rope_t(proj_t(o, IDX_DIM)[None], cos, sin)[0]
    ki_ref[...] = jnp.concatenate([ki, ki], axis=0).T.astype(BF16)
    o += IDX_DIM
    wit_ref[...] = proj_t(o, BF16_SUBLANES)[:IDX_HEADS] * wi_scale


def _in_proj(x2, g, sh, sc, w_in, pos_row, q_norm, k_norm, seq, tm, d_ssm, d_attn, dh):
    n, d = x2.shape
    bsz = n // seq
    tiles_per_batch = seq // tm
    n_qi = IDX_HEADS * IDX_DIM
    n_heads = d_attn // dh
    wt = jnp.concatenate([w_in.astype(BF16).T, jnp.zeros((BF16_SUBLANES - IDX_HEADS, d), BF16)], axis=0)
    rt = wt.shape[0]
    inv = (ROPE_THETA ** (-jnp.arange(0, dh, 2, dtype=F32) / dh)).reshape(dh // 2, 1)
    wi_scale = IDX_HEADS ** -0.5 * IDX_DIM ** -0.5

    row = lambda i: (i, 0)
    col = lambda i: (0, i)
    const = lambda i: (0, 0)
    per_batch = lambda i: (i // tiles_per_batch, 0, 0)
    return pl.pallas_call(
        functools.partial(_in_proj_kernel, d_ssm, d_attn, dh, wi_scale),
        grid=(n // tm,),
        in_specs=[pl.BlockSpec((tm, d), row),
                  pl.BlockSpec((1, d), const),
                  pl.BlockSpec((1, 1, d), per_batch),
                  pl.BlockSpec((1, 1, d), per_batch),
                  pl.BlockSpec((rt, d), const),
                  pl.BlockSpec((1, tm), col),
                  pl.BlockSpec((dh // 2, 1), const),
                  pl.BlockSpec((dh, 1), const),
                  pl.BlockSpec((dh, 1), const)],
        out_specs=[pl.BlockSpec((d_ssm, tm), col),
                   pl.BlockSpec((d_attn, tm), col),
                   pl.BlockSpec((tm, d_attn), row),
                   pl.BlockSpec((n_heads, LANES), row),
                   pl.BlockSpec((1, n_heads, dh + BF16_SUBLANES, tm), lambda i: (i, 0, 0, 0)),
                   pl.BlockSpec((n_qi, tm), col),
                   pl.BlockSpec((tm, 2 * IDX_DIM), row),
                   pl.BlockSpec((IDX_HEADS, tm), col)],
        out_shape=[jax.ShapeDtypeStruct((d_ssm, n), F32),
                   jax.ShapeDtypeStruct((d_attn, n), BF16),
                   jax.ShapeDtypeStruct((n, d_attn), BF16),
                   jax.ShapeDtypeStruct((n // tm * n_heads, LANES), F32),
                   jax.ShapeDtypeStruct((n // tm, n_heads, dh + BF16_SUBLANES, tm), BF16),
                   jax.ShapeDtypeStruct((n_qi, n), BF16),
                   jax.ShapeDtypeStruct((n, 2 * IDX_DIM), BF16),
                   jax.ShapeDtypeStruct((IDX_HEADS, n), F32)],
        compiler_params=_params("parallel"),
        name="in_proj",
    )(x2, g.reshape(1, d), sh.reshape(bsz, 1, d), sc.reshape(bsz, 1, d), wt, pos_row, inv,
      q_norm.reshape(dh, 1), k_norm.reshape(dh, 1))


def _cmul(ar, ai, br, bi):
    return ar * br - ai * bi, ar * bi + ai * br


def _s5_kernel(chunks_per_batch,
               ut_ref, ldt_ref, ar_row_ref, ai_row_ref, ar_col_ref, ai_col_ref,
               btr_ref, bti_ref, cr_ref, ci_ref, ctr_ref, cti_ref, d_ref,
               y_ref, k2_scr, tm_scr, w_scr, v_scr):
    nc, p, t = btr_ref.shape[1], btr_ref.shape[2], SSM_CHUNK
    dt = jnp.exp(ldt_ref[0])
    lr, li = ar_row_ref[0], ai_row_ref[0]
    mag = jnp.exp(lr * dt)
    lbr, lbi = mag * jnp.cos(li * dt), mag * jnp.sin(li * dt)
    den = lr * lr + li * li
    nr, ni = lbr - 1.0, lbi
    coef_r, coef_i = (nr * lr + ni * li) / den, (ni * lr - nr * li) / den
    bbr, bbi = _cmul(coef_r, coef_i, btr_ref[0], bti_ref[0])

    def powers(rate, freq, n):
        m = jnp.exp(rate * n)
        return m * jnp.cos(freq * n), m * jnp.sin(freq * n)

    rate_c, freq_c = ar_col_ref[0] * dt, ai_col_ref[0] * dt
    rate_r, freq_r = lr * dt, li * dt
    tau = lax.broadcasted_iota(jnp.int32, (1, t), 1).astype(F32)
    pr0, pi0 = powers(rate_c, freq_c, tau)
    pr1, pi1 = powers(rate_c, freq_c, tau + 1.0)
    back = (t - 1) - lax.broadcasted_iota(jnp.int32, (t, 1), 0)
    qr, qi = powers(rate_r, freq_r, back.astype(F32))

    cr, ci = cr_ref[0], ci_ref[0]
    rows = []
    for c in range(nc):
        gr, gi = _cmul(cr[c:c + 1], ci[c:c + 1], bbr, bbi)
        rows.append(jnp.concatenate([gr, -gi], axis=1))
    k2_scr[...] = jnp.dot(jnp.concatenate(rows, axis=0), jnp.concatenate([pr0, pi0], axis=0),
                          precision=HIGHEST, preferred_element_type=F32)

    causal = (lax.broadcasted_iota(jnp.int32, (t, t), 1) >= lax.broadcasted_iota(jnp.int32, (t, t), 0))

    def build(cp, carry):
        r0 = pl.multiple_of(cp * t, t)
        for c in range(nc):
            krow = jnp.broadcast_to(k2_scr[pl.ds(c * nc + cp, 1), :], (t, t))
            tile = pltpu.roll(krow, 0, 1, stride=1, stride_axis=0)
            tm_scr[pl.ds(r0, t), c * t:(c + 1) * t] = jnp.where(causal, tile, 0.0).astype(BF16)
        return carry

    lax.fori_loop(0, nc, build, 0)

    a = jnp.concatenate([ut_ref[c] for c in range(nc)], axis=1)
    y = jnp.dot(a.astype(BF16), tm_scr[...], preferred_element_type=F32)

    for c in range(nc):
        wr, wi = _cmul(qr, qi, bbr[c:c + 1], bbi[c:c + 1])
        w_scr[c * t:(c + 1) * t, :] = jnp.concatenate([wr, wi], axis=1)
    s = jnp.dot(a, w_scr[...], precision=HIGHEST, preferred_element_type=F32)

    r = s.shape[0]
    kidx = lax.broadcasted_iota(jnp.int32, (r, 1), 0) % chunks_per_batch

    def times_lam_pow(z, n):
        zr, zi = powers(rate_r, freq_r, float(n))
        return (jnp.concatenate([zr, zr], axis=1) * z
                + jnp.concatenate([-zi, zi], axis=1) * pltpu.roll(z, p, 1))

    step = 1
    while step < chunks_per_batch:
        prev = pltpu.roll(s, step, 0)
        s = s + jnp.where(kidx >= step, times_lam_pow(prev, t * step), 0.0)
        step *= 2
    e = jnp.where(kidx >= 1, pltpu.roll(s, 1, 0), 0.0)

    ctr, cti = ctr_ref[0], cti_ref[0]
    for c in range(nc):
        vr, vi = _cmul(ctr[:, c:c + 1], cti[:, c:c + 1], pr1, pi1)
        v_scr[:, c * t:(c + 1) * t] = jnp.concatenate([vr, -vi], axis=0)
    y = y + jnp.dot(e, v_scr[...], precision=HIGHEST, preferred_element_type=F32)

    for c in range(nc):
        y_ref[c] = y[:, c * t:(c + 1) * t] + d_ref[c] * ut_ref[c]


def _s5(ut, seq, log_dt, a_re, a_im, b_re, b_im, c_re, c_im, d_skip):
    d_ssm, n = ut.shape
    g, p = a_re.shape
    nc = b_re.shape[2]
    t = SSM_CHUNK
    r = n // t
    ut3 = ut.reshape(d_ssm, r, t)
    grp = lambda i: (i, 0, 0)
    row_spec = pl.BlockSpec((1, 1, p), grp)
    col_spec = pl.BlockSpec((1, p, 1), grp)
    cp_spec = pl.BlockSpec((1, nc, p), grp)
    pc_spec = pl.BlockSpec((1, p, nc), grp)
    yt3 = pl.pallas_call(
        functools.partial(_s5_kernel, seq // t),
        grid=(g,),
        in_specs=[pl.BlockSpec((nc, r, t), grp),
                  pl.BlockSpec((1, 1, 1), grp),
                  row_spec, row_spec, col_spec, col_spec,
                  cp_spec, cp_spec, cp_spec, cp_spec, pc_spec, pc_spec,
                  pl.BlockSpec((nc, 1, 1), grp)],
        out_specs=pl.BlockSpec((nc, r, t), grp),
        out_shape=jax.ShapeDtypeStruct((d_ssm, r, t), F32),
        scratch_shapes=[pltpu.VMEM((nc * nc, t), F32),
                        pltpu.VMEM((nc * t, nc * t), BF16),
                        pltpu.VMEM((nc * t, 2 * p), F32),
                        pltpu.VMEM((2 * p, nc * t), F32)],
        compiler_params=_params("parallel"),
        name="s5",
    )(ut3, log_dt.reshape(g, 1, 1),
      a_re.reshape(g, 1, p), a_im.reshape(g, 1, p), a_re.reshape(g, p, 1), a_im.reshape(g, p, 1),
      jnp.swapaxes(b_re, 1, 2), jnp.swapaxes(b_im, 1, 2), c_re, c_im,
      jnp.swapaxes(c_re, 1, 2), jnp.swapaxes(c_im, 1, 2), d_skip.reshape(d_ssm, 1, 1))
    return yt3.reshape(d_ssm, n)


def _dsa_kernel(topk, seq, tq, kc, dh, bisect_passes,
                qt_ref, qit_ref, wit_ref, k_ref, ksq_ref, ki_ref, vt_ref, o_ref,
                qm_scr, qim_scr, i_scr, m_scr, l_scr, acc_scr):
    i = pl.program_id(1)
    n_heads = qm_scr.shape[0]
    n_chunks = (i * tq + tq + kc - 1) // kc
    low_rows = lax.broadcasted_iota(jnp.int32, (LANES, 1), 0) < dh

    def split_heads(src_ref, dst_scr, heads):
        for h in range(heads):
            keep = low_rows if h % 2 == 0 else jnp.logical_not(low_rows)
            rows = slice((h // 2) * LANES, (h // 2 + 1) * LANES)
            dst_scr[h] = jnp.where(keep, src_ref[rows, :].astype(F32), 0.0).astype(BF16)

    split_heads(qt_ref, qm_scr, n_heads)
    split_heads(qit_ref, qim_scr, IDX_HEADS)

    qpos = i * tq + lax.broadcasted_iota(jnp.int32, (1, tq), 1)
    krow = lax.broadcasted_iota(jnp.int32, (kc, 1), 0)
    wi = wit_ref[...]

    def index_chunk(c, carry):
        lo, hi = carry
        r0 = pl.multiple_of(c * kc, kc)
        ki = ki_ref[pl.ds(r0, kc), :]
        score = jnp.zeros((kc, tq), F32)
        for h in range(IDX_HEADS):
            s = jnp.dot(ki, qim_scr[h], preferred_element_type=F32)
            score = score + wi[h:h + 1, :] * jnp.maximum(s, 0.0)
        i_scr[c] = jnp.where(r0 + krow <= qpos, score, -jnp.inf)
        return (jnp.minimum(lo, jnp.min(score, axis=0, keepdims=True)),
                jnp.maximum(hi, jnp.max(score, axis=0, keepdims=True)))

    lo, hi = lax.fori_loop(0, n_chunks, index_chunk,
                           (jnp.full((1, tq), jnp.inf, F32), jnp.full((1, tq), -jnp.inf, F32)))

    def over_chunks(per_chunk, combine, init):
        def body(c, acc):
            part = per_chunk(i_scr[c]).reshape(kc // PART_ROWS, PART_ROWS, tq)
            return combine(acc, part)
        return lax.fori_loop(0, n_chunks, body, jnp.full((PART_ROWS, tq), init, F32))

    def count(pred):
        parts = over_chunks(lambda x: jnp.where(pred(x), 1.0, 0.0),
                            lambda acc, part: acc + jnp.sum(part, axis=0), 0.0)
        return jnp.sum(parts, axis=0, keepdims=True)

    def max_below(bound):
        parts = over_chunks(lambda x: jnp.where(x < bound, x, -jnp.inf),
                            lambda acc, part: jnp.maximum(acc, jnp.max(part, axis=0)), -jnp.inf)
        return jnp.max(parts, axis=0, keepdims=True)

    kf = float(topk)
    n_causal = (qpos + 1).astype(F32)
    hi = hi + jnp.abs(hi) * 2.0 ** -20 + 1e-30

    def bisecting(st):
        it, _, _, _, active = st
        return jnp.logical_and(it < bisect_passes, jnp.max(active) > 0.0)

    def bisect(st):
        it, lo, hi, c_lo, active = st
        mid = 0.5 * lo + 0.5 * hi
        c = count(lambda x: x >= mid)
        upd = jnp.logical_and(active > 0.0, jnp.logical_and(mid > lo, mid < hi))
        up_lo = jnp.logical_and(upd, c >= kf)
        up_hi = jnp.logical_and(upd, c < kf)
        lo = jnp.where(up_lo, mid, lo)
        c_lo = jnp.where(up_lo, c, c_lo)
        hi = jnp.where(up_hi, mid, hi)
        return it + 1, lo, hi, c_lo, jnp.where(c_lo > kf, active, 0.0)

    active0 = jnp.where(n_causal > kf, 1.0, 0.0)
    _, lo, hi, c_lo, active = lax.while_loop(bisecting, bisect, (jnp.int32(0), lo, hi, n_causal, active0))

    def stepping(st):
        return jnp.max(st[3]) > 0.0

    def step_down(st):
        lo, hi, c_lo, active = st
        v = max_below(hi)
        c = count(lambda x: x >= v)
        on = active > 0.0
        found = jnp.logical_and(on, c >= kf)
        lo = jnp.where(found, v, lo)
        c_lo = jnp.where(found, c, c_lo)
        hi = jnp.where(jnp.logical_and(on, c < kf), v, hi)
        return lo, hi, c_lo, jnp.where(found, 0.0, active)

    thr, _, c_thr, _ = lax.while_loop(stepping, step_down, (lo, hi, c_lo, active))
    tied = jnp.max(jnp.where(c_thr > kf, 1.0, 0.0)) > 0.0

    qsq = jnp.zeros((1, tq), F32)
    for h in range(n_heads):
        qh = qm_scr[h].astype(F32)
        qsq = jnp.maximum(qsq, jnp.sum(qh * qh, axis=0, keepdims=True))
    ksq = jnp.max(jnp.max(ksq_ref[...], axis=0, keepdims=True), axis=1, keepdims=True)
    shift = -1.02 * jnp.sqrt(qsq * ksq)

    @pl.when(jnp.logical_not(tied))
    def _():
        def mask_chunk(c, carry):
            i_scr[c] = jnp.where(i_scr[c] >= thr, shift, NEG_BIG)
            return carry
        lax.fori_loop(0, n_chunks, mask_chunk, 0)

    @pl.when(tied)
    def _():
        above = count(lambda x: x > thr)
        room = jnp.where(c_thr > kf, kf - above, float(seq))
        tri = (lax.broadcasted_iota(jnp.int32, (kc, kc), 0)
               >= lax.broadcasted_iota(jnp.int32, (kc, kc), 1)).astype(BF16)

        def mask_chunk(c, seen):
            x = i_scr[c]
            eq = jnp.where(x == thr, 1.0, 0.0)
            rank = jnp.dot(tri, eq.astype(BF16), preferred_element_type=F32) + seen
            keep = jnp.logical_or(x > thr, jnp.logical_and(x == thr, rank <= room))
            i_scr[c] = jnp.where(keep, shift, NEG_BIG)
            return seen + jnp.sum(eq, axis=0, keepdims=True)
        lax.fori_loop(0, n_chunks, mask_chunk, jnp.zeros((1, tq), F32))

    def logits(c, h):
        r0 = pl.multiple_of(c * kc, kc)
        kk = k_ref[pl.ds(r0, kc), (h // 2) * LANES:(h // 2 + 1) * LANES]
        return jnp.dot(kk, qm_scr[h], preferred_element_type=F32) + i_scr[c]

    acc_scr[...] = jnp.zeros(acc_scr.shape, F32)

    def attend_chunk(c, carry):
        p = jnp.exp2(logits(c, 0)).astype(BF16)
        for h in range(n_heads):
            p_next = jnp.exp2(logits(c, h + 1)).astype(BF16) if h + 1 < n_heads else None
            acc_scr[h] += jnp.dot(vt_ref[c, h], p, preferred_element_type=F32)
            p = p_next
        return carry

    lax.fori_loop(0, n_chunks, attend_chunk, 0)

    l_min = jnp.full((1, tq), jnp.inf, F32)
    for h in range(n_heads):
        denom = acc_scr[h, dh:dh + 1, :]
        l_min = jnp.minimum(l_min, denom)
        o_ref[h * dh:(h + 1) * dh, :] = acc_scr[h, 0:dh, :] / denom

    @pl.when(jnp.min(l_min) < 2.0 ** -64)
    def _():
        m_scr[...] = jnp.full(m_scr.shape, NEG_BIG, F32)
        l_scr[...] = jnp.zeros(l_scr.shape, F32)
        acc_scr[...] = jnp.zeros(acc_scr.shape, F32)

        def attend_chunk_online(c, carry):
            for h in range(n_heads):
                s = logits(c, h)
                m_old = m_scr[h:h + 1, :]
                m_new = jnp.maximum(m_old, jnp.max(s, axis=0, keepdims=True))
                alpha = jnp.exp2(m_old - m_new)
                p = jnp.exp2(s - m_new)
                l_scr[h:h + 1, :] = alpha * l_scr[h:h + 1, :] + jnp.sum(p, axis=0, keepdims=True)
                m_scr[h:h + 1, :] = m_new
                pv = jnp.dot(vt_ref[c, h, 0:dh, :], p.astype(BF16), preferred_element_type=F32)
                acc_scr[h, 0:dh, :] = alpha * acc_scr[h, 0:dh, :] + pv
            return carry

        lax.fori_loop(0, n_chunks, attend_chunk_online, 0)
        for h in range(n_heads):
            o_ref[h * dh:(h + 1) * dh, :] = acc_scr[h, 0:dh, :] / l_scr[h:h + 1, :]


def _dsa(qt, qit, wit, k, ksq, ki, vt4, seq, tq, dh):
    d_attn, n = qt.shape
    bsz = n // seq
    kc = vt4.shape[3]
    n_heads = d_attn // dh
    n_qi = IDX_HEADS * IDX_DIM
    assert 2 * dh == LANES and 2 * IDX_DIM == LANES and n_heads % 2 == 0 and IDX_HEADS % 2 == 0
    assert kc % PART_ROWS == 0
    topk = min(TOPK_MAX, seq // 4)
    blocks_per_batch = seq // tq
    chunks_per_batch = seq // kc
    bisect_passes = 14
    qcol = lambda b, i: (0, b * blocks_per_batch + i)
    per_batch = lambda b, i: (b, 0)
    return pl.pallas_call(
        functools.partial(_dsa_kernel, topk, seq, tq, kc, dh, bisect_passes),
        grid=(bsz, blocks_per_batch),
        in_specs=[pl.BlockSpec((d_attn, tq), qcol),
                  pl.BlockSpec((n_qi, tq), qcol),
                  pl.BlockSpec((IDX_HEADS, tq), qcol),
                  pl.BlockSpec((seq, d_attn), per_batch),
                  pl.BlockSpec((chunks_per_batch * n_heads, LANES), per_batch),
                  pl.BlockSpec((seq, 2 * IDX_DIM), per_batch),
                  pl.BlockSpec((chunks_per_batch, n_heads, dh + BF16_SUBLANES, kc), lambda b, i: (b, 0, 0, 0))],
        out_specs=pl.BlockSpec((d_attn, tq), qcol),
        out_shape=jax.ShapeDtypeStruct((d_attn, n), F32),
        scratch_shapes=[pltpu.VMEM((n_heads, LANES, tq), BF16),
                        pltpu.VMEM((IDX_HEADS, LANES, tq), BF16),
                        pltpu.VMEM((chunks_per_batch, kc, tq), F32),
                        pltpu.VMEM((n_heads, tq), F32),
                        pltpu.VMEM((n_heads, tq), F32),
                        pltpu.VMEM((n_heads, dh + BF16_SUBLANES, tq), F32)],
        compiler_params=_params("parallel", "arbitrary"),
        name="dsa",
    )(qt, qit, wit, k, ksq, ki, vt4)


def _mix_out_kernel(yt_ref, yat_ref, x_ref, g1_ref, sh_ref, sc_ref, gs_ref, ga_ref, wglu_ref, bglu_ref,
                    wo_ref, gf_ref, h_ref, hn_ref):
    def norm_cols(v, g):
        return v * lax.rsqrt(jnp.mean(v * v, axis=0, keepdims=True) + EPS) * g

    y = _gelu(yt_ref[...])
    z = jnp.dot(wglu_ref[...], y.astype(BF16), preferred_element_type=F32) + bglu_ref[...]
    y = y * jax.nn.sigmoid(z)
    mix_t = jnp.concatenate([norm_cols(y, gs_ref[...]), norm_cols(yat_ref[...], ga_ref[...])], axis=0)
    mix = jnp.dot(mix_t.T.astype(BF16), wo_ref[...], preferred_element_type=F32)
    h = x_ref[...] + g1_ref[0] * mix
    h_ref[...] = h
    hn_ref[...] = (_rmsnorm_rows(h, gf_ref[...]) * (1.0 + sc_ref[0]) + sh_ref[0]).astype(BF16)


def _mix_out(yt, yat, x2, g1, sh2, sc2, gs, ga, w_glu, b_glu, w_out, gf, seq, tm):
    d_ssm, n = yt.shape
    d = x2.shape[1]
    d_attn = yat.shape[0]
    bsz = n // seq
    tiles_per_batch = seq // tm
    row = lambda i: (i, 0)
    col = lambda i: (0, i)
    const = lambda i: (0, 0)
    per_batch = lambda i: (i // tiles_per_batch, 0, 0)
    mod_spec = pl.BlockSpec((1, 1, d), per_batch)
    return pl.pallas_call(
        _mix_out_kernel,
        grid=(n // tm,),
        in_specs=[pl.BlockSpec((d_ssm, tm), col),
                  pl.BlockSpec((d_attn, tm), col),
                  pl.BlockSpec((tm, d), row),
                  mod_spec, mod_spec, mod_spec,
                  pl.BlockSpec((d_ssm, 1), const),
                  pl.BlockSpec((d_attn, 1), const),
                  pl.BlockSpec((d_ssm, d_ssm), const),
                  pl.BlockSpec((d_ssm, 1), const),
                  pl.BlockSpec((d_ssm + d_attn, d), const),
                  pl.BlockSpec((1, d), const)],
        out_specs=[pl.BlockSpec((tm, d), row), pl.BlockSpec((tm, d), row)],
        out_shape=[jax.ShapeDtypeStruct((n, d), F32), jax.ShapeDtypeStruct((n, d), BF16)],
        compiler_params=_params("parallel"),
        name="mix_out",
    )(yt, yat, x2, g1.reshape(bsz, 1, d), sh2.reshape(bsz, 1, d), sc2.reshape(bsz, 1, d),
      gs.reshape(d_ssm, 1), ga.reshape(d_attn, 1), w_glu.T.astype(BF16), b_glu.reshape(d_ssm, 1),
      w_out.astype(BF16), gf.reshape(1, d))


def _ffn_kernel(tiles_per_batch, tf,
                hn_ref, halo_ref, h_ref, g2_ref, wu_ref, cw_ref, cb_ref, wd_ref,
                o_ref, lhs_scr, act_scr):
    i = pl.program_id(0)
    halo_rows = halo_ref.shape[0]
    lhs_scr[0:halo_rows, :] = jnp.where(i % tiles_per_batch == 0, 0, halo_ref[...]).astype(BF16)
    lhs_scr[halo_rows:, :] = hn_ref[...]

    for j in range(wu_ref.shape[0]):
        up = jnp.dot(lhs_scr[...], wu_ref[j], preferred_element_type=F32)
        cw = cw_ref[j]
        conv = cb_ref[j] + cw[CONV_W - 1:CONV_W] * up
        for back in range(1, CONV_W):
            tap = CONV_W - 1 - back
            conv = conv + cw[tap:tap + 1] * pltpu.roll(up, back, 0)
        conv = conv[halo_rows:]
        act_scr[:, j * tf:(j + 1) * tf] = (_gelu(conv[:, :tf]) * conv[:, tf:]).astype(BF16)

    down = jnp.dot(act_scr[...], wd_ref[...], preferred_element_type=F32)
    o_ref[...] = h_ref[...] + g2_ref[0] * down


def _ffn(hn2, h1, g2, w_up, conv_w, conv_b, w_down, seq, tm, tf):
    n, d = h1.shape
    d_ff = w_down.shape[0]
    bsz = n // seq
    nf = d_ff // tf
    tiles_per_batch = seq // tm
    halo_rows = BF16_SUBLANES
    assert halo_rows >= CONV_W - 1 and tm % halo_rows == 0
    halo_per_tile = tm // halo_rows
    row = lambda i: (i, 0)
    resident = pl.Buffered(1)

    def pair_tiles(a):
        r = a.shape[0]
        return a.reshape(r, 2, nf, tf).transpose(2, 0, 1, 3).reshape(nf, r, 2 * tf)

    return pl.pallas_call(
        functools.partial(_ffn_kernel, tiles_per_batch, tf),
        grid=(n // tm,),
        in_specs=[pl.BlockSpec((tm, d), row),
                  pl.BlockSpec((halo_rows, d), lambda i: (jnp.maximum(i * halo_per_tile - 1, 0), 0)),
                  pl.BlockSpec((tm, d), row),
                  pl.BlockSpec((1, 1, d), lambda i: (i // tiles_per_batch, 0, 0)),
                  pl.BlockSpec((nf, d, 2 * tf), lambda i: (0, 0, 0), pipeline_mode=resident),
                  pl.BlockSpec((nf, CONV_W, 2 * tf), lambda i: (0, 0, 0), pipeline_mode=resident),
                  pl.BlockSpec((nf, 1, 2 * tf), lambda i: (0, 0, 0), pipeline_mode=resident),
                  pl.BlockSpec((d_ff, d), lambda i: (0, 0), pipeline_mode=resident)],
        out_specs=pl.BlockSpec((tm, d), row),
        out_shape=jax.ShapeDtypeStruct((n, d), F32),
        scratch_shapes=[pltpu.VMEM((halo_rows + tm, d), BF16),
                        pltpu.VMEM((tm, d_ff), BF16)],
        compiler_params=_params("parallel"),
        name="ffn",
    )(hn2, hn2, h1, g2.reshape(bsz, 1, d), pair_tiles(w_up.astype(BF16)), pair_tiles(conv_w),
      pair_tiles(conv_b.reshape(1, 2 * d_ff)), w_down.astype(BF16))


def _tile(total, want):
    t = min(total, want)
    assert total % t == 0
    return t


def kernel(x, c, positions, w_ada, b_ada, norm_mix, w_in, ssm_log_dt, ssm_a_re, ssm_a_im, ssm_b_re, ssm_b_im, ssm_c_re, ssm_c_im, ssm_d, ssm_w_glu, ssm_b_glu, q_norm, k_norm, out_norm_ssm, out_norm_attn, w_out, norm_ffn, w_up, conv_w, conv_b, w_down):
    bsz, seq, d = x.shape
    depth = w_ada.shape[0]
    n = bsz * seq
    d_ssm = out_norm_ssm.shape[1]
    d_attn = out_norm_attn.shape[1]
    dh = q_norm.shape[1]
    d_ff = w_down.shape[1]
    assert w_in.shape[2] == d_ssm + 3 * d_attn + IDX_HEADS * IDX_DIM + IDX_DIM + IDX_HEADS
    assert seq % SSM_CHUNK == 0

    t_proj = _tile(seq, 512)
    t_q = _tile(seq, 256)
    t_mix = _tile(seq, 512)
    t_ffn = _tile(seq, 512)
    t_ff = _tile(d_ff, 256)

    h = x.astype(F32).reshape(n, d)
    pos_row = positions.reshape(1, n)
    for l in range(depth):
        mod = _adaln(c.astype(F32), w_ada[l].astype(F32), b_ada[l].astype(F32))
        sh1, sc1, g1, sh2, sc2, g2 = jnp.split(mod, 6, axis=-1)
        ut, qt, k, ksq, vt4, qit, ki, wit = _in_proj(h, norm_mix[l], sh1, sc1, w_in[l], pos_row, q_norm[l],
                                                     k_norm[l], seq, t_proj, d_ssm, d_attn, dh)
        yt = _s5(ut, seq, ssm_log_dt[l], ssm_a_re[l], ssm_a_im[l], ssm_b_re[l], ssm_b_im[l],
                 ssm_c_re[l], ssm_c_im[l], ssm_d[l])
        yat = _dsa(qt, qit, wit, k, ksq, ki, vt4, seq, t_q, dh)
        h1, hn2 = _mix_out(yt, yat, h, g1, sh2, sc2, out_norm_ssm[l], out_norm_attn[l], ssm_w_glu[l],
                           ssm_b_glu[l], w_out[l], norm_ffn[l], seq, t_mix)
        h = _ffn(hn2, h1, g2, w_up[l], conv_w[l], conv_b[l], w_down[l], seq, t_ffn, t_ff)
    return h.reshape(bsz, seq, d).astype(x.dtype)
```

```python
import functools
import math

import jax
import jax.numpy as jnp
from jax import lax
from jax.experimental import pallas as pl
from jax.experimental.pallas import tpu as pltpu

EPS = 1e-6
ROPE_THETA = 10000.0
IDX_HEADS = 8
IDX_DIM = 64
TOPK_MAX = 256
CONV_W = 3

LANES = 128
BF16_SUBLANES = 16
SSM_CHUNK = LANES
PART_ROWS = 32
VMEM_LIMIT = 56 * 1024 * 1024
NEG_BIG = -1e30
LOG2E = math.log2(math.e)

F32 = jnp.float32
BF16 = jnp.bfloat16
HIGHEST = lax.Precision.HIGHEST


def _params(*semantics):
    return pltpu.CompilerParams(dimension_semantics=semantics, vmem_limit_bytes=VMEM_LIMIT)


def _gelu(x):
    return 0.5 * x * (1.0 + jnp.tanh(math.sqrt(2.0 / math.pi) * (x + 0.044715 * (x * x * x))))


def _adaln_kernel(ct_ref, w_ref, b_ref, o_ref):
    ct = ct_ref[...]
    cs = ct * jax.nn.sigmoid(ct)
    w = w_ref[...]
    for b in range(ct.shape[1]):
        o_ref[b:b + 1, :] = jnp.sum(w * cs[:, b:b + 1], axis=0, keepdims=True) + b_ref[...]


def _adaln(c, w, bias):
    bsz, d = c.shape
    n = w.shape[1]
    tn = 1024 if n % 1024 == 0 else n
    return pl.pallas_call(
        _adaln_kernel,
        grid=(n // tn,),
        in_specs=[pl.BlockSpec((d, bsz), lambda j: (0, 0)),
                  pl.BlockSpec((d, tn), lambda j: (0, j)),
                  pl.BlockSpec((1, tn), lambda j: (0, j))],
        out_specs=pl.BlockSpec((bsz, tn), lambda j: (0, j)),
        out_shape=jax.ShapeDtypeStruct((bsz, n), F32),
        compiler_params=_params("parallel"),
        name="adaln",
    )(c.T, w, bias.reshape(1, n))


def _rmsnorm_rows(x, g):
    return x * lax.rsqrt(jnp.mean(x * x, axis=-1, keepdims=True) + EPS) * g


def _rope_t(x3, cos, sin):
    half = x3.shape[1] // 2
    x1, x2 = x3[:, :half, :], x3[:, half:, :]
    return jnp.concatenate([x1 * cos - x2 * sin, x1 * sin + x2 * cos], axis=1)


def _in_proj_kernel(d_ssm, d_attn, dh, wi_scale,
                    x_ref, g_ref, sh_ref, sc_ref, w_ref, pos_ref, inv_ref, qg_ref, kg_ref,
                    ut_ref, qt_ref, k_ref, ksq_ref, vt_ref, qit_ref, ki_ref, wit_ref):
    hn = _rmsnorm_rows(x_ref[...], g_ref[...]) * (1.0 + sc_ref[0]) + sh_ref[0]
    hb = hn.astype(BF16)
    tm = hb.shape[0]

    def proj_t(r0, rows):
        return lax.dot_general(w_ref[:, r0:r0 + rows], hb, (((0,), (1,)), ((), ())),
                               preferred_element_type=F32)

    ang = inv_ref[...] * pos_ref[...].astype(F32)
    cos, sin = jnp.cos(ang)[None], jnp.sin(ang)[None]

    def head_norm(x3, gcol):
        return x3 * lax.rsqrt(jnp.mean(x3 * x3, axis=1, keepdims=True) + EPS) * gcol[None]

    ut_ref[...] = proj_t(0, d_ssm)
    o = d_ssm
    n_heads = d_attn // dh
    q3 = proj_t(o, d_attn).reshape(n_heads, dh, tm)
    q3 = _rope_t(head_norm(q3, qg_ref[...]), cos, sin) * (dh ** -0.5 * LOG2E)
    qt_ref[...] = q3.reshape(d_attn, tm).astype(BF16)
    o += d_attn
    k3 = _rope_t(head_norm(proj_t(o, d_attn).reshape(n_heads, dh, tm), kg_ref[...]), cos, sin)
    k_ref[...] = k3.reshape(d_attn, tm).T.astype(BF16)
    ksq = jnp.max(jnp.sum(k3 * k3, axis=1), axis=1, keepdims=True)
    ksq_ref[...] = jnp.broadcast_to(ksq, ksq_ref.shape)
    o += d_attn
    v3 = proj_t(o, d_attn).reshape(n_heads, dh, tm)
    vt_ref[0] = jnp.concatenate([v3, jnp.ones((n_heads, BF16_SUBLANES, tm), F32)], axis=1).astype(BF16)
    o += d_attn
    n_qi = IDX_HEADS * IDX_DIM
    qi3 = _rope_t(proj_t(o, n_qi).reshape(IDX_HEADS, IDX_DIM, tm), cos, sin)
    qit_ref[...] = qi3.reshape(n_qi, tm).astype(BF16)
    o += n_qi
    tail = proj_t(o, IDX_DIM + BF16_SUBLANES)
    ki = _rope_t(tail[None, :IDX_DIM], cos, sin)[0]
    ki_ref[...] = jnp.concatenate([ki, ki], axis=0).T.astype(BF16)
    wit_ref[...] = tail[IDX_DIM:IDX_DIM + IDX_HEADS] * wi_scale


def _in_proj(x2, g, sh, sc, w_in, pos_row, q_norm, k_norm, seq, tm, d_ssm, d_attn, dh):
    n, d = x2.shape
    bsz = n // seq
    tiles_per_batch = seq // tm
    n_qi = IDX_HEADS * IDX_DIM
    n_heads = d_attn // dh
    w = jnp.concatenate([w_in.astype(BF16), jnp.zeros((d, BF16_SUBLANES - IDX_HEADS), BF16)], axis=1)
    rt = w.shape[1]
    inv = (ROPE_THETA ** (-jnp.arange(0, dh, 2, dtype=F32) / dh)).reshape(dh // 2, 1)
    wi_scale = IDX_HEADS ** -0.5 * IDX_DIM ** -0.5

    row = lambda i: (i, 0)
    col = lambda i: (0, i)
    const = lambda i: (0, 0)
    per_batch = lambda i: (i // tiles_per_batch, 0, 0)
    return pl.pallas_call(
        functools.partial(_in_proj_kernel, d_ssm, d_attn, dh, wi_scale),
        grid=(n // tm,),
        in_specs=[pl.BlockSpec((tm, d), row),
                  pl.BlockSpec((1, d), const),
                  pl.BlockSpec((1, 1, d), per_batch),
                  pl.BlockSpec((1, 1, d), per_batch),
                  pl.BlockSpec((d, rt), const),
                  pl.BlockSpec((1, tm), col),
                  pl.BlockSpec((dh // 2, 1), const),
                  pl.BlockSpec((dh, 1), const),
                  pl.BlockSpec((dh, 1), const)],
        out_specs=[pl.BlockSpec((d_ssm, tm), col),
                   pl.BlockSpec((d_attn, tm), col),
                   pl.BlockSpec((tm, d_attn), row),
                   pl.BlockSpec((n_heads, LANES), row),
                   pl.BlockSpec((1, n_heads, dh + BF16_SUBLANES, tm), lambda i: (i, 0, 0, 0)),
                   pl.BlockSpec((n_qi, tm), col),
                   pl.BlockSpec((tm, 2 * IDX_DIM), row),
                   pl.BlockSpec((IDX_HEADS, tm), col)],
        out_shape=[jax.ShapeDtypeStruct((d_ssm, n), F32),
                   jax.ShapeDtypeStruct((d_attn, n), BF16),
                   jax.ShapeDtypeStruct((n, d_attn), BF16),
                   jax.ShapeDtypeStruct((n // tm * n_heads, LANES), F32),
                   jax.ShapeDtypeStruct((n // tm, n_heads, dh + BF16_SUBLANES, tm), BF16),
                   jax.ShapeDtypeStruct((n_qi, n), BF16),
                   jax.ShapeDtypeStruct((n, 2 * IDX_DIM), BF16),
                   jax.ShapeDtypeStruct((IDX_HEADS, n), F32)],
        compiler_params=_params("parallel"),
        name="in_proj",
    )(x2, g.reshape(1, d), sh.reshape(bsz, 1, d), sc.reshape(bsz, 1, d), w, pos_row, inv,
      q_norm.reshape(dh, 1), k_norm.reshape(dh, 1))


def _cmul(ar, ai, br, bi):
    return ar * br - ai * bi, ar * bi + ai * br


def _s5_kernel(chunks_per_batch,
               ut_ref, ldt_ref, ar_row_ref, ai_row_ref, ar_col_ref, ai_col_ref,
               btr_ref, bti_ref, cr_ref, ci_ref, ctr_ref, cti_ref, d_ref,
               y_ref, k2_scr, tm_scr, w_scr, v_scr):
    nc, p, t = btr_ref.shape[1], btr_ref.shape[2], SSM_CHUNK
    dt = jnp.exp(ldt_ref[0])
    lr, li = ar_row_ref[0], ai_row_ref[0]
    mag = jnp.exp(lr * dt)
    lbr, lbi = mag * jnp.cos(li * dt), mag * jnp.sin(li * dt)
    den = lr * lr + li * li
    nr, ni = lbr - 1.0, lbi
    coef_r, coef_i = (nr * lr + ni * li) / den, (ni * lr - nr * li) / den
    bbr, bbi = _cmul(coef_r, coef_i, btr_ref[0], bti_ref[0])

    def powers(rate, freq, n):
        m = jnp.exp(rate * n)
        return m * jnp.cos(freq * n), m * jnp.sin(freq * n)

    rate_c, freq_c = ar_col_ref[0] * dt, ai_col_ref[0] * dt
    rate_r, freq_r = lr * dt, li * dt
    tau = lax.broadcasted_iota(jnp.int32, (1, t), 1).astype(F32)
    pr0, pi0 = powers(rate_c, freq_c, tau)
    pr1, pi1 = powers(rate_c, freq_c, tau + 1.0)
    back = (t - 1) - lax.broadcasted_iota(jnp.int32, (t, 1), 0)
    qr, qi = powers(rate_r, freq_r, back.astype(F32))

    cr, ci = cr_ref[0], ci_ref[0]
    rows = []
    for c in range(nc):
        gr, gi = _cmul(cr[c:c + 1], ci[c:c + 1], bbr, bbi)
        rows.append(jnp.concatenate([gr, -gi], axis=1))
    k2_scr[...] = jnp.dot(jnp.concatenate(rows, axis=0), jnp.concatenate([pr0, pi0], axis=0),
                          precision=HIGHEST, preferred_element_type=F32)

    causal = (lax.broadcasted_iota(jnp.int32, (t, t), 1) >= lax.broadcasted_iota(jnp.int32, (t, t), 0))

    def build(cp, carry):
        r0 = pl.multiple_of(cp * t, t)
        for c in range(nc):
            krow = jnp.broadcast_to(k2_scr[pl.ds(c * nc + cp, 1), :], (t, t))
            tile = pltpu.roll(krow, 0, 1, stride=1, stride_axis=0)
            tm_scr[pl.ds(r0, t), c * t:(c + 1) * t] = jnp.where(causal, tile, 0.0).astype(BF16)
        return carry

    lax.fori_loop(0, nc, build, 0)

    a = jnp.concatenate([ut_ref[c] for c in range(nc)], axis=1).astype(BF16)
    y = jnp.dot(a, tm_scr[...], preferred_element_type=F32)

    for c in range(nc):
        wr, wi = _cmul(qr, qi, bbr[c:c + 1], bbi[c:c + 1])
        w_scr[c * t:(c + 1) * t, :] = jnp.concatenate([wr, wi], axis=1).astype(BF16)
    s = jnp.dot(a, w_scr[...], preferred_element_type=F32)

    r = s.shape[0]
    kidx = lax.broadcasted_iota(jnp.int32, (r, 1), 0) % chunks_per_batch

    def times_lam_pow(z, n):
        zr, zi = powers(rate_r, freq_r, float(n))
        return (jnp.concatenate([zr, zr], axis=1) * z
                + jnp.concatenate([-zi, zi], axis=1) * pltpu.roll(z, p, 1))

    step = 1
    while step < chunks_per_batch:
        prev = pltpu.roll(s, step, 0)
        s = s + jnp.where(kidx >= step, times_lam_pow(prev, t * step), 0.0)
        step *= 2
    e = jnp.where(kidx >= 1, pltpu.roll(s, 1, 0), 0.0)

    ctr, cti = ctr_ref[0], cti_ref[0]
    for c in range(nc):
        vr, vi = _cmul(ctr[:, c:c + 1], cti[:, c:c + 1], pr1, pi1)
        v_scr[:, c * t:(c + 1) * t] = jnp.concatenate([vr, -vi], axis=0).astype(BF16)
    y = y + jnp.dot(e.astype(BF16), v_scr[...], preferred_element_type=F32)

    for c in range(nc):
        y_ref[c] = y[:, c * t:(c + 1) * t] + d_ref[c] * ut_ref[c]


def _s5(ut, seq, log_dt, a_re, a_im, b_re, b_im, c_re, c_im, d_skip):
    d_ssm, n = ut.shape
    g, p = a_re.shape
    nc = b_re.shape[2]
    t = SSM_CHUNK
    r = n // t
    ut3 = ut.reshape(d_ssm, r, t)
    grp = lambda i: (i, 0, 0)
    row_spec = pl.BlockSpec((1, 1, p), grp)
    col_spec = pl.BlockSpec((1, p, 1), grp)
    cp_spec = pl.BlockSpec((1, nc, p), grp)
    pc_spec = pl.BlockSpec((1, p, nc), grp)
    yt3 = pl.pallas_call(
        functools.partial(_s5_kernel, seq // t),
        grid=(g,),
        in_specs=[pl.BlockSpec((nc, r, t), grp),
                  pl.BlockSpec((1, 1, 1), grp),
                  row_spec, row_spec, col_spec, col_spec,
                  cp_spec, cp_spec, cp_spec, cp_spec, pc_spec, pc_spec,
                  pl.BlockSpec((nc, 1, 1), grp)],
        out_specs=pl.BlockSpec((nc, r, t), grp),
        out_shape=jax.ShapeDtypeStruct((d_ssm, r, t), F32),
        scratch_shapes=[pltpu.VMEM((nc * nc, t), F32),
                        pltpu.VMEM((nc * t, nc * t), BF16),
                        pltpu.VMEM((nc * t, 2 * p), BF16),
                        pltpu.VMEM((2 * p, nc * t), BF16)],
        compiler_params=_params("parallel"),
        name="s5",
    )(ut3, log_dt.reshape(g, 1, 1),
      a_re.reshape(g, 1, p), a_im.reshape(g, 1, p), a_re.reshape(g, p, 1), a_im.reshape(g, p, 1),
      jnp.swapaxes(b_re, 1, 2), jnp.swapaxes(b_im, 1, 2), c_re, c_im,
      jnp.swapaxes(c_re, 1, 2), jnp.swapaxes(c_im, 1, 2), d_skip.reshape(d_ssm, 1, 1))
    return yt3.reshape(d_ssm, n)


def _dsa_kernel(topk, seq, tq, kc, dh, bisect_passes,
                qt_ref, qit_ref, wit_ref, k_ref, ksq_ref, ki_ref, vt_ref, o_ref,
                qm_scr, qim_scr, i_scr, m_scr, l_scr, acc_scr):
    i = pl.program_id(1)
    n_heads = qm_scr.shape[0]
    n_chunks = (i * tq + tq + kc - 1) // kc
    low_rows = lax.broadcasted_iota(jnp.int32, (LANES, 1), 0) < dh

    def split_heads(src_ref, dst_scr, heads):
        for h in range(heads):
            keep = low_rows if h % 2 == 0 else jnp.logical_not(low_rows)
            rows = slice((h // 2) * LANES, (h // 2 + 1) * LANES)
            dst_scr[h] = jnp.where(keep, src_ref[rows, :].astype(F32), 0.0).astype(BF16)

    split_heads(qt_ref, qm_scr, n_heads)
    split_heads(qit_ref, qim_scr, IDX_HEADS)

    qpos = i * tq + lax.broadcasted_iota(jnp.int32, (1, tq), 1)
    krow = lax.broadcasted_iota(jnp.int32, (kc, 1), 0)
    wi = wit_ref[...]

    def index_chunk(c, carry):
        lo, hi = carry
        r0 = pl.multiple_of(c * kc, kc)
        ki = ki_ref[pl.ds(r0, kc), :]
        score = jnp.zeros((kc, tq), F32)
        ahead = 2
        dots = [jnp.dot(ki, qim_scr[h], preferred_element_type=F32) for h in range(ahead)]
        for h in range(IDX_HEADS):
            if h + ahead < IDX_HEADS:
                dots.append(jnp.dot(ki, qim_scr[h + ahead], preferred_element_type=F32))
            score = score + wi[h:h + 1, :] * jnp.maximum(dots[h], 0.0)
        i_scr[c] = jnp.where(r0 + krow <= qpos, score, -jnp.inf)
        return (jnp.minimum(lo, jnp.min(score, axis=0, keepdims=True)),
                jnp.maximum(hi, jnp.max(score, axis=0, keepdims=True)))

    lo, hi = lax.fori_loop(0, n_chunks, index_chunk,
                           (jnp.full((1, tq), jnp.inf, F32), jnp.full((1, tq), -jnp.inf, F32)))

    def over_chunks(per_chunk, combine, init):
        def body(c, acc):
            part = per_chunk(i_scr[c]).reshape(kc // PART_ROWS, PART_ROWS, tq)
            return combine(acc, part)
        return lax.fori_loop(0, n_chunks, body, jnp.full((PART_ROWS, tq), init, F32))

    def count(pred):
        parts = over_chunks(lambda x: jnp.where(pred(x), 1.0, 0.0),
                            lambda acc, part: acc + jnp.sum(part, axis=0), 0.0)
        return jnp.sum(parts, axis=0, keepdims=True)

    def max_below(bound):
        parts = over_chunks(lambda x: jnp.where(x < bound, x, -jnp.inf),
                            lambda acc, part: jnp.maximum(acc, jnp.max(part, axis=0)), -jnp.inf)
        return jnp.max(parts, axis=0, keepdims=True)

    kf = float(topk)
    n_causal = (qpos + 1).astype(F32)
    hi = hi + jnp.abs(hi) * 2.0 ** -20 + 1e-30

    def bisect(_, st):
        lo, hi, c_lo, active = st
        mid = 0.5 * lo + 0.5 * hi
        c = count(lambda x: x >= mid)
        upd = jnp.logical_and(active > 0.0, jnp.logical_and(mid > lo, mid < hi))
        up_lo = jnp.logical_and(upd, c >= kf)
        up_hi = jnp.logical_and(upd, c < kf)
        lo = jnp.where(up_lo, mid, lo)
        c_lo = jnp.where(up_lo, c, c_lo)
        hi = jnp.where(up_hi, mid, hi)
        return lo, hi, c_lo, jnp.where(c_lo > kf, active, 0.0)

    active0 = jnp.where(n_causal > kf, 1.0, 0.0)
    lo, hi, c_lo, active = lax.fori_loop(0, bisect_passes, bisect, (lo, hi, n_causal, active0))

    def stepping(st):
        return jnp.max(st[3]) > 0.0

    def step_down(st):
        lo, hi, c_lo, active = st
        v = max_below(hi)
        c = count(lambda x: x >= v)
        on = active > 0.0
        found = jnp.logical_and(on, c >= kf)
        lo = jnp.where(found, v, lo)
        c_lo = jnp.where(found, c, c_lo)
        hi = jnp.where(jnp.logical_and(on, c < kf), v, hi)
        return lo, hi, c_lo, jnp.where(found, 0.0, active)

    thr, _, c_thr, _ = lax.while_loop(stepping, step_down, (lo, hi, c_lo, active))
    tied = jnp.max(jnp.where(c_thr > kf, 1.0, 0.0)) > 0.0

    qsq = jnp.zeros((1, tq), F32)
    for h in range(n_heads):
        qh = qm_scr[h].astype(F32)
        qsq = jnp.maximum(qsq, jnp.sum(qh * qh, axis=0, keepdims=True))
    ksq = jnp.max(jnp.max(ksq_ref[...], axis=0, keepdims=True), axis=1, keepdims=True)
    shift = -1.02 * jnp.sqrt(qsq * ksq)

    @pl.when(jnp.logical_not(tied))
    def _():
        def mask_chunk(c, carry):
            i_scr[c] = jnp.where(i_scr[c] >= thr, shift, NEG_BIG)
            return carry
        lax.fori_loop(0, n_chunks, mask_chunk, 0)

    @pl.when(tied)
    def _():
        above = count(lambda x: x > thr)
        room = jnp.where(c_thr > kf, kf - above, float(seq))
        tri = (lax.broadcasted_iota(jnp.int32, (kc, kc), 0)
               >= lax.broadcasted_iota(jnp.int32, (kc, kc), 1)).astype(BF16)

        def mask_chunk(c, seen):
            x = i_scr[c]
            eq = jnp.where(x == thr, 1.0, 0.0)
            rank = jnp.dot(tri, eq.astype(BF16), preferred_element_type=F32) + seen
            keep = jnp.logical_or(x > thr, jnp.logical_and(x == thr, rank <= room))
            i_scr[c] = jnp.where(keep, shift, NEG_BIG)
            return seen + jnp.sum(eq, axis=0, keepdims=True)
        lax.fori_loop(0, n_chunks, mask_chunk, jnp.zeros((1, tq), F32))

    def logits(c, h):
        r0 = pl.multiple_of(c * kc, kc)
        kk = k_ref[pl.ds(r0, kc), (h // 2) * LANES:(h // 2 + 1) * LANES]
        return jnp.dot(kk, qm_scr[h], preferred_element_type=F32) + i_scr[c]

    acc_scr[...] = jnp.zeros(acc_scr.shape, F32)

    def attend_chunk(c, carry):
        ahead = 4
        probs = [jnp.exp2(logits(c, h)).astype(BF16) for h in range(ahead)]
        for h in range(n_heads):
            if h + ahead < n_heads:
                probs.append(jnp.exp2(logits(c, h + ahead)).astype(BF16))
            acc_scr[h] += jnp.dot(vt_ref[c, h], probs[h], preferred_element_type=F32)
        return carry

    lax.fori_loop(0, n_chunks, attend_chunk, 0)

    l_min = jnp.full((1, tq), jnp.inf, F32)
    for h in range(n_heads):
        denom = acc_scr[h, dh:dh + 1, :]
        l_min = jnp.minimum(l_min, denom)
        o_ref[h * dh:(h + 1) * dh, :] = acc_scr[h, 0:dh, :] / denom

    @pl.when(jnp.min(l_min) < 2.0 ** -64)
    def _():
        m_scr[...] = jnp.full(m_scr.shape, NEG_BIG, F32)
        l_scr[...] = jnp.zeros(l_scr.shape, F32)
        acc_scr[...] = jnp.zeros(acc_scr.shape, F32)

        def attend_chunk_online(c, carry):
            for h in range(n_heads):
                s = logits(c, h)
                m_old = m_scr[h:h + 1, :]
                m_new = jnp.maximum(m_old, jnp.max(s, axis=0, keepdims=True))
                alpha = jnp.exp2(m_old - m_new)
                p = jnp.exp2(s - m_new)
                l_scr[h:h + 1, :] = alpha * l_scr[h:h + 1, :] + jnp.sum(p, axis=0, keepdims=True)
                m_scr[h:h + 1, :] = m_new
                pv = jnp.dot(vt_ref[c, h, 0:dh, :], p.astype(BF16), preferred_element_type=F32)
                acc_scr[h, 0:dh, :] = alpha * acc_scr[h, 0:dh, :] + pv
            return carry

        lax.fori_loop(0, n_chunks, attend_chunk_online, 0)
        for h in range(n_heads):
            o_ref[h * dh:(h + 1) * dh, :] = acc_scr[h, 0:dh, :] / l_scr[h:h + 1, :]


def _dsa(qt, qit, wit, k, ksq, ki, vt4, seq, tq, dh):
    d_attn, n = qt.shape
    bsz = n // seq
    kc = vt4.shape[3]
    n_heads = d_attn // dh
    n_qi = IDX_HEADS * IDX_DIM
    assert 2 * dh == LANES and 2 * IDX_DIM == LANES and n_heads % 2 == 0 and IDX_HEADS % 2 == 0
    assert kc % PART_ROWS == 0
    topk = min(TOPK_MAX, seq // 4)
    blocks_per_batch = seq // tq
    chunks_per_batch = seq // kc
    bisect_passes = 14
    qcol = lambda b, i: (0, b * blocks_per_batch + i)
    per_batch = lambda b, i: (b, 0)
    return pl.pallas_call(
        functools.partial(_dsa_kernel, topk, seq, tq, kc, dh, bisect_passes),
        grid=(bsz, blocks_per_batch),
        in_specs=[pl.BlockSpec((d_attn, tq), qcol),
                  pl.BlockSpec((n_qi, tq), qcol),
                  pl.BlockSpec((IDX_HEADS, tq), qcol),
                  pl.BlockSpec((seq, d_attn), per_batch),
                  pl.BlockSpec((chunks_per_batch * n_heads, LANES), per_batch),
                  pl.BlockSpec((seq, 2 * IDX_DIM), per_batch),
                  pl.BlockSpec((chunks_per_batch, n_heads, dh + BF16_SUBLANES, kc), lambda b, i: (b, 0, 0, 0))],
        out_specs=pl.BlockSpec((d_attn, tq), qcol),
        out_shape=jax.ShapeDtypeStruct((d_attn, n), F32),
        scratch_shapes=[pltpu.VMEM((n_heads, LANES, tq), BF16),
                        pltpu.VMEM((IDX_HEADS, LANES, tq), BF16),
                        pltpu.VMEM((chunks_per_batch, kc, tq), F32),
                        pltpu.VMEM((n_heads, tq), F32),
                        pltpu.VMEM((n_heads, tq), F32),
                        pltpu.VMEM((n_heads, dh + BF16_SUBLANES, tq), F32)],
        compiler_params=_params("parallel", "arbitrary"),
        name="dsa",
    )(qt, qit, wit, k, ksq, ki, vt4)


def _mix_out_kernel(yt_ref, yat_ref, x_ref, g1_ref, sh_ref, sc_ref, gs_ref, ga_ref, wglu_ref, bglu_ref,
                    wo_ref, gf_ref, h_ref, hn_ref):
    def norm_cols(v, g):
        return v * lax.rsqrt(jnp.mean(v * v, axis=0, keepdims=True) + EPS) * g

    y = _gelu(yt_ref[...])
    z = jnp.dot(wglu_ref[...], y.astype(BF16), preferred_element_type=F32) + bglu_ref[...]
    y = y * jax.nn.sigmoid(z)
    mix_t = jnp.concatenate([norm_cols(y, gs_ref[...]), norm_cols(yat_ref[...], ga_ref[...])], axis=0)
    mix = jnp.dot(mix_t.T.astype(BF16), wo_ref[...], preferred_element_type=F32)
    h = x_ref[...] + g1_ref[0] * mix
    h_ref[...] = h
    hn_ref[...] = (_rmsnorm_rows(h, gf_ref[...]) * (1.0 + sc_ref[0]) + sh_ref[0]).astype(BF16)


def _mix_out(yt, yat, x2, g1, sh2, sc2, gs, ga, w_glu, b_glu, w_out, gf, seq, tm):
    d_ssm, n = yt.shape
    d = x2.shape[1]
    d_attn = yat.shape[0]
    bsz = n // seq
    tiles_per_batch = seq // tm
    row = lambda i: (i, 0)
    col = lambda i: (0, i)
    const = lambda i: (0, 0)
    per_batch = lambda i: (i // tiles_per_batch, 0, 0)
    mod_spec = pl.BlockSpec((1, 1, d), per_batch)
    return pl.pallas_call(
        _mix_out_kernel,
        grid=(n // tm,),
        in_specs=[pl.BlockSpec((d_ssm, tm), col),
                  pl.BlockSpec((d_attn, tm), col),
                  pl.BlockSpec((tm, d), row),
                  mod_spec, mod_spec, mod_spec,
                  pl.BlockSpec((d_ssm, 1), const),
                  pl.BlockSpec((d_attn, 1), const),
                  pl.BlockSpec((d_ssm, d_ssm), const),
                  pl.BlockSpec((d_ssm, 1), const),
                  pl.BlockSpec((d_ssm + d_attn, d), const),
                  pl.BlockSpec((1, d), const)],
        out_specs=[pl.BlockSpec((tm, d), row), pl.BlockSpec((tm, d), row)],
        out_shape=[jax.ShapeDtypeStruct((n, d), F32), jax.ShapeDtypeStruct((n, d), BF16)],
        compiler_params=_params("parallel"),
        name="mix_out",
    )(yt, yat, x2, g1.reshape(bsz, 1, d), sh2.reshape(bsz, 1, d), sc2.reshape(bsz, 1, d),
      gs.reshape(d_ssm, 1), ga.reshape(d_attn, 1), w_glu.T.astype(BF16), b_glu.reshape(d_ssm, 1),
      w_out.astype(BF16), gf.reshape(1, d))


def _ffn_kernel(tiles_per_batch, tf,
                hn_ref, halo_ref, h_ref, g2_ref, wu_ref, cw_ref, cb_ref, wd_ref,
                o_ref, lhs_scr, act_scr):
    i = pl.program_id(0)
    halo_rows = halo_ref.shape[0]
    lhs_scr[0:halo_rows, :] = jnp.where(i % tiles_per_batch == 0, 0, halo_ref[...]).astype(BF16)
    lhs_scr[halo_rows:, :] = hn_ref[...]

    d_ff = act_scr.shape[1]

    def conv_up(c0):
        up = jnp.dot(lhs_scr[...], wu_ref[:, c0:c0 + tf], preferred_element_type=F32)
        conv = cb_ref[:, c0:c0 + tf] + cw_ref[CONV_W - 1:CONV_W, c0:c0 + tf] * up
        for back in range(1, CONV_W):
            tap = CONV_W - 1 - back
            conv = conv + cw_ref[tap:tap + 1, c0:c0 + tf] * pltpu.roll(up, back, 0)
        return conv[halo_rows:]

    for c0 in range(0, d_ff, tf):
        act_scr[:, c0:c0 + tf] = (_gelu(conv_up(c0)) * conv_up(d_ff + c0)).astype(BF16)

    down = jnp.dot(act_scr[...], wd_ref[...], preferred_element_type=F32)
    o_ref[...] = h_ref[...] + g2_ref[0] * down


def _ffn(hn2, h1, g2, w_up, conv_w, conv_b, w_down, seq, tm, tf):
    n, d = h1.shape
    d_ff = w_down.shape[0]
    bsz = n // seq
    tiles_per_batch = seq // tm
    halo_rows = BF16_SUBLANES
    assert halo_rows >= CONV_W - 1 and tm % halo_rows == 0 and d_ff % tf == 0
    halo_per_tile = tm // halo_rows
    row = lambda i: (i, 0)
    const = lambda i: (0, 0)
    resident = pl.Buffered(1)

    return pl.pallas_call(
        functools.partial(_ffn_kernel, tiles_per_batch, tf),
        grid=(n // tm,),
        in_specs=[pl.BlockSpec((tm, d), row),
                  pl.BlockSpec((halo_rows, d), lambda i: (jnp.maximum(i * halo_per_tile - 1, 0), 0)),
                  pl.BlockSpec((tm, d), row),
                  pl.BlockSpec((1, 1, d), lambda i: (i // tiles_per_batch, 0, 0)),
                  pl.BlockSpec((d, 2 * d_ff), const, pipeline_mode=resident),
                  pl.BlockSpec((CONV_W, 2 * d_ff), const, pipeline_mode=resident),
                  pl.BlockSpec((1, 2 * d_ff), const, pipeline_mode=resident),
                  pl.BlockSpec((d_ff, d), const, pipeline_mode=resident)],
        out_specs=pl.BlockSpec((tm, d), row),
        out_shape=jax.ShapeDtypeStruct((n, d), F32),
        scratch_shapes=[pltpu.VMEM((halo_rows + tm, d), BF16),
                        pltpu.VMEM((tm, d_ff), BF16)],
        compiler_params=_params("parallel"),
        name="ffn",
    )(hn2, hn2, h1, g2.reshape(bsz, 1, d), w_up.astype(BF16), conv_w, conv_b.reshape(1, 2 * d_ff),
      w_down.astype(BF16))


def _tile(total, want):
    t = min(total, want)
    assert total % t == 0
    return t


def kernel(x, c, positions, w_ada, b_ada, norm_mix, w_in, ssm_log_dt, ssm_a_re, ssm_a_im, ssm_b_re, ssm_b_im, ssm_c_re, ssm_c_im, ssm_d, ssm_w_glu, ssm_b_glu, q_norm, k_norm, out_norm_ssm, out_norm_attn, w_out, norm_ffn, w_up, conv_w, conv_b, w_down):
    bsz, seq, d = x.shape
    depth = w_ada.shape[0]
    n = bsz * seq
    d_ssm = out_norm_ssm.shape[1]
    d_attn = out_norm_attn.shape[1]
    dh = q_norm.shape[1]
    d_ff = w_down.shape[1]
    assert w_in.shape[2] == d_ssm + 3 * d_attn + IDX_HEADS * IDX_DIM + IDX_DIM + IDX_HEADS
    assert seq % SSM_CHUNK == 0

    t_proj = _tile(seq, 512)
    t_q = _tile(seq, 256)
    t_mix = _tile(seq, 512)
    t_ffn = _tile(seq, 512)
    t_ff = _tile(d_ff, 256)

    h = x.astype(F32).reshape(n, d)
    pos_row = positions.reshape(1, n)
    for l in range(depth):
        mod = _adaln(c.astype(F32), w_ada[l].astype(F32), b_ada[l].astype(F32))
        sh1, sc1, g1, sh2, sc2, g2 = jnp.split(mod, 6, axis=-1)
        ut, qt, k, ksq, vt4, qit, ki, wit = _in_proj(h, norm_mix[l], sh1, sc1, w_in[l], pos_row, q_norm[l],
                                                     k_norm[l], seq, t_proj, d_ssm, d_attn, dh)
        yt = _s5(ut, seq, ssm_log_dt[l], ssm_a_re[l], ssm_a_im[l], ssm_b_re[l], ssm_b_im[l],
                 ssm_c_re[l], ssm_c_im[l], ssm_d[l])
        yat = _dsa(qt, qit, wit, k, ksq, ki, vt4, seq, t_q, dh)
        h1, hn2 = _mix_out(yt, yat, h, g1, sh2, sc2, out_norm_ssm[l], out_norm_attn[l], ssm_w_glu[l],
                           ssm_b_glu[l], w_out[l], norm_ffn[l], seq, t_mix)
        h = _ffn(hn2, h1, g2, w_up[l], conv_w[l], conv_b[l], w_down[l], seq, t_ffn, t_ff)
    return h.reshape(bsz, seq, d).astype(x.dtype)
```

```python
import functools
import math

import jax
import jax.numpy as jnp
from jax import lax
from jax.experimental import pallas as pl
from jax.experimental.pallas import tpu as pltpu

EPS = 1e-6
ROPE_THETA = 10000.0
IDX_HEADS = 8
IDX_DIM = 64
TOPK_MAX = 256
CONV_W = 3

LANES = 128
BF16_SUBLANES = 16
SSM_CHUNK = LANES
PART_ROWS = 32
VMEM_LIMIT = 56 * 1024 * 1024
NEG_BIG = -1e30
LOG2E = math.log2(math.e)

F32 = jnp.float32
BF16 = jnp.bfloat16
HIGHEST = lax.Precision.HIGHEST


def _params(*semantics):
    return pltpu.CompilerParams(dimension_semantics=semantics, vmem_limit_bytes=VMEM_LIMIT)


def _gelu(x):
    return 0.5 * x * (1.0 + jnp.tanh(math.sqrt(2.0 / math.pi) * (x + 0.044715 * (x * x * x))))


def _adaln_kernel(ct_ref, w_ref, b_ref, o_ref):
    ct = ct_ref[...]
    cs = ct * jax.nn.sigmoid(ct)
    w = w_ref[...]
    for b in range(ct.shape[1]):
        o_ref[b:b + 1, :] = jnp.sum(w * cs[:, b:b + 1], axis=0, keepdims=True) + b_ref[...]


def _adaln(c, w, bias):
    bsz, d = c.shape
    n = w.shape[1]
    tn = 1024 if n % 1024 == 0 else n
    return pl.pallas_call(
        _adaln_kernel,
        grid=(n // tn,),
        in_specs=[pl.BlockSpec((d, bsz), lambda j: (0, 0)),
                  pl.BlockSpec((d, tn), lambda j: (0, j)),
                  pl.BlockSpec((1, tn), lambda j: (0, j))],
        out_specs=pl.BlockSpec((bsz, tn), lambda j: (0, j)),
        out_shape=jax.ShapeDtypeStruct((bsz, n), F32),
        compiler_params=_params("parallel"),
        name="adaln",
    )(c.T, w, bias.reshape(1, n))


def _rmsnorm_rows(x, g):
    return x * lax.rsqrt(jnp.mean(x * x, axis=-1, keepdims=True) + EPS) * g


def _rope_t(x3, cos, sin):
    half = x3.shape[1] // 2
    x1, x2 = x3[:, :half, :], x3[:, half:, :]
    return jnp.concatenate([x1 * cos - x2 * sin, x1 * sin + x2 * cos], axis=1)


def _pair_rows(x3):
    heads, dh, t = x3.shape
    zero = jnp.zeros_like(x3)
    even = lax.broadcasted_iota(jnp.int32, (heads, 1, 1), 0) % 2 == 0
    return jnp.concatenate([jnp.where(even, x3, zero), jnp.where(even, zero, x3)], axis=1).reshape(heads * 2 * dh, t)


def _in_proj_kernel(d_ssm, d_attn, dh, wi_scale,
                    x_ref, g_ref, sh_ref, sc_ref, w_ref, pos_ref, inv_ref, qg_ref, kg_ref,
                    ut_ref, qt_ref, k_ref, ksq_ref, vt_ref, qit_ref, ki_ref, wit_ref):
    hn = _rmsnorm_rows(x_ref[...], g_ref[...]) * (1.0 + sc_ref[0]) + sh_ref[0]
    hb = hn.astype(BF16)
    tm = hb.shape[0]

    def proj_t(r0, rows):
        return lax.dot_general(w_ref[:, r0:r0 + rows], hb, (((0,), (1,)), ((), ())),
                               preferred_element_type=F32)

    ang = inv_ref[...] * pos_ref[...].astype(F32)
    cos, sin = jnp.cos(ang)[None], jnp.sin(ang)[None]

    def head_norm(x3, gcol):
        return x3 * lax.rsqrt(jnp.mean(x3 * x3, axis=1, keepdims=True) + EPS) * gcol[None]

    ut_ref[...] = proj_t(0, d_ssm).reshape(d_ssm, tm // LANES, LANES)
    o = d_ssm
    n_heads = d_attn // dh
    q3 = proj_t(o, d_attn).reshape(n_heads, dh, tm)
    q3 = _rope_t(head_norm(q3, qg_ref[...]), cos, sin) * (dh ** -0.5 * LOG2E)
    qt_ref[...] = _pair_rows(q3).astype(BF16)
    o += d_attn
    k3 = _rope_t(head_norm(proj_t(o, d_attn).reshape(n_heads, dh, tm), kg_ref[...]), cos, sin)
    k_ref[...] = k3.reshape(d_attn, tm).T.astype(BF16)
    ksq = jnp.max(jnp.sum(k3 * k3, axis=1), axis=1, keepdims=True)
    ksq_ref[...] = jnp.broadcast_to(ksq, ksq_ref.shape)
    o += d_attn
    v3 = proj_t(o, d_attn).reshape(n_heads, dh, tm)
    vt = jnp.concatenate([v3, jnp.ones((n_heads, BF16_SUBLANES, tm), F32)], axis=1).astype(BF16)
    kc = vt_ref.shape[3]
    for j in range(tm // kc):
        vt_ref[j] = vt[:, :, j * kc:(j + 1) * kc]
    o += d_attn
    n_qi = IDX_HEADS * IDX_DIM
    qi3 = _rope_t(proj_t(o, n_qi).reshape(IDX_HEADS, IDX_DIM, tm), cos, sin)
    qit_ref[...] = _pair_rows(qi3).astype(BF16)
    o += n_qi
    tail = proj_t(o, IDX_DIM + BF16_SUBLANES)
    ki = _rope_t(tail[None, :IDX_DIM], cos, sin)[0]
    ki_ref[...] = jnp.concatenate([ki, ki], axis=0).T.astype(BF16)
    wit_ref[...] = tail[IDX_DIM:IDX_DIM + IDX_HEADS] * wi_scale


def _in_proj(x2, g, sh, sc, w_in, pos_row, q_norm, k_norm, seq, tm, kc, d_ssm, d_attn, dh):
    n, d = x2.shape
    bsz = n // seq
    tiles_per_batch = seq // tm
    n_qi = IDX_HEADS * IDX_DIM
    n_heads = d_attn // dh
    w = jnp.concatenate([w_in.astype(BF16), jnp.zeros((d, BF16_SUBLANES - IDX_HEADS), BF16)], axis=1)
    rt = w.shape[1]
    inv = (ROPE_THETA ** (-jnp.arange(0, dh, 2, dtype=F32) / dh)).reshape(dh // 2, 1)
    wi_scale = IDX_HEADS ** -0.5 * IDX_DIM ** -0.5

    row = lambda i: (i, 0)
    col = lambda i: (0, i)
    const = lambda i: (0, 0)
    per_batch = lambda i: (i // tiles_per_batch, 0, 0)
    return pl.pallas_call(
        functools.partial(_in_proj_kernel, d_ssm, d_attn, dh, wi_scale),
        grid=(n // tm,),
        in_specs=[pl.BlockSpec((tm, d), row),
                  pl.BlockSpec((1, d), const),
                  pl.BlockSpec((1, 1, d), per_batch),
                  pl.BlockSpec((1, 1, d), per_batch),
                  pl.BlockSpec((d, rt), const),
                  pl.BlockSpec((1, tm), col),
                  pl.BlockSpec((dh // 2, 1), const),
                  pl.BlockSpec((dh, 1), const),
                  pl.BlockSpec((dh, 1), const)],
        out_specs=[pl.BlockSpec((d_ssm, tm // LANES, LANES), lambda i: (0, i, 0)),
                   pl.BlockSpec((2 * d_attn, tm), col),
                   pl.BlockSpec((tm, d_attn), row),
                   pl.BlockSpec((n_heads, LANES), row),
                   pl.BlockSpec((tm // kc, n_heads, dh + BF16_SUBLANES, kc), lambda i: (i, 0, 0, 0)),
                   pl.BlockSpec((2 * n_qi, tm), col),
                   pl.BlockSpec((tm, 2 * IDX_DIM), row),
                   pl.BlockSpec((IDX_HEADS, tm), col)],
        out_shape=[jax.ShapeDtypeStruct((d_ssm, n // LANES, LANES), F32),
                   jax.ShapeDtypeStruct((2 * d_attn, n), BF16),
                   jax.ShapeDtypeStruct((n, d_attn), BF16),
                   jax.ShapeDtypeStruct((n // tm * n_heads, LANES), F32),
                   jax.ShapeDtypeStruct((n // kc, n_heads, dh + BF16_SUBLANES, kc), BF16),
                   jax.ShapeDtypeStruct((2 * n_qi, n), BF16),
                   jax.ShapeDtypeStruct((n, 2 * IDX_DIM), BF16),
                   jax.ShapeDtypeStruct((IDX_HEADS, n), F32)],
        compiler_params=_params("parallel"),
        name="in_proj",
    )(x2, g.reshape(1, d), sh.reshape(bsz, 1, d), sc.reshape(bsz, 1, d), w, pos_row, inv,
      q_norm.reshape(dh, 1), k_norm.reshape(dh, 1))


def _cmul(ar, ai, br, bi):
    return ar * br - ai * bi, ar * bi + ai * br


def _s5_kernel(chunks_per_batch,
               ut_ref, ldt_ref, ar_row_ref, ai_row_ref, ar_col_ref, ai_col_ref,
               btr_ref, bti_ref, cr_ref, ci_ref, ctr_ref, cti_ref, d_ref,
               y_ref, k2_scr, tm_scr, w_scr, v_scr):
    nc, p, t = btr_ref.shape[1], btr_ref.shape[2], SSM_CHUNK
    dt = jnp.exp(ldt_ref[0])
    lr, li = ar_row_ref[0], ai_row_ref[0]
    mag = jnp.exp(lr * dt)
    lbr, lbi = mag * jnp.cos(li * dt), mag * jnp.sin(li * dt)
    den = lr * lr + li * li
    nr, ni = lbr - 1.0, lbi
    coef_r, coef_i = (nr * lr + ni * li) / den, (ni * lr - nr * li) / den
    bbr, bbi = _cmul(coef_r, coef_i, btr_ref[0], bti_ref[0])

    def powers(rate, freq, n):
        m = jnp.exp(rate * n)
        return m * jnp.cos(freq * n), m * jnp.sin(freq * n)

    rate_c, freq_c = ar_col_ref[0] * dt, ai_col_ref[0] * dt
    rate_r, freq_r = lr * dt, li * dt
    tau = lax.broadcasted_iota(jnp.int32, (1, t), 1).astype(F32)
    pr0, pi0 = powers(rate_c, freq_c, tau)
    pr1, pi1 = powers(rate_c, freq_c, tau + 1.0)
    back = (t - 1) - lax.broadcasted_iota(jnp.int32, (t, 1), 0)
    qr, qi = powers(rate_r, freq_r, back.astype(F32))

    cr, ci = cr_ref[0], ci_ref[0]
    rows = []
    for c in range(nc):
        gr, gi = _cmul(cr[c:c + 1], ci[c:c + 1], bbr, bbi)
        rows.append(jnp.concatenate([gr, -gi], axis=1))
    k2_scr[...] = jnp.dot(jnp.concatenate(rows, axis=0), jnp.concatenate([pr0, pi0], axis=0),
                          precision=HIGHEST, preferred_element_type=F32)

    causal = (lax.broadcasted_iota(jnp.int32, (t, t), 1) >= lax.broadcasted_iota(jnp.int32, (t, t), 0))

    def build(cp, carry):
        r0 = pl.multiple_of(cp * t, t)
        for c in range(nc):
            krow = jnp.broadcast_to(k2_scr[pl.ds(c * nc + cp, 1), :], (t, t))
            tile = pltpu.roll(krow, 0, 1, stride=1, stride_axis=0)
            tm_scr[pl.ds(r0, t), c * t:(c + 1) * t] = jnp.where(causal, tile, 0.0).astype(BF16)
        return carry

    lax.fori_loop(0, nc, build, 0)

    a = jnp.concatenate([ut_ref[c] for c in range(nc)], axis=1).astype(BF16)
    y = jnp.dot(a, tm_scr[...], preferred_element_type=F32)

    for c in range(nc):
        wr, wi = _cmul(qr, qi, bbr[c:c + 1], bbi[c:c + 1])
        w_scr[c * t:(c + 1) * t, :] = jnp.concatenate([wr, wi], axis=1).astype(BF16)
    s = jnp.dot(a, w_scr[...], preferred_element_type=F32)

    r = s.shape[0]
    kidx = lax.broadcasted_iota(jnp.int32, (r, 1), 0) % chunks_per_batch

    def times_lam_pow(z, n):
        zr, zi = powers(rate_r, freq_r, float(n))
        return (jnp.concatenate([zr, zr], axis=1) * z
                + jnp.concatenate([-zi, zi], axis=1) * pltpu.roll(z, p, 1))

    step = 1
    while step < chunks_per_batch:
        prev = pltpu.roll(s, step, 0)
        s = s + jnp.where(kidx >= step, times_lam_pow(prev, t * step), 0.0)
        step *= 2
    e = jnp.where(kidx >= 1, pltpu.roll(s, 1, 0), 0.0)

    ctr, cti = ctr_ref[0], cti_ref[0]
    for c in range(nc):
        vr, vi = _cmul(ctr[:, c:c + 1], cti[:, c:c + 1], pr1, pi1)
        v_scr[:, c * t:(c + 1) * t] = jnp.concatenate([vr, -vi], axis=0).astype(BF16)
    y = y + jnp.dot(e.astype(BF16), v_scr[...], preferred_element_type=F32)

    for c in range(nc):
        y_ref[c] = y[:, c * t:(c + 1) * t] + d_ref[c] * ut_ref[c]


def _s5(ut3, seq, log_dt, a_re, a_im, b_re, b_im, c_re, c_im, d_skip):
    d_ssm, r, t = ut3.shape
    assert t == SSM_CHUNK
    n = r * t
    g, p = a_re.shape
    nc = b_re.shape[2]
    grp = lambda i: (i, 0, 0)
    row_spec = pl.BlockSpec((1, 1, p), grp)
    col_spec = pl.BlockSpec((1, p, 1), grp)
    cp_spec = pl.BlockSpec((1, nc, p), grp)
    pc_spec = pl.BlockSpec((1, p, nc), grp)
    return pl.pallas_call(
        functools.partial(_s5_kernel, seq // t),
        grid=(g,),
        in_specs=[pl.BlockSpec((nc, r, t), grp),
                  pl.BlockSpec((1, 1, 1), grp),
                  row_spec, row_spec, col_spec, col_spec,
                  cp_spec, cp_spec, cp_spec, cp_spec, pc_spec, pc_spec,
                  pl.BlockSpec((nc, 1, 1), grp)],
        out_specs=pl.BlockSpec((nc, r, t), grp),
        out_shape=jax.ShapeDtypeStruct((d_ssm, r, t), F32),
        scratch_shapes=[pltpu.VMEM((nc * nc, t), F32),
                        pltpu.VMEM((nc * t, nc * t), BF16),
                        pltpu.VMEM((nc * t, 2 * p), BF16),
                        pltpu.VMEM((2 * p, nc * t), BF16)],
        compiler_params=_params("parallel"),
        name="s5",
    )(ut3, log_dt.reshape(g, 1, 1),
      a_re.reshape(g, 1, p), a_im.reshape(g, 1, p), a_re.reshape(g, p, 1), a_im.reshape(g, p, 1),
      jnp.swapaxes(b_re, 1, 2), jnp.swapaxes(b_im, 1, 2), c_re, c_im,
      jnp.swapaxes(c_re, 1, 2), jnp.swapaxes(c_im, 1, 2), d_skip.reshape(d_ssm, 1, 1))


def _dsa_kernel(topk, seq, tq, kc, dh, bisect_passes,
                qt_ref, qit_ref, wit_ref, k_ref, ksq_ref, ki_ref, vt_ref, o_ref,
                i_scr, m_scr, l_scr, acc_scr):
    i = pl.program_id(1)
    n_heads = acc_scr.shape[0]
    n_chunks = (i * tq + tq + kc - 1) // kc

    def head_rows(ref, h):
        return ref[h * LANES:(h + 1) * LANES, :]

    qpos = i * tq + lax.broadcasted_iota(jnp.int32, (1, tq), 1)
    krow = lax.broadcasted_iota(jnp.int32, (kc, 1), 0)
    wi = wit_ref[...]

    def index_chunk(c, carry):
        lo, hi = carry
        r0 = pl.multiple_of(c * kc, kc)
        ki = ki_ref[pl.ds(r0, kc), :]
        score = jnp.zeros((kc, tq), F32)
        ahead = 2
        dots = [jnp.dot(ki, head_rows(qit_ref, h), preferred_element_type=F32) for h in range(ahead)]
        for h in range(IDX_HEADS):
            if h + ahead < IDX_HEADS:
                dots.append(jnp.dot(ki, head_rows(qit_ref, h + ahead), preferred_element_type=F32))
            score = score + wi[h:h + 1, :] * jnp.maximum(dots[h], 0.0)
        i_scr[c] = jnp.where(r0 + krow <= qpos, score, -jnp.inf)
        return (jnp.minimum(lo, jnp.min(score, axis=0, keepdims=True)),
                jnp.maximum(hi, jnp.max(score, axis=0, keepdims=True)))

    lo, hi = lax.fori_loop(0, n_chunks, index_chunk,
                           (jnp.full((1, tq), jnp.inf, F32), jnp.full((1, tq), -jnp.inf, F32)))

    def over_chunks(per_chunk, combine, init):
        def body(c, acc):
            part = per_chunk(i_scr[c]).reshape(kc // PART_ROWS, PART_ROWS, tq)
            return combine(acc, part)
        return lax.fori_loop(0, n_chunks, body, jnp.full((PART_ROWS, tq), init, F32))

    def count(pred):
        parts = over_chunks(lambda x: jnp.where(pred(x), 1.0, 0.0),
                            lambda acc, part: acc + jnp.sum(part, axis=0), 0.0)
        return jnp.sum(parts, axis=0, keepdims=True)

    def max_below(bound):
        parts = over_chunks(lambda x: jnp.where(x < bound, x, -jnp.inf),
                            lambda acc, part: jnp.maximum(acc, jnp.max(part, axis=0)), -jnp.inf)
        return jnp.max(parts, axis=0, keepdims=True)

    kf = float(topk)
    n_causal = (qpos + 1).astype(F32)
    hi = hi + jnp.abs(hi) * 2.0 ** -20 + 1e-30

    def bisect(_, st):
        lo, hi, c_lo, active = st
        mid = 0.5 * lo + 0.5 * hi
        c = count(lambda x: x >= mid)
        upd = jnp.logical_and(active > 0.0, jnp.logical_and(mid > lo, mid < hi))
        up_lo = jnp.logical_and(upd, c >= kf)
        up_hi = jnp.logical_and(upd, c < kf)
        lo = jnp.where(up_lo, mid, lo)
        c_lo = jnp.where(up_lo, c, c_lo)
        hi = jnp.where(up_hi, mid, hi)
        return lo, hi, c_lo, jnp.where(c_lo > kf, active, 0.0)

    active0 = jnp.where(n_causal > kf, 1.0, 0.0)
    lo, hi, c_lo, active = lax.fori_loop(0, bisect_passes, bisect, (lo, hi, n_causal, active0))

    def stepping(st):
        return jnp.max(st[3]) > 0.0

    def step_down(st):
        lo, hi, c_lo, active = st
        v = max_below(hi)
        c = count(lambda x: x >= v)
        on = active > 0.0
        found = jnp.logical_and(on, c >= kf)
        lo = jnp.where(found, v, lo)
        c_lo = jnp.where(found, c, c_lo)
        hi = jnp.where(jnp.logical_and(on, c < kf), v, hi)
        return lo, hi, c_lo, jnp.where(found, 0.0, active)

    thr, _, c_thr, _ = lax.while_loop(stepping, step_down, (lo, hi, c_lo, active))
    tied = jnp.max(jnp.where(c_thr > kf, 1.0, 0.0)) > 0.0

    qsq = jnp.zeros((1, tq), F32)
    for h in range(n_heads):
        qh = head_rows(qt_ref, h).astype(F32)
        qsq = jnp.maximum(qsq, jnp.sum(qh * qh, axis=0, keepdims=True))
    ksq = jnp.max(jnp.max(ksq_ref[...], axis=0, keepdims=True), axis=1, keepdims=True)
    shift = -1.02 * jnp.sqrt(qsq * ksq)

    @pl.when(jnp.logical_not(tied))
    def _():
        def mask_chunk(c, carry):
            i_scr[c] = jnp.where(i_scr[c] >= thr, shift, NEG_BIG)
            return carry
        lax.fori_loop(0, n_chunks, mask_chunk, 0)

    @pl.when(tied)
    def _():
        above = count(lambda x: x > thr)
        room = jnp.where(c_thr > kf, kf - above, float(seq))
        tri = (lax.broadcasted_iota(jnp.int32, (kc, kc), 0)
               >= lax.broadcasted_iota(jnp.int32, (kc, kc), 1)).astype(BF16)

        def mask_chunk(c, seen):
            x = i_scr[c]
            eq = jnp.where(x == thr, 1.0, 0.0)
            rank = jnp.dot(tri, eq.astype(BF16), preferred_element_type=F32) + seen
            keep = jnp.logical_or(x > thr, jnp.logical_and(x == thr, rank <= room))
            i_scr[c] = jnp.where(keep, shift, NEG_BIG)
            return seen + jnp.sum(eq, axis=0, keepdims=True)
        lax.fori_loop(0, n_chunks, mask_chunk, jnp.zeros((1, tq), F32))

    def logits(c, h):
        r0 = pl.multiple_of(c * kc, kc)
        kk = k_ref[pl.ds(r0, kc), (h // 2) * LANES:(h // 2 + 1) * LANES]
        return jnp.dot(kk, head_rows(qt_ref, h), preferred_element_type=F32) + i_scr[c]

    acc_scr[...] = jnp.zeros(acc_scr.shape, F32)

    def attend_chunk(c, carry):
        ahead = 4
        probs = [jnp.exp2(logits(c, h)).astype(BF16) for h in range(ahead)]
        for h in range(n_heads):
            if h + ahead < n_heads:
                probs.append(jnp.exp2(logits(c, h + ahead)).astype(BF16))
            acc_scr[h] += jnp.dot(vt_ref[c, h], probs[h], preferred_element_type=F32)
        return carry

    lax.fori_loop(0, n_chunks, attend_chunk, 0)

    l_min = jnp.full((1, tq), jnp.inf, F32)
    for h in range(n_heads):
        denom = acc_scr[h, dh:dh + 1, :]
        l_min = jnp.minimum(l_min, denom)
        o_ref[h * dh:(h + 1) * dh, :] = acc_scr[h, 0:dh, :] / denom

    @pl.when(jnp.min(l_min) < 2.0 ** -64)
    def _():
        m_scr[...] = jnp.full(m_scr.shape, NEG_BIG, F32)
        l_scr[...] = jnp.zeros(l_scr.shape, F32)
        acc_scr[...] = jnp.zeros(acc_scr.shape, F32)

        def attend_chunk_online(c, carry):
            for h in range(n_heads):
                s = logits(c, h)
                m_old = m_scr[h:h + 1, :]
                m_new = jnp.maximum(m_old, jnp.max(s, axis=0, keepdims=True))
                alpha = jnp.exp2(m_old - m_new)
                p = jnp.exp2(s - m_new)
                l_scr[h:h + 1, :] = alpha * l_scr[h:h + 1, :] + jnp.sum(p, axis=0, keepdims=True)
                m_scr[h:h + 1, :] = m_new
                pv = jnp.dot(vt_ref[c, h, 0:dh, :], p.astype(BF16), preferred_element_type=F32)
                acc_scr[h, 0:dh, :] = alpha * acc_scr[h, 0:dh, :] + pv
            return carry

        lax.fori_loop(0, n_chunks, attend_chunk_online, 0)
        for h in range(n_heads):
            o_ref[h * dh:(h + 1) * dh, :] = acc_scr[h, 0:dh, :] / l_scr[h:h + 1, :]


def _dsa(qt, qit, wit, k, ksq, ki, vt4, seq, tq, dh):
    n, d_attn = k.shape
    bsz = n // seq
    kc = vt4.shape[3]
    n_heads = d_attn // dh
    n_qi = IDX_HEADS * IDX_DIM
    assert 2 * dh == LANES and 2 * IDX_DIM == LANES and n_heads % 2 == 0 and IDX_HEADS % 2 == 0
    assert kc % PART_ROWS == 0
    topk = min(TOPK_MAX, seq // 4)
    blocks_per_batch = seq // tq
    chunks_per_batch = seq // kc
    bisect_passes = 14
    qcol = lambda b, i: (0, b * blocks_per_batch + i)
    per_batch = lambda b, i: (b, 0)
    return pl.pallas_call(
        functools.partial(_dsa_kernel, topk, seq, tq, kc, dh, bisect_passes),
        grid=(bsz, blocks_per_batch),
        in_specs=[pl.BlockSpec((2 * d_attn, tq), qcol),
                  pl.BlockSpec((2 * n_qi, tq), qcol),
                  pl.BlockSpec((IDX_HEADS, tq), qcol),
                  pl.BlockSpec((seq, d_attn), per_batch),
                  pl.BlockSpec((ksq.shape[0] // bsz, LANES), per_batch),
                  pl.BlockSpec((seq, 2 * IDX_DIM), per_batch),
                  pl.BlockSpec((chunks_per_batch, n_heads, dh + BF16_SUBLANES, kc), lambda b, i: (b, 0, 0, 0))],
        out_specs=pl.BlockSpec((d_attn, tq), qcol),
        out_shape=jax.ShapeDtypeStruct((d_attn, n), F32),
        scratch_shapes=[pltpu.VMEM((chunks_per_batch, kc, tq), F32),
                        pltpu.VMEM((n_heads, tq), F32),
                        pltpu.VMEM((n_heads, tq), F32),
                        pltpu.VMEM((n_heads, dh + BF16_SUBLANES, tq), F32)],
        compiler_params=_params("parallel", "arbitrary"),
        name="dsa",
    )(qt, qit, wit, k, ksq, ki, vt4)


def _mix_out_kernel(yt_ref, yat_ref, x_ref, g1_ref, sh_ref, sc_ref, gs_ref, ga_ref, wglu_ref, bglu_ref,
                    wo_ref, gf_ref, h_ref, hn_ref):
    def norm_cols(v, g):
        return v * lax.rsqrt(jnp.mean(v * v, axis=0, keepdims=True) + EPS) * g

    d_ssm, chunks, steps = yt_ref.shape
    y = _gelu(yt_ref[...].reshape(d_ssm, chunks * steps))
    z = jnp.dot(wglu_ref[...], y.astype(BF16), preferred_element_type=F32) + bglu_ref[...]
    y = y * jax.nn.sigmoid(z)
    mix_t = jnp.concatenate([norm_cols(y, gs_ref[...]), norm_cols(yat_ref[...], ga_ref[...])], axis=0)
    mix = jnp.dot(mix_t.T.astype(BF16), wo_ref[...], preferred_element_type=F32)
    h = x_ref[...] + g1_ref[0] * mix
    h_ref[...] = h
    hn_ref[...] = (_rmsnorm_rows(h, gf_ref[...]) * (1.0 + sc_ref[0]) + sh_ref[0]).astype(BF16)


def _mix_out(yt3, yat, x2, g1, sh2, sc2, gs, ga, w_glu, b_glu, w_out, gf, seq, tm):
    d_ssm, _, steps = yt3.shape
    n, d = x2.shape
    d_attn = yat.shape[0]
    bsz = n // seq
    tiles_per_batch = seq // tm
    row = lambda i: (i, 0)
    col = lambda i: (0, i)
    const = lambda i: (0, 0)
    per_batch = lambda i: (i // tiles_per_batch, 0, 0)
    mod_spec = pl.BlockSpec((1, 1, d), per_batch)
    return pl.pallas_call(
        _mix_out_kernel,
        grid=(n // tm,),
        in_specs=[pl.BlockSpec((d_ssm, tm // steps, steps), lambda i: (0, i, 0)),
                  pl.BlockSpec((d_attn, tm), col),
                  pl.BlockSpec((tm, d), row),
                  mod_spec, mod_spec, mod_spec,
                  pl.BlockSpec((d_ssm, 1), const),
                  pl.BlockSpec((d_attn, 1), const),
                  pl.BlockSpec((d_ssm, d_ssm), const),
                  pl.BlockSpec((d_ssm, 1), const),
                  pl.BlockSpec((d_ssm + d_attn, d), const),
                  pl.BlockSpec((1, d), const)],
        out_specs=[pl.BlockSpec((tm, d), row), pl.BlockSpec((tm, d), row)],
        out_shape=[jax.ShapeDtypeStruct((n, d), F32), jax.ShapeDtypeStruct((n, d), BF16)],
        compiler_params=_params("parallel"),
        name="mix_out",
    )(yt3, yat, x2, g1.reshape(bsz, 1, d), sh2.reshape(bsz, 1, d), sc2.reshape(bsz, 1, d),
      gs.reshape(d_ssm, 1), ga.reshape(d_attn, 1), w_glu.T.astype(BF16), b_glu.reshape(d_ssm, 1),
      w_out.astype(BF16), gf.reshape(1, d))


def _ffn_kernel(tiles_per_batch, tf,
                hn_ref, halo_ref, h_ref, g2_ref, wu_ref, cw_ref, cb_ref, wd_ref,
                o_ref, lhs_scr, act_scr):
    i = pl.program_id(0)
    halo_rows = halo_ref.shape[0]
    lhs_scr[0:halo_rows, :] = jnp.where(i % tiles_per_batch == 0, 0, halo_ref[...]).astype(BF16)
    lhs_scr[halo_rows:, :] = hn_ref[...]

    d_ff = act_scr.shape[1]

    def conv_up(c0):
        up = jnp.dot(lhs_scr[...], wu_ref[:, c0:c0 + tf], preferred_element_type=F32)
        conv = cb_ref[:, c0:c0 + tf] + cw_ref[CONV_W - 1:CONV_W, c0:c0 + tf] * up
        for back in range(1, CONV_W):
            tap = CONV_W - 1 - back
            conv = conv + cw_ref[tap:tap + 1, c0:c0 + tf] * pltpu.roll(up, back, 0)
        return conv[halo_rows:]

    for c0 in range(0, d_ff, tf):
        act_scr[:, c0:c0 + tf] = (_gelu(conv_up(c0)) * conv_up(d_ff + c0)).astype(BF16)

    down = jnp.dot(act_scr[...], wd_ref[...], preferred_element_type=F32)
    o_ref[...] = h_ref[...] + g2_ref[0] * down


def _ffn(hn2, h1, g2, w_up, conv_w, conv_b, w_down, seq, tm, tf):
    n, d = h1.shape
    d_ff = w_down.shape[0]
    bsz = n // seq
    tiles_per_batch = seq // tm
    halo_rows = BF16_SUBLANES
    assert halo_rows >= CONV_W - 1 and tm % halo_rows == 0 and d_ff % tf == 0
    halo_per_tile = tm // halo_rows
    row = lambda i: (i, 0)
    const = lambda i: (0, 0)
    resident = pl.Buffered(1)

    return pl.pallas_call(
        functools.partial(_ffn_kernel, tiles_per_batch, tf),
        grid=(n // tm,),
        in_specs=[pl.BlockSpec((tm, d), row),
                  pl.BlockSpec((halo_rows, d), lambda i: (jnp.maximum(i * halo_per_tile - 1, 0), 0)),
                  pl.BlockSpec((tm, d), row),
                  pl.BlockSpec((1, 1, d), lambda i: (i // tiles_per_batch, 0, 0)),
                  pl.BlockSpec((d, 2 * d_ff), const, pipeline_mode=resident),
                  pl.BlockSpec((CONV_W, 2 * d_ff), const, pipeline_mode=resident),
                  pl.BlockSpec((1, 2 * d_ff), const, pipeline_mode=resident),
                  pl.BlockSpec((d_ff, d), const, pipeline_mode=resident)],
        out_specs=pl.BlockSpec((tm, d), row),
        out_shape=jax.ShapeDtypeStruct((n, d), F32),
        scratch_shapes=[pltpu.VMEM((halo_rows + tm, d), BF16),
                        pltpu.VMEM((tm, d_ff), BF16)],
        compiler_params=_params("parallel"),
        name="ffn",
    )(hn2, hn2, h1, g2.reshape(bsz, 1, d), w_up.astype(BF16), conv_w, conv_b.reshape(1, 2 * d_ff),
      w_down.astype(BF16))


def _tile(total, want):
    t = min(total, want)
    assert total % t == 0
    return t


def kernel(x, c, positions, w_ada, b_ada, norm_mix, w_in, ssm_log_dt, ssm_a_re, ssm_a_im, ssm_b_re, ssm_b_im, ssm_c_re, ssm_c_im, ssm_d, ssm_w_glu, ssm_b_glu, q_norm, k_norm, out_norm_ssm, out_norm_attn, w_out, norm_ffn, w_up, conv_w, conv_b, w_down):
    bsz, seq, d = x.shape
    depth = w_ada.shape[0]
    n = bsz * seq
    d_ssm = out_norm_ssm.shape[1]
    d_attn = out_norm_attn.shape[1]
    dh = q_norm.shape[1]
    d_ff = w_down.shape[1]
    assert w_in.shape[2] == d_ssm + 3 * d_attn + IDX_HEADS * IDX_DIM + IDX_DIM + IDX_HEADS
    assert seq % SSM_CHUNK == 0

    t_proj = _tile(seq, 1024)
    t_key = _tile(seq, 512)
    t_q = _tile(seq, 256)
    t_mix = _tile(seq, 1024)
    t_ffn = _tile(seq, 512)
    t_ff = _tile(d_ff, 256)

    h = x.astype(F32).reshape(n, d)
    pos_row = positions.reshape(1, n)
    for l in range(depth):
        mod = _adaln(c.astype(F32), w_ada[l].astype(F32), b_ada[l].astype(F32))
        sh1, sc1, g1, sh2, sc2, g2 = jnp.split(mod, 6, axis=-1)
        ut, qt, k, ksq, vt4, qit, ki, wit = _in_proj(h, norm_mix[l], sh1, sc1, w_in[l], pos_row, q_norm[l],
                                                     k_norm[l], seq, t_proj, t_key, d_ssm, d_attn, dh)
        yt = _s5(ut, seq, ssm_log_dt[l], ssm_a_re[l], ssm_a_im[l], ssm_b_re[l], ssm_b_im[l],
                 ssm_c_re[l], ssm_c_im[l], ssm_d[l])
        yat = _dsa(qt, qit, wit, k, ksq, ki, vt4, seq, t_q, dh)
        h1, hn2 = _mix_out(yt, yat, h, g1, sh2, sc2, out_norm_ssm[l], out_norm_attn[l], ssm_w_glu[l],
                           ssm_b_glu[l], w_out[l], norm_ffn[l], seq, t_mix)
        h = _ffn(hn2, h1, g2, w_up[l], conv_w[l], conv_b[l], w_down[l], seq, t_ffn, t_ff)
    return h.reshape(bsz, seq, d).astype(x.dtype)
```

```python
import functools
import math

import jax
import jax.numpy as jnp
from jax import lax
from jax.experimental import pallas as pl
from jax.experimental.pallas import tpu as pltpu

EPS = 1e-6
ROPE_THETA = 10000.0
IDX_HEADS = 8
IDX_DIM = 64
TOPK_MAX = 256
CONV_W = 3

LANES = 128
BF16_SUBLANES = 16
SSM_CHUNK = LANES
TOEPLITZ_BLOCK = 2
PART_ROWS = 32
VMEM_LIMIT = 56 * 1024 * 1024
NEG_BIG = -1e30
LOG2E = math.log2(math.e)

F32 = jnp.float32
BF16 = jnp.bfloat16
HIGHEST = lax.Precision.HIGHEST


def _params(*semantics):
    return pltpu.CompilerParams(dimension_semantics=semantics, vmem_limit_bytes=VMEM_LIMIT)


def _gelu(x):
    return 0.5 * x * (1.0 + jnp.tanh(math.sqrt(2.0 / math.pi) * (x + 0.044715 * (x * x * x))))


def _adaln_kernel(ct_ref, w_ref, b_ref, o_ref):
    ct = ct_ref[...]
    cs = ct * jax.nn.sigmoid(ct)
    w = w_ref[...]
    for b in range(ct.shape[1]):
        o_ref[b:b + 1, :] = jnp.sum(w * cs[:, b:b + 1], axis=0, keepdims=True) + b_ref[...]


def _adaln(c, w, bias):
    bsz, d = c.shape
    n = w.shape[1]
    tn = 1024 if n % 1024 == 0 else n
    return pl.pallas_call(
        _adaln_kernel,
        grid=(n // tn,),
        in_specs=[pl.BlockSpec((d, bsz), lambda j: (0, 0)),
                  pl.BlockSpec((d, tn), lambda j: (0, j)),
                  pl.BlockSpec((1, tn), lambda j: (0, j))],
        out_specs=pl.BlockSpec((bsz, tn), lambda j: (0, j)),
        out_shape=jax.ShapeDtypeStruct((bsz, n), F32),
        compiler_params=_params("parallel"),
        name="adaln",
    )(c.T, w, bias.reshape(1, n))


def _rmsnorm_rows(x, g):
    return x * lax.rsqrt(jnp.mean(x * x, axis=-1, keepdims=True) + EPS) * g


def _rope_t(x3, cos, sin):
    half = x3.shape[1] // 2
    x1, x2 = x3[:, :half, :], x3[:, half:, :]
    return jnp.concatenate([x1 * cos - x2 * sin, x1 * sin + x2 * cos], axis=1)


def _pair_rows(x3):
    heads, dh, t = x3.shape
    zero = jnp.zeros_like(x3)
    even = lax.broadcasted_iota(jnp.int32, (heads, 1, 1), 0) % 2 == 0
    return jnp.concatenate([jnp.where(even, x3, zero), jnp.where(even, zero, x3)], axis=1).reshape(heads * 2 * dh, t)


def _in_proj_kernel(d_ssm, d_attn, dh, wi_scale,
                    x_ref, g_ref, sh_ref, sc_ref, w_ref, pos_ref, inv_ref, qg_ref, kg_ref,
                    ut_ref, qt_ref, k_ref, ksq_ref, vt_ref, qit_ref, ki_ref, wit_ref):
    hn = _rmsnorm_rows(x_ref[...], g_ref[...]) * (1.0 + sc_ref[0]) + sh_ref[0]
    hb = hn.astype(BF16)
    tm = hb.shape[0]

    def proj_t(r0, rows):
        return lax.dot_general(w_ref[:, r0:r0 + rows], hb, (((0,), (1,)), ((), ())),
                               preferred_element_type=F32)

    ang = inv_ref[...] * pos_ref[...].astype(F32)
    cos, sin = jnp.cos(ang)[None], jnp.sin(ang)[None]

    def head_norm(x3, gcol):
        return x3 * lax.rsqrt(jnp.mean(x3 * x3, axis=1, keepdims=True) + EPS) * gcol[None]

    ut_ref[...] = proj_t(0, d_ssm).reshape(d_ssm, tm // LANES, LANES)
    o = d_ssm
    n_heads = d_attn // dh
    q3 = proj_t(o, d_attn).reshape(n_heads, dh, tm)
    q3 = _rope_t(head_norm(q3, qg_ref[...]), cos, sin) * (dh ** -0.5 * LOG2E)
    qt_ref[...] = _pair_rows(q3).astype(BF16)
    o += d_attn
    k3 = _rope_t(head_norm(proj_t(o, d_attn).reshape(n_heads, dh, tm), kg_ref[...]), cos, sin)
    k_ref[...] = k3.reshape(d_attn, tm).T.astype(BF16)
    ksq = jnp.max(jnp.sum(k3 * k3, axis=1), axis=1, keepdims=True)
    ksq_ref[...] = jnp.broadcast_to(ksq, ksq_ref.shape)
    o += d_attn
    v3 = proj_t(o, d_attn).reshape(n_heads, dh, tm)
    vt = jnp.concatenate([v3, jnp.ones((n_heads, BF16_SUBLANES, tm), F32)], axis=1).astype(BF16)
    kc = vt_ref.shape[3]
    for j in range(tm // kc):
        vt_ref[j] = vt[:, :, j * kc:(j + 1) * kc]
    o += d_attn
    n_qi = IDX_HEADS * IDX_DIM
    qi3 = _rope_t(proj_t(o, n_qi).reshape(IDX_HEADS, IDX_DIM, tm), cos, sin)
    qit_ref[...] = _pair_rows(qi3).astype(BF16)
    o += n_qi
    tail = proj_t(o, IDX_DIM + BF16_SUBLANES)
    ki = _rope_t(tail[None, :IDX_DIM], cos, sin)[0]
    ki_ref[...] = jnp.concatenate([ki, ki], axis=0).T.astype(BF16)
    wit_ref[...] = tail[IDX_DIM:IDX_DIM + IDX_HEADS] * wi_scale


def _in_proj(x2, g, sh, sc, w_in, pos_row, q_norm, k_norm, seq, tm, kc, d_ssm, d_attn, dh):
    n, d = x2.shape
    bsz = n // seq
    tiles_per_batch = seq // tm
    n_qi = IDX_HEADS * IDX_DIM
    n_heads = d_attn // dh
    w = jnp.concatenate([w_in.astype(BF16), jnp.zeros((d, BF16_SUBLANES - IDX_HEADS), BF16)], axis=1)
    rt = w.shape[1]
    inv = (ROPE_THETA ** (-jnp.arange(0, dh, 2, dtype=F32) / dh)).reshape(dh // 2, 1)
    wi_scale = IDX_HEADS ** -0.5 * IDX_DIM ** -0.5

    row = lambda i: (i, 0)
    col = lambda i: (0, i)
    const = lambda i: (0, 0)
    per_batch = lambda i: (i // tiles_per_batch, 0, 0)
    return pl.pallas_call(
        functools.partial(_in_proj_kernel, d_ssm, d_attn, dh, wi_scale),
        grid=(n // tm,),
        in_specs=[pl.BlockSpec((tm, d), row),
                  pl.BlockSpec((1, d), const),
                  pl.BlockSpec((1, 1, d), per_batch),
                  pl.BlockSpec((1, 1, d), per_batch),
                  pl.BlockSpec((d, rt), const),
                  pl.BlockSpec((1, tm), col),
                  pl.BlockSpec((dh // 2, 1), const),
                  pl.BlockSpec((dh, 1), const),
                  pl.BlockSpec((dh, 1), const)],
        out_specs=[pl.BlockSpec((d_ssm, tm // LANES, LANES), lambda i: (0, i, 0)),
                   pl.BlockSpec((2 * d_attn, tm), col),
                   pl.BlockSpec((tm, d_attn), row),
                   pl.BlockSpec((n_heads, LANES), row),
                   pl.BlockSpec((tm // kc, n_heads, dh + BF16_SUBLANES, kc), lambda i: (i, 0, 0, 0)),
                   pl.BlockSpec((2 * n_qi, tm), col),
                   pl.BlockSpec((tm, 2 * IDX_DIM), row),
                   pl.BlockSpec((IDX_HEADS, tm), col)],
        out_shape=[jax.ShapeDtypeStruct((d_ssm, n // LANES, LANES), F32),
                   jax.ShapeDtypeStruct((2 * d_attn, n), BF16),
                   jax.ShapeDtypeStruct((n, d_attn), BF16),
                   jax.ShapeDtypeStruct((n // tm * n_heads, LANES), F32),
                   jax.ShapeDtypeStruct((n // kc, n_heads, dh + BF16_SUBLANES, kc), BF16),
                   jax.ShapeDtypeStruct((2 * n_qi, n), BF16),
                   jax.ShapeDtypeStruct((n, 2 * IDX_DIM), BF16),
                   jax.ShapeDtypeStruct((IDX_HEADS, n), F32)],
        compiler_params=_params("parallel"),
        name="in_proj",
    )(x2, g.reshape(1, d), sh.reshape(bsz, 1, d), sc.reshape(bsz, 1, d), w, pos_row, inv,
      q_norm.reshape(dh, 1), k_norm.reshape(dh, 1))


def _cmul(ar, ai, br, bi):
    return ar * br - ai * bi, ar * bi + ai * br


def _s5_kernel(chunks_per_batch,
               ut_ref, ldt_ref, ar_row_ref, ai_row_ref, ar_col_ref, ai_col_ref,
               btr_ref, bti_ref, cr_ref, ci_ref, ctr_ref, cti_ref, d_ref,
               y_ref, k2_scr, tm_scr, w_scr, v_scr, yi_scr):
    nc, p, t = btr_ref.shape[1], btr_ref.shape[2], SSM_CHUNK
    dt = jnp.exp(ldt_ref[0])
    lr, li = ar_row_ref[0], ai_row_ref[0]
    mag = jnp.exp(lr * dt)
    lbr, lbi = mag * jnp.cos(li * dt), mag * jnp.sin(li * dt)
    den = lr * lr + li * li
    nr, ni = lbr - 1.0, lbi
    coef_r, coef_i = (nr * lr + ni * li) / den, (ni * lr - nr * li) / den
    bbr, bbi = _cmul(coef_r, coef_i, btr_ref[0], bti_ref[0])

    def powers(rate, freq, n):
        m = jnp.exp(rate * n)
        return m * jnp.cos(freq * n), m * jnp.sin(freq * n)

    rate_c, freq_c = ar_col_ref[0] * dt, ai_col_ref[0] * dt
    rate_r, freq_r = lr * dt, li * dt
    tau = lax.broadcasted_iota(jnp.int32, (1, t), 1).astype(F32)
    pr0, pi0 = powers(rate_c, freq_c, tau)
    pr1, pi1 = powers(rate_c, freq_c, tau + 1.0)
    back = (t - 1) - lax.broadcasted_iota(jnp.int32, (t, 1), 0)
    qr, qi = powers(rate_r, freq_r, back.astype(F32))

    cr, ci = cr_ref[0], ci_ref[0]
    rows = []
    for c in range(nc):
        gr, gi = _cmul(cr[c:c + 1], ci[c:c + 1], bbr, bbi)
        rows.append(jnp.concatenate([gr, -gi], axis=1))
    k2_scr[...] = jnp.dot(jnp.concatenate(rows, axis=0), jnp.concatenate([pr0, pi0], axis=0),
                          precision=HIGHEST, preferred_element_type=F32)

    causal = (lax.broadcasted_iota(jnp.int32, (t, t), 1) >= lax.broadcasted_iota(jnp.int32, (t, t), 0))

    width = tm_scr.shape[2] // t
    n_blocks = nc // width

    def build_block(q):
        for e in range(width):
            c = q * width + e
            for cp in range(nc):
                krow = jnp.broadcast_to(k2_scr[pl.ds(c * nc + cp, 1), :], (t, t))
                tile = pltpu.roll(krow, 0, 1, stride=1, stride_axis=0)
                tm_scr[q, cp * t:(cp + 1) * t, e * t:(e + 1) * t] = jnp.where(causal, tile, 0.0).astype(BF16)

    a = jnp.concatenate([ut_ref[c] for c in range(nc)], axis=1).astype(BF16)

    for c in range(nc):
        wr, wi = _cmul(qr, qi, bbr[c:c + 1], bbi[c:c + 1])
        w_scr[c * t:(c + 1) * t, :] = jnp.concatenate([wr, wi], axis=1).astype(BF16)
    s = jnp.dot(a, w_scr[...], preferred_element_type=F32)

    r = s.shape[0]
    kidx = lax.broadcasted_iota(jnp.int32, (r, 1), 0) % chunks_per_batch

    def times_lam_pow(z, n):
        zr, zi = powers(rate_r, freq_r, float(n))
        return (jnp.concatenate([zr, zr], axis=1) * z
                + jnp.concatenate([-zi, zi], axis=1) * pltpu.roll(z, p, 1))

    step = 1
    while step < chunks_per_batch:
        prev = pltpu.roll(s, step, 0)
        s = s + jnp.where(kidx >= step, times_lam_pow(prev, t * step), 0.0)
        step *= 2
    e = jnp.where(kidx >= 1, pltpu.roll(s, 1, 0), 0.0)

    ctr, cti = ctr_ref[0], cti_ref[0]
    for c in range(nc):
        vr, vi = _cmul(ctr[:, c:c + 1], cti[:, c:c + 1], pr1, pi1)
        v_scr[:, c * t:(c + 1) * t] = jnp.concatenate([vr, -vi], axis=0).astype(BF16)
    y_inter = jnp.dot(e.astype(BF16), v_scr[...], preferred_element_type=F32)
    for q in range(n_blocks):
        yi_scr[q] = y_inter[:, q * width * t:(q + 1) * width * t]

    def emit_block(q):
        y = jnp.dot(a, tm_scr[q], preferred_element_type=F32) + yi_scr[q]
        for e in range(width):
            c = q * width + e
            y_ref[c] = y[:, e * t:(e + 1) * t] + d_ref[c] * ut_ref[c]

    build_block(0)

    def trip(q, carry):
        build_block(q + 1)
        emit_block(q)
        return carry

    lax.fori_loop(0, n_blocks - 1, trip, 0)
    emit_block(n_blocks - 1)


def _s5(ut3, seq, log_dt, a_re, a_im, b_re, b_im, c_re, c_im, d_skip):
    d_ssm, r, t = ut3.shape
    assert t == SSM_CHUNK
    n = r * t
    g, p = a_re.shape
    nc = b_re.shape[2]
    grp = lambda i: (i, 0, 0)
    row_spec = pl.BlockSpec((1, 1, p), grp)
    col_spec = pl.BlockSpec((1, p, 1), grp)
    cp_spec = pl.BlockSpec((1, nc, p), grp)
    pc_spec = pl.BlockSpec((1, p, nc), grp)
    return pl.pallas_call(
        functools.partial(_s5_kernel, seq // t),
        grid=(g,),
        in_specs=[pl.BlockSpec((nc, r, t), grp),
                  pl.BlockSpec((1, 1, 1), grp),
                  row_spec, row_spec, col_spec, col_spec,
                  cp_spec, cp_spec, cp_spec, cp_spec, pc_spec, pc_spec,
                  pl.BlockSpec((nc, 1, 1), grp)],
        out_specs=pl.BlockSpec((nc, r, t), grp),
        out_shape=jax.ShapeDtypeStruct((d_ssm, r, t), F32),
        scratch_shapes=[pltpu.VMEM((nc * nc, t), F32),
                        pltpu.VMEM((nc // TOEPLITZ_BLOCK, nc * t, TOEPLITZ_BLOCK * t), BF16),
                        pltpu.VMEM((nc * t, 2 * p), BF16),
                        pltpu.VMEM((2 * p, nc * t), BF16),
                        pltpu.VMEM((nc // TOEPLITZ_BLOCK, r, TOEPLITZ_BLOCK * t), F32)],
        compiler_params=_params("parallel"),
        name="s5",
    )(ut3, log_dt.reshape(g, 1, 1),
      a_re.reshape(g, 1, p), a_im.reshape(g, 1, p), a_re.reshape(g, p, 1), a_im.reshape(g, p, 1),
      jnp.swapaxes(b_re, 1, 2), jnp.swapaxes(b_im, 1, 2), c_re, c_im,
      jnp.swapaxes(c_re, 1, 2), jnp.swapaxes(c_im, 1, 2), d_skip.reshape(d_ssm, 1, 1))


def _dsa_kernel(topk, seq, tq, kc, dh, bisect_passes,
                qt_ref, qit_ref, wit_ref, k_ref, ksq_ref, ki_ref, vt_ref, o_ref,
                i_scr, m_scr, l_scr, acc_scr):
    i = pl.program_id(1)
    n_heads = acc_scr.shape[0]
    n_chunks = (i * tq + tq + kc - 1) // kc

    def head_rows(ref, h):
        return ref[h * LANES:(h + 1) * LANES, :]

    qpos = i * tq + lax.broadcasted_iota(jnp.int32, (1, tq), 1)
    krow = lax.broadcasted_iota(jnp.int32, (kc, 1), 0)
    wi = wit_ref[...]
    qi_all = jnp.concatenate([head_rows(qit_ref, h) for h in range(IDX_HEADS)], axis=1)

    def chunk_loop(body, carry):
        done = 0
        for width, shift in ((4, 2), (2, 1), (1, 0)):
            trips = (n_chunks - done) >> shift
            carry = lax.fori_loop(
                0, trips, lambda j, cr, w=width, d=done: body(tuple(d + w * j + u for u in range(w)), cr), carry)
            done = done + trips * width
        return carry

    def index_chunks(chunks, carry):
        lo, hi = carry
        starts = [pl.multiple_of(c * kc, kc) for c in chunks]
        dots = [jnp.dot(ki_ref[pl.ds(r0, kc), :], qi_all, preferred_element_type=F32) for r0 in starts]
        for c, r0, s_all in zip(chunks, starts, dots):
            score = jnp.zeros((kc, tq), F32)
            for h in range(IDX_HEADS):
                score = score + wi[h:h + 1, :] * jnp.maximum(s_all[:, h * tq:(h + 1) * tq], 0.0)
            i_scr[c] = jnp.where(r0 + krow <= qpos, score, -jnp.inf)
            lo = jnp.minimum(lo, jnp.min(score, axis=0, keepdims=True))
            hi = jnp.maximum(hi, jnp.max(score, axis=0, keepdims=True))
        return lo, hi

    lo, hi = chunk_loop(index_chunks, (jnp.full((1, tq), jnp.inf, F32), jnp.full((1, tq), -jnp.inf, F32)))

    def over_chunks(per_chunk, combine, init):
        def body(c, acc):
            part = per_chunk(i_scr[c]).reshape(kc // PART_ROWS, PART_ROWS, tq)
            return combine(acc, part)
        return lax.fori_loop(0, n_chunks, body, jnp.full((PART_ROWS, tq), init, F32))

    def count(pred):
        parts = over_chunks(lambda x: jnp.where(pred(x), 1.0, 0.0),
                            lambda acc, part: acc + jnp.sum(part, axis=0), 0.0)
        return jnp.sum(parts, axis=0, keepdims=True)

    def max_below(bound):
        parts = over_chunks(lambda x: jnp.where(x < bound, x, -jnp.inf),
                            lambda acc, part: jnp.maximum(acc, jnp.max(part, axis=0)), -jnp.inf)
        return jnp.max(parts, axis=0, keepdims=True)

    kf = float(topk)
    n_causal = (qpos + 1).astype(F32)
    hi = hi + jnp.abs(hi) * 2.0 ** -20 + 1e-30

    def bisect(_, st):
        lo, hi, c_lo, active = st
        mid = 0.5 * lo + 0.5 * hi
        c = count(lambda x: x >= mid)
        upd = jnp.logical_and(active > 0.0, jnp.logical_and(mid > lo, mid < hi))
        up_lo = jnp.logical_and(upd, c >= kf)
        up_hi = jnp.logical_and(upd, c < kf)
        lo = jnp.where(up_lo, mid, lo)
        c_lo = jnp.where(up_lo, c, c_lo)
        hi = jnp.where(up_hi, mid, hi)
        return lo, hi, c_lo, jnp.where(c_lo > kf, active, 0.0)

    active0 = jnp.where(n_causal > kf, 1.0, 0.0)
    lo, hi, c_lo, active = lax.fori_loop(0, bisect_passes, bisect, (lo, hi, n_causal, active0))

    def stepping(st):
        return jnp.max(st[3]) > 0.0

    def step_down(st):
        lo, hi, c_lo, active = st
        v = max_below(hi)
        c = count(lambda x: x >= v)
        on = active > 0.0
        found = jnp.logical_and(on, c >= kf)
        lo = jnp.where(found, v, lo)
        c_lo = jnp.where(found, c, c_lo)
        hi = jnp.where(jnp.logical_and(on, c < kf), v, hi)
        return lo, hi, c_lo, jnp.where(found, 0.0, active)

    thr, _, c_thr, _ = lax.while_loop(stepping, step_down, (lo, hi, c_lo, active))
    tied = jnp.max(jnp.where(c_thr > kf, 1.0, 0.0)) > 0.0

    qsq = jnp.zeros((1, tq), F32)
    for h in range(n_heads):
        qh = head_rows(qt_ref, h).astype(F32)
        qsq = jnp.maximum(qsq, jnp.sum(qh * qh, axis=0, keepdims=True))
    ksq = jnp.max(jnp.max(ksq_ref[...], axis=0, keepdims=True), axis=1, keepdims=True)
    shift = -1.02 * jnp.sqrt(qsq * ksq)

    @pl.when(jnp.logical_not(tied))
    def _():
        def mask_chunk(c, carry):
            i_scr[c] = jnp.where(i_scr[c] >= thr, shift, NEG_BIG)
            return carry
        lax.fori_loop(0, n_chunks, mask_chunk, 0)

    @pl.when(tied)
    def _():
        above = count(lambda x: x > thr)
        room = jnp.where(c_thr > kf, kf - above, float(seq))
        tri = (lax.broadcasted_iota(jnp.int32, (kc, kc), 0)
               >= lax.broadcasted_iota(jnp.int32, (kc, kc), 1)).astype(BF16)

        def mask_chunk(c, seen):
            x = i_scr[c]
            eq = jnp.where(x == thr, 1.0, 0.0)
            rank = jnp.dot(tri, eq.astype(BF16), preferred_element_type=F32) + seen
            keep = jnp.logical_or(x > thr, jnp.logical_and(x == thr, rank <= room))
            i_scr[c] = jnp.where(keep, shift, NEG_BIG)
            return seen + jnp.sum(eq, axis=0, keepdims=True)
        lax.fori_loop(0, n_chunks, mask_chunk, jnp.zeros((1, tq), F32))

    def logits(c, h):
        r0 = pl.multiple_of(c * kc, kc)
        kk = k_ref[pl.ds(r0, kc), (h // 2) * LANES:(h // 2 + 1) * LANES]
        return jnp.dot(kk, head_rows(qt_ref, h), preferred_element_type=F32) + i_scr[c]

    acc_scr[...] = jnp.zeros(acc_scr.shape, F32)

    def attend_chunks(chunks, carry):
        ahead = 4
        items = [(c, h) for c in chunks for h in range(n_heads)]
        probs = [jnp.exp2(logits(c, h)).astype(BF16) for c, h in items[:ahead]]
        for n, (c, h) in enumerate(items):
            if n + ahead < len(items):
                probs.append(jnp.exp2(logits(*items[n + ahead])).astype(BF16))
            acc_scr[h] += jnp.dot(vt_ref[c, h], probs[n], preferred_element_type=F32)
        return carry

    chunk_loop(attend_chunks, 0)

    l_min = jnp.full((1, tq), jnp.inf, F32)
    for h in range(n_heads):
        denom = acc_scr[h, dh:dh + 1, :]
        l_min = jnp.minimum(l_min, denom)
        o_ref[h * dh:(h + 1) * dh, :] = acc_scr[h, 0:dh, :] / denom

    @pl.when(jnp.min(l_min) < 2.0 ** -64)
    def _():
        m_scr[...] = jnp.full(m_scr.shape, NEG_BIG, F32)
        l_scr[...] = jnp.zeros(l_scr.shape, F32)
        acc_scr[...] = jnp.zeros(acc_scr.shape, F32)

        def attend_chunk_online(c, carry):
            for h in range(n_heads):
                s = logits(c, h)
                m_old = m_scr[h:h + 1, :]
                m_new = jnp.maximum(m_old, jnp.max(s, axis=0, keepdims=True))
                alpha = jnp.exp2(m_old - m_new)
                p = jnp.exp2(s - m_new)
                l_scr[h:h + 1, :] = alpha * l_scr[h:h + 1, :] + jnp.sum(p, axis=0, keepdims=True)
                m_scr[h:h + 1, :] = m_new
                pv = jnp.dot(vt_ref[c, h, 0:dh, :], p.astype(BF16), preferred_element_type=F32)
                acc_scr[h, 0:dh, :] = alpha * acc_scr[h, 0:dh, :] + pv
            return carry

        lax.fori_loop(0, n_chunks, attend_chunk_online, 0)
        for h in range(n_heads):
            o_ref[h * dh:(h + 1) * dh, :] = acc_scr[h, 0:dh, :] / l_scr[h:h + 1, :]


def _dsa(qt, qit, wit, k, ksq, ki, vt4, seq, tq, dh):
    n, d_attn = k.shape
    bsz = n // seq
    kc = vt4.shape[3]
    n_heads = d_attn // dh
    n_qi = IDX_HEADS * IDX_DIM
    assert 2 * dh == LANES and 2 * IDX_DIM == LANES and n_heads % 2 == 0 and IDX_HEADS % 2 == 0
    assert kc % PART_ROWS == 0
    topk = min(TOPK_MAX, seq // 4)
    blocks_per_batch = seq // tq
    chunks_per_batch = seq // kc
    bisect_passes = 14
    qcol = lambda b, i: (0, b * blocks_per_batch + i)
    per_batch = lambda b, i: (b, 0)
    return pl.pallas_call(
        functools.partial(_dsa_kernel, topk, seq, tq, kc, dh, bisect_passes),
        grid=(bsz, blocks_per_batch),
        in_specs=[pl.BlockSpec((2 * d_attn, tq), qcol),
                  pl.BlockSpec((2 * n_qi, tq), qcol),
                  pl.BlockSpec((IDX_HEADS, tq), qcol),
                  pl.BlockSpec((seq, d_attn), per_batch),
                  pl.BlockSpec((ksq.shape[0] // bsz, LANES), per_batch),
                  pl.BlockSpec((seq, 2 * IDX_DIM), per_batch),
                  pl.BlockSpec((chunks_per_batch, n_heads, dh + BF16_SUBLANES, kc), lambda b, i: (b, 0, 0, 0))],
        out_specs=pl.BlockSpec((d_attn, tq), qcol),
        out_shape=jax.ShapeDtypeStruct((d_attn, n), F32),
        scratch_shapes=[pltpu.VMEM((chunks_per_batch, kc, tq), F32),
                        pltpu.VMEM((n_heads, tq), F32),
                        pltpu.VMEM((n_heads, tq), F32),
                        pltpu.VMEM((n_heads, dh + BF16_SUBLANES, tq), F32)],
        compiler_params=_params("parallel", "arbitrary"),
        name="dsa",
    )(qt, qit, wit, k, ksq, ki, vt4)


def _mix_out_kernel(yt_ref, yat_ref, x_ref, g1_ref, sh_ref, sc_ref, gs_ref, ga_ref, wglu_ref, bglu_ref,
                    wo_ref, gf_ref, h_ref, hn_ref):
    def norm_cols(v, g):
        return v * lax.rsqrt(jnp.mean(v * v, axis=0, keepdims=True) + EPS) * g

    d_ssm, chunks, steps = yt_ref.shape
    y = _gelu(yt_ref[...].reshape(d_ssm, chunks * steps))
    z = jnp.dot(wglu_ref[...], y.astype(BF16), preferred_element_type=F32) + bglu_ref[...]
    y = y * jax.nn.sigmoid(z)
    mix_t = jnp.concatenate([norm_cols(y, gs_ref[...]), norm_cols(yat_ref[...], ga_ref[...])], axis=0)
    mix = jnp.dot(mix_t.T.astype(BF16), wo_ref[...], preferred_element_type=F32)
    h = x_ref[...] + g1_ref[0] * mix
    h_ref[...] = h
    hn_ref[...] = (_rmsnorm_rows(h, gf_ref[...]) * (1.0 + sc_ref[0]) + sh_ref[0]).astype(BF16)


def _mix_out(yt3, yat, x2, g1, sh2, sc2, gs, ga, w_glu, b_glu, w_out, gf, seq, tm):
    d_ssm, _, steps = yt3.shape
    n, d = x2.shape
    d_attn = yat.shape[0]
    bsz = n // seq
    tiles_per_batch = seq // tm
    row = lambda i: (i, 0)
    col = lambda i: (0, i)
    const = lambda i: (0, 0)
    per_batch = lambda i: (i // tiles_per_batch, 0, 0)
    mod_spec = pl.BlockSpec((1, 1, d), per_batch)
    return pl.pallas_call(
        _mix_out_kernel,
        grid=(n // tm,),
        in_specs=[pl.BlockSpec((d_ssm, tm // steps, steps), lambda i: (0, i, 0)),
                  pl.BlockSpec((d_attn, tm), col),
                  pl.BlockSpec((tm, d), row),
                  mod_spec, mod_spec, mod_spec,
                  pl.BlockSpec((d_ssm, 1), const),
                  pl.BlockSpec((d_attn, 1), const),
                  pl.BlockSpec((d_ssm, d_ssm), const),
                  pl.BlockSpec((d_ssm, 1), const),
                  pl.BlockSpec((d_ssm + d_attn, d), const),
                  pl.BlockSpec((1, d), const)],
        out_specs=[pl.BlockSpec((tm, d), row), pl.BlockSpec((tm, d), row)],
        out_shape=[jax.ShapeDtypeStruct((n, d), F32), jax.ShapeDtypeStruct((n, d), BF16)],
        compiler_params=_params("parallel"),
        name="mix_out",
    )(yt3, yat, x2, g1.reshape(bsz, 1, d), sh2.reshape(bsz, 1, d), sc2.reshape(bsz, 1, d),
      gs.reshape(d_ssm, 1), ga.reshape(d_attn, 1), w_glu.T.astype(BF16), b_glu.reshape(d_ssm, 1),
      w_out.astype(BF16), gf.reshape(1, d))


def _ffn_kernel(tiles_per_batch, tf,
                hn_ref, halo_ref, h_ref, g2_ref, wu_ref, cw_ref, cb_ref, wd_ref,
                o_ref, lhs_scr, act_scr):
    i = pl.program_id(0)
    halo_rows = halo_ref.shape[0]
    lhs_scr[0:halo_rows, :] = jnp.where(i % tiles_per_batch == 0, 0, halo_ref[...]).astype(BF16)
    lhs_scr[halo_rows:, :] = hn_ref[...]

    d_ff = act_scr.shape[1]

    def conv_up(c0):
        up = jnp.dot(lhs_scr[...], wu_ref[:, c0:c0 + tf], preferred_element_type=F32)
        conv = cb_ref[:, c0:c0 + tf] + cw_ref[CONV_W - 1:CONV_W, c0:c0 + tf] * up
        for back in range(1, CONV_W):
            tap = CONV_W - 1 - back
            conv = conv + cw_ref[tap:tap + 1, c0:c0 + tf] * pltpu.roll(up, back, 0)
        return conv[halo_rows:]

    for c0 in range(0, d_ff, tf):
        act_scr[:, c0:c0 + tf] = (_gelu(conv_up(c0)) * conv_up(d_ff + c0)).astype(BF16)

    down = jnp.dot(act_scr[...], wd_ref[...], preferred_element_type=F32)
    o_ref[...] = h_ref[...] + g2_ref[0] * down


def _ffn(hn2, h1, g2, w_up, conv_w, conv_b, w_down, seq, tm, tf):
    n, d = h1.shape
    d_ff = w_down.shape[0]
    bsz = n // seq
    tiles_per_batch = seq // tm
    halo_rows = BF16_SUBLANES
    assert halo_rows >= CONV_W - 1 and tm % halo_rows == 0 and d_ff % tf == 0
    halo_per_tile = tm // halo_rows
    row = lambda i: (i, 0)
    const = lambda i: (0, 0)
    resident = pl.Buffered(1)

    return pl.pallas_call(
        functools.partial(_ffn_kernel, tiles_per_batch, tf),
        grid=(n // tm,),
        in_specs=[pl.BlockSpec((tm, d), row),
                  pl.BlockSpec((halo_rows, d), lambda i: (jnp.maximum(i * halo_per_tile - 1, 0), 0)),
                  pl.BlockSpec((tm, d), row),
                  pl.BlockSpec((1, 1, d), lambda i: (i // tiles_per_batch, 0, 0)),
                  pl.BlockSpec((d, 2 * d_ff), const, pipeline_mode=resident),
                  pl.BlockSpec((CONV_W, 2 * d_ff), const, pipeline_mode=resident),
                  pl.BlockSpec((1, 2 * d_ff), const, pipeline_mode=resident),
                  pl.BlockSpec((d_ff, d), const, pipeline_mode=resident)],
        out_specs=pl.BlockSpec((tm, d), row),
        out_shape=jax.ShapeDtypeStruct((n, d), F32),
        scratch_shapes=[pltpu.VMEM((halo_rows + tm, d), BF16),
                        pltpu.VMEM((tm, d_ff), BF16)],
        compiler_params=_params("parallel"),
        name="ffn",
    )(hn2, hn2, h1, g2.reshape(bsz, 1, d), w_up.astype(BF16), conv_w, conv_b.reshape(1, 2 * d_ff),
      w_down.astype(BF16))


def _tile(total, want):
    t = min(total, want)
    assert total % t == 0
    return t


def kernel(x, c, positions, w_ada, b_ada, norm_mix, w_in, ssm_log_dt, ssm_a_re, ssm_a_im, ssm_b_re, ssm_b_im, ssm_c_re, ssm_c_im, ssm_d, ssm_w_glu, ssm_b_glu, q_norm, k_norm, out_norm_ssm, out_norm_attn, w_out, norm_ffn, w_up, conv_w, conv_b, w_down):
    bsz, seq, d = x.shape
    depth = w_ada.shape[0]
    n = bsz * seq
    d_ssm = out_norm_ssm.shape[1]
    d_attn = out_norm_attn.shape[1]
    dh = q_norm.shape[1]
    d_ff = w_down.shape[1]
    assert w_in.shape[2] == d_ssm + 3 * d_attn + IDX_HEADS * IDX_DIM + IDX_DIM + IDX_HEADS
    assert seq % SSM_CHUNK == 0

    t_proj = _tile(seq, 1024)
    t_key = _tile(seq, 512)
    t_q = _tile(seq, 256)
    t_mix = _tile(seq, 1024)
    t_ffn = _tile(seq, 512)
    t_ff = _tile(d_ff, 256)

    h = x.astype(F32).reshape(n, d)
    pos_row = positions.reshape(1, n)
    for l in range(depth):
        mod = _adaln(c.astype(F32), w_ada[l].astype(F32), b_ada[l].astype(F32))
        sh1, sc1, g1, sh2, sc2, g2 = jnp.split(mod, 6, axis=-1)
        ut, qt, k, ksq, vt4, qit, ki, wit = _in_proj(h, norm_mix[l], sh1, sc1, w_in[l], pos_row, q_norm[l],
                                                     k_norm[l], seq, t_proj, t_key, d_ssm, d_attn, dh)
        yt = _s5(ut, seq, ssm_log_dt[l], ssm_a_re[l], ssm_a_im[l], ssm_b_re[l], ssm_b_im[l],
                 ssm_c_re[l], ssm_c_im[l], ssm_d[l])
        yat = _dsa(qt, qit, wit, k, ksq, ki, vt4, seq, t_q, dh)
        h1, hn2 = _mix_out(yt, yat, h, g1, sh2, sc2, out_norm_ssm[l], out_norm_attn[l], ssm_w_glu[l],
                           ssm_b_glu[l], w_out[l], norm_ffn[l], seq, t_mix)
        h = _ffn(hn2, h1, g2, w_up[l], conv_w[l], conv_b[l], w_down[l], seq, t_ffn, t_ff)
    return h.reshape(bsz, seq, d).astype(x.dtype)
```

```python
import functools
import math

import jax
import jax.numpy as jnp
from jax import lax
from jax.experimental import pallas as pl
from jax.experimental.pallas import tpu as pltpu

EPS = 1e-6
ROPE_THETA = 10000.0
IDX_HEADS = 8
IDX_DIM = 64
TOPK_MAX = 256
CONV_W = 3

LANES = 128
BF16_SUBLANES = 16
SSM_CHUNK = LANES
TOEPLITZ_BLOCK = 2
PART_ROWS = 32
VMEM_LIMIT = 56 * 1024 * 1024
NEG_BIG = -1e30
LOG2E = math.log2(math.e)

F32 = jnp.float32
BF16 = jnp.bfloat16
HIGHEST = lax.Precision.HIGHEST


def _params(*semantics):
    return pltpu.CompilerParams(dimension_semantics=semantics, vmem_limit_bytes=VMEM_LIMIT)


def _gelu(x):
    return 0.5 * x * (1.0 + jnp.tanh(math.sqrt(2.0 / math.pi) * (x + 0.044715 * (x * x * x))))


def _adaln_kernel(ct_ref, w_ref, b_ref, o_ref):
    ct = ct_ref[...]
    cs = ct * jax.nn.sigmoid(ct)
    w = w_ref[...]
    for b in range(ct.shape[1]):
        o_ref[b:b + 1, :] = jnp.sum(w * cs[:, b:b + 1], axis=0, keepdims=True) + b_ref[...]


def _adaln(c, w, bias):
    bsz, d = c.shape
    n = w.shape[1]
    tn = 1024 if n % 1024 == 0 else n
    return pl.pallas_call(
        _adaln_kernel,
        grid=(n // tn,),
        in_specs=[pl.BlockSpec((d, bsz), lambda j: (0, 0)),
                  pl.BlockSpec((d, tn), lambda j: (0, j)),
                  pl.BlockSpec((1, tn), lambda j: (0, j))],
        out_specs=pl.BlockSpec((bsz, tn), lambda j: (0, j)),
        out_shape=jax.ShapeDtypeStruct((bsz, n), F32),
        compiler_params=_params("parallel"),
        name="adaln",
    )(c.T, w, bias.reshape(1, n))


def _rmsnorm_rows(x, g):
    return x * lax.rsqrt(jnp.mean(x * x, axis=-1, keepdims=True) + EPS) * g


def _rope_t(x3, cos, sin):
    half = x3.shape[1] // 2
    x1, x2 = x3[:, :half, :], x3[:, half:, :]
    return jnp.concatenate([x1 * cos - x2 * sin, x1 * sin + x2 * cos], axis=1)


def _pair_rows(x3):
    heads, dh, t = x3.shape
    zero = jnp.zeros_like(x3)
    even = lax.broadcasted_iota(jnp.int32, (heads, 1, 1), 0) % 2 == 0
    return jnp.concatenate([jnp.where(even, x3, zero), jnp.where(even, zero, x3)], axis=1).reshape(heads * 2 * dh, t)


def _in_proj_kernel(d_ssm, d_attn, dh, wi_scale,
                    x_ref, g_ref, sh_ref, sc_ref, w_ref, pos_ref, inv_ref, qg_ref, kg_ref,
                    ut_ref, qt_ref, k_ref, ksq_ref, vt_ref, qit_ref, ki_ref, wit_ref):
    hn = _rmsnorm_rows(x_ref[...], g_ref[...]) * (1.0 + sc_ref[0]) + sh_ref[0]
    hb = hn.astype(BF16)
    tm = hb.shape[0]

    def proj_t(r0, rows):
        return lax.dot_general(w_ref[:, r0:r0 + rows], hb, (((0,), (1,)), ((), ())),
                               preferred_element_type=F32)

    ang = inv_ref[...] * pos_ref[...].astype(F32)
    cos, sin = jnp.cos(ang)[None], jnp.sin(ang)[None]

    def head_norm(x3, gcol):
        return x3 * lax.rsqrt(jnp.mean(x3 * x3, axis=1, keepdims=True) + EPS) * gcol[None]

    ut_ref[...] = proj_t(0, d_ssm).reshape(d_ssm, tm // LANES, LANES)
    o = d_ssm
    n_heads = d_attn // dh
    q3 = proj_t(o, d_attn).reshape(n_heads, dh, tm)
    q3 = _rope_t(head_norm(q3, qg_ref[...]), cos, sin) * (dh ** -0.5 * LOG2E)
    qt_ref[...] = _pair_rows(q3).astype(BF16)
    o += d_attn
    k3 = _rope_t(head_norm(proj_t(o, d_attn).reshape(n_heads, dh, tm), kg_ref[...]), cos, sin)
    k_ref[...] = k3.reshape(d_attn, tm).T.astype(BF16)
    ksq = jnp.max(jnp.sum(k3 * k3, axis=1), axis=1, keepdims=True)
    ksq_ref[...] = jnp.broadcast_to(ksq, ksq_ref.shape)
    o += d_attn
    v3 = proj_t(o, d_attn).reshape(n_heads, dh, tm)
    vt = jnp.concatenate([v3, jnp.ones((n_heads, BF16_SUBLANES, tm), F32)], axis=1).astype(BF16)
    kc = vt_ref.shape[3]
    for j in range(tm // kc):
        vt_ref[j] = vt[:, :, j * kc:(j + 1) * kc]
    o += d_attn
    n_qi = IDX_HEADS * IDX_DIM
    qi3 = _rope_t(proj_t(o, n_qi).reshape(IDX_HEADS, IDX_DIM, tm), cos, sin)
    qit_ref[...] = _pair_rows(qi3).astype(BF16)
    o += n_qi
    tail = proj_t(o, IDX_DIM + BF16_SUBLANES)
    ki = _rope_t(tail[None, :IDX_DIM], cos, sin)[0]
    ki_ref[...] = jnp.concatenate([ki, ki], axis=0).T.astype(BF16)
    wit_ref[...] = tail[IDX_DIM:IDX_DIM + IDX_HEADS] * wi_scale


def _in_proj(x2, g, sh, sc, w_in, pos_row, q_norm, k_norm, seq, tm, kc, d_ssm, d_attn, dh):
    n, d = x2.shape
    bsz = n // seq
    tiles_per_batch = seq // tm
    n_qi = IDX_HEADS * IDX_DIM
    n_heads = d_attn // dh
    w = jnp.concatenate([w_in.astype(BF16), jnp.zeros((d, BF16_SUBLANES - IDX_HEADS), BF16)], axis=1)
    rt = w.shape[1]
    inv = (ROPE_THETA ** (-jnp.arange(0, dh, 2, dtype=F32) / dh)).reshape(dh // 2, 1)
    wi_scale = IDX_HEADS ** -0.5 * IDX_DIM ** -0.5

    row = lambda i: (i, 0)
    col = lambda i: (0, i)
    const = lambda i: (0, 0)
    per_batch = lambda i: (i // tiles_per_batch, 0, 0)
    return pl.pallas_call(
        functools.partial(_in_proj_kernel, d_ssm, d_attn, dh, wi_scale),
        grid=(n // tm,),
        in_specs=[pl.BlockSpec((tm, d), row),
                  pl.BlockSpec((1, d), const),
                  pl.BlockSpec((1, 1, d), per_batch),
                  pl.BlockSpec((1, 1, d), per_batch),
                  pl.BlockSpec((d, rt), const),
                  pl.BlockSpec((1, tm), col),
                  pl.BlockSpec((dh // 2, 1), const),
                  pl.BlockSpec((dh, 1), const),
                  pl.BlockSpec((dh, 1), const)],
        out_specs=[pl.BlockSpec((d_ssm, tm // LANES, LANES), lambda i: (0, i, 0)),
                   pl.BlockSpec((2 * d_attn, tm), col),
                   pl.BlockSpec((tm, d_attn), row),
                   pl.BlockSpec((n_heads, LANES), row),
                   pl.BlockSpec((tm // kc, n_heads, dh + BF16_SUBLANES, kc), lambda i: (i, 0, 0, 0)),
                   pl.BlockSpec((2 * n_qi, tm), col),
                   pl.BlockSpec((tm, 2 * IDX_DIM), row),
                   pl.BlockSpec((IDX_HEADS, tm), col)],
        out_shape=[jax.ShapeDtypeStruct((d_ssm, n // LANES, LANES), F32),
                   jax.ShapeDtypeStruct((2 * d_attn, n), BF16),
                   jax.ShapeDtypeStruct((n, d_attn), BF16),
                   jax.ShapeDtypeStruct((n // tm * n_heads, LANES), F32),
                   jax.ShapeDtypeStruct((n // kc, n_heads, dh + BF16_SUBLANES, kc), BF16),
                   jax.ShapeDtypeStruct((2 * n_qi, n), BF16),
                   jax.ShapeDtypeStruct((n, 2 * IDX_DIM), BF16),
                   jax.ShapeDtypeStruct((IDX_HEADS, n), F32)],
        compiler_params=_params("parallel"),
        name="in_proj",
    )(x2, g.reshape(1, d), sh.reshape(bsz, 1, d), sc.reshape(bsz, 1, d), w, pos_row, inv,
      q_norm.reshape(dh, 1), k_norm.reshape(dh, 1))


def _cmul(ar, ai, br, bi):
    return ar * br - ai * bi, ar * bi + ai * br


def _s5_kernel(chunks_per_batch,
               ut_ref, ldt_ref, ar_row_ref, ai_row_ref, ar_col_ref, ai_col_ref,
               btr_ref, bti_ref, cr_ref, ci_ref, ctr_ref, cti_ref, d_ref,
               y_ref, k2_scr, tm_scr, w_scr, v_scr, yi_scr):
    nc, p, t = btr_ref.shape[1], btr_ref.shape[2], SSM_CHUNK
    dt = jnp.exp(ldt_ref[0])
    lr, li = ar_row_ref[0], ai_row_ref[0]
    mag = jnp.exp(lr * dt)
    lbr, lbi = mag * jnp.cos(li * dt), mag * jnp.sin(li * dt)
    den = lr * lr + li * li
    nr, ni = lbr - 1.0, lbi
    coef_r, coef_i = (nr * lr + ni * li) / den, (ni * lr - nr * li) / den
    bbr, bbi = _cmul(coef_r, coef_i, btr_ref[0], bti_ref[0])

    def powers(rate, freq, n):
        m = jnp.exp(rate * n)
        return m * jnp.cos(freq * n), m * jnp.sin(freq * n)

    rate_c, freq_c = ar_col_ref[0] * dt, ai_col_ref[0] * dt
    rate_r, freq_r = lr * dt, li * dt
    tau = lax.broadcasted_iota(jnp.int32, (1, t), 1).astype(F32)
    pr0, pi0 = powers(rate_c, freq_c, tau)
    pr1, pi1 = powers(rate_c, freq_c, tau + 1.0)
    back = (t - 1) - lax.broadcasted_iota(jnp.int32, (t, 1), 0)
    qr, qi = powers(rate_r, freq_r, back.astype(F32))

    cr, ci = cr_ref[0], ci_ref[0]
    rows = []
    for c in range(nc):
        gr, gi = _cmul(cr[c:c + 1], ci[c:c + 1], bbr, bbi)
        rows.append(jnp.concatenate([gr, -gi], axis=1))
    k2_scr[...] = jnp.dot(jnp.concatenate(rows, axis=0), jnp.concatenate([pr0, pi0], axis=0),
                          precision=HIGHEST, preferred_element_type=F32)

    causal = (lax.broadcasted_iota(jnp.int32, (t, t), 1) >= lax.broadcasted_iota(jnp.int32, (t, t), 0))

    width = tm_scr.shape[2] // t
    n_blocks = nc // width

    def build_block(q):
        for e in range(width):
            c = q * width + e
            for cp in range(nc):
                krow = jnp.broadcast_to(k2_scr[pl.ds(c * nc + cp, 1), :], (t, t))
                tile = pltpu.roll(krow, 0, 1, stride=1, stride_axis=0)
                tm_scr[q, cp * t:(cp + 1) * t, e * t:(e + 1) * t] = jnp.where(causal, tile, 0.0).astype(BF16)

    a = jnp.concatenate([ut_ref[c] for c in range(nc)], axis=1).astype(BF16)

    for c in range(nc):
        wr, wi = _cmul(qr, qi, bbr[c:c + 1], bbi[c:c + 1])
        w_scr[c * t:(c + 1) * t, :] = jnp.concatenate([wr, wi], axis=1).astype(BF16)
    s = jnp.dot(a, w_scr[...], preferred_element_type=F32)

    r = s.shape[0]
    kidx = lax.broadcasted_iota(jnp.int32, (r, 1), 0) % chunks_per_batch

    def times_lam_pow(z, n):
        zr, zi = powers(rate_r, freq_r, float(n))
        return (jnp.concatenate([zr, zr], axis=1) * z
                + jnp.concatenate([-zi, zi], axis=1) * pltpu.roll(z, p, 1))

    step = 1
    while step < chunks_per_batch:
        prev = pltpu.roll(s, step, 0)
        s = s + jnp.where(kidx >= step, times_lam_pow(prev, t * step), 0.0)
        step *= 2
    e = jnp.where(kidx >= 1, pltpu.roll(s, 1, 0), 0.0)

    ctr, cti = ctr_ref[0], cti_ref[0]
    for c in range(nc):
        vr, vi = _cmul(ctr[:, c:c + 1], cti[:, c:c + 1], pr1, pi1)
        v_scr[:, c * t:(c + 1) * t] = jnp.concatenate([vr, -vi], axis=0).astype(BF16)
    y_inter = jnp.dot(e.astype(BF16), v_scr[...], preferred_element_type=F32)
    for q in range(n_blocks):
        yi_scr[q] = y_inter[:, q * width * t:(q + 1) * width * t]

    def emit_block(q):
        y = jnp.dot(a, tm_scr[q], preferred_element_type=F32) + yi_scr[q]
        for e in range(width):
            c = q * width + e
            y_ref[c] = y[:, e * t:(e + 1) * t] + d_ref[c] * ut_ref[c]

    build_block(0)

    def trip(q, carry):
        build_block(q + 1)
        emit_block(q)
        return carry

    lax.fori_loop(0, n_blocks - 1, trip, 0)
    emit_block(n_blocks - 1)


def _s5(ut3, seq, log_dt, a_re, a_im, b_re, b_im, c_re, c_im, d_skip):
    d_ssm, r, t = ut3.shape
    assert t == SSM_CHUNK
    n = r * t
    g, p = a_re.shape
    nc = b_re.shape[2]
    grp = lambda i: (i, 0, 0)
    row_spec = pl.BlockSpec((1, 1, p), grp)
    col_spec = pl.BlockSpec((1, p, 1), grp)
    cp_spec = pl.BlockSpec((1, nc, p), grp)
    pc_spec = pl.BlockSpec((1, p, nc), grp)
    return pl.pallas_call(
        functools.partial(_s5_kernel, seq // t),
        grid=(g,),
        in_specs=[pl.BlockSpec((nc, r, t), grp),
                  pl.BlockSpec((1, 1, 1), grp),
                  row_spec, row_spec, col_spec, col_spec,
                  cp_spec, cp_spec, cp_spec, cp_spec, pc_spec, pc_spec,
                  pl.BlockSpec((nc, 1, 1), grp)],
        out_specs=pl.BlockSpec((nc, r, t), grp),
        out_shape=jax.ShapeDtypeStruct((d_ssm, r, t), F32),
        scratch_shapes=[pltpu.VMEM((nc * nc, t), F32),
                        pltpu.VMEM((nc // TOEPLITZ_BLOCK, nc * t, TOEPLITZ_BLOCK * t), BF16),
                        pltpu.VMEM((nc * t, 2 * p), BF16),
                        pltpu.VMEM((2 * p, nc * t), BF16),
                        pltpu.VMEM((nc // TOEPLITZ_BLOCK, r, TOEPLITZ_BLOCK * t), F32)],
        compiler_params=_params("parallel"),
        name="s5",
    )(ut3, log_dt.reshape(g, 1, 1),
      a_re.reshape(g, 1, p), a_im.reshape(g, 1, p), a_re.reshape(g, p, 1), a_im.reshape(g, p, 1),
      jnp.swapaxes(b_re, 1, 2), jnp.swapaxes(b_im, 1, 2), c_re, c_im,
      jnp.swapaxes(c_re, 1, 2), jnp.swapaxes(c_im, 1, 2), d_skip.reshape(d_ssm, 1, 1))


def _dsa_kernel(topk, seq, tq, kc, dh, bisect_passes,
                qt_ref, qit_ref, wit_ref, k_ref, ksq_ref, ki_ref, vt_ref, o_ref,
                i_scr, m_scr, l_scr, acc_scr):
    i = pl.program_id(1)
    n_heads = acc_scr.shape[0]
    n_chunks = (i * tq + tq + kc - 1) // kc

    def head_rows(ref, h):
        return ref[h * LANES:(h + 1) * LANES, :]

    qpos = i * tq + lax.broadcasted_iota(jnp.int32, (1, tq), 1)
    krow = lax.broadcasted_iota(jnp.int32, (kc, 1), 0)
    wi = wit_ref[...]
    qi_all = jnp.concatenate([head_rows(qit_ref, h) for h in range(IDX_HEADS)], axis=1)

    def chunk_loop(body, carry):
        done = 0
        for width, shift in ((4, 2), (2, 1), (1, 0)):
            trips = (n_chunks - done) >> shift
            carry = lax.fori_loop(
                0, trips, lambda j, cr, w=width, d=done: body(tuple(d + w * j + u for u in range(w)), cr), carry)
            done = done + trips * width
        return carry

    def index_chunks(chunks, carry):
        lo, hi = carry
        starts = [pl.multiple_of(c * kc, kc) for c in chunks]
        dots = [jnp.dot(ki_ref[pl.ds(r0, kc), :], qi_all, preferred_element_type=F32) for r0 in starts]
        for c, r0, s_all in zip(chunks, starts, dots):
            score = jnp.zeros((kc, tq), F32)
            for h in range(IDX_HEADS):
                score = score + wi[h:h + 1, :] * jnp.maximum(s_all[:, h * tq:(h + 1) * tq], 0.0)
            i_scr[c] = jnp.where(r0 + krow <= qpos, score, -jnp.inf)
            lo = jnp.minimum(lo, jnp.min(score, axis=0, keepdims=True))
            hi = jnp.maximum(hi, jnp.max(score, axis=0, keepdims=True))
        return lo, hi

    lo, hi = chunk_loop(index_chunks, (jnp.full((1, tq), jnp.inf, F32), jnp.full((1, tq), -jnp.inf, F32)))

    def over_chunks(per_chunk, combine, init):
        def body(chunks, acc):
            for c in chunks:
                acc = combine(acc, per_chunk(i_scr[c]).reshape(kc // PART_ROWS, PART_ROWS, tq))
            return acc
        return chunk_loop(body, jnp.full((PART_ROWS, tq), init, F32))

    def count(pred):
        parts = over_chunks(lambda x: jnp.where(pred(x), 1.0, 0.0),
                            lambda acc, part: acc + jnp.sum(part, axis=0), 0.0)
        return jnp.sum(parts, axis=0, keepdims=True)

    def max_below(bound):
        parts = over_chunks(lambda x: jnp.where(x < bound, x, -jnp.inf),
                            lambda acc, part: jnp.maximum(acc, jnp.max(part, axis=0)), -jnp.inf)
        return jnp.max(parts, axis=0, keepdims=True)

    kf = float(topk)
    n_causal = (qpos + 1).astype(F32)
    hi = hi + jnp.abs(hi) * 2.0 ** -20 + 1e-30

    def bisect(_, st):
        lo, hi, c_lo, active = st
        mid = 0.5 * lo + 0.5 * hi
        c = count(lambda x: x >= mid)
        upd = jnp.logical_and(active > 0.0, jnp.logical_and(mid > lo, mid < hi))
        up_lo = jnp.logical_and(upd, c >= kf)
        up_hi = jnp.logical_and(upd, c < kf)
        lo = jnp.where(up_lo, mid, lo)
        c_lo = jnp.where(up_lo, c, c_lo)
        hi = jnp.where(up_hi, mid, hi)
        return lo, hi, c_lo, jnp.where(c_lo > kf, active, 0.0)

    active0 = jnp.where(n_causal > kf, 1.0, 0.0)
    lo, hi, c_lo, active = lax.fori_loop(0, bisect_passes, bisect, (lo, hi, n_causal, active0))

    def stepping(st):
        return jnp.max(st[3]) > 0.0

    def step_down(st):
        lo, hi, c_lo, active = st
        v = max_below(hi)
        c = count(lambda x: x >= v)
        on = active > 0.0
        found = jnp.logical_and(on, c >= kf)
        lo = jnp.where(found, v, lo)
        c_lo = jnp.where(found, c, c_lo)
        hi = jnp.where(jnp.logical_and(on, c < kf), v, hi)
        return lo, hi, c_lo, jnp.where(found, 0.0, active)

    thr, _, c_thr, _ = lax.while_loop(stepping, step_down, (lo, hi, c_lo, active))
    tied = jnp.max(jnp.where(c_thr > kf, 1.0, 0.0)) > 0.0

    qsq = jnp.zeros((1, tq), F32)
    for h in range(n_heads):
        qh = head_rows(qt_ref, h).astype(F32)
        qsq = jnp.maximum(qsq, jnp.sum(qh * qh, axis=0, keepdims=True))
    ksq = jnp.max(jnp.max(ksq_ref[...], axis=0, keepdims=True), axis=1, keepdims=True)
    shift = -1.02 * jnp.sqrt(qsq * ksq)

    @pl.when(jnp.logical_not(tied))
    def _():
        def mask_chunk(c, carry):
            i_scr[c] = jnp.where(i_scr[c] >= thr, shift, NEG_BIG)
            return carry
        lax.fori_loop(0, n_chunks, mask_chunk, 0)

    @pl.when(tied)
    def _():
        above = count(lambda x: x > thr)
        room = jnp.where(c_thr > kf, kf - above, float(seq))
        tri = (lax.broadcasted_iota(jnp.int32, (kc, kc), 0)
               >= lax.broadcasted_iota(jnp.int32, (kc, kc), 1)).astype(BF16)

        def mask_chunk(c, seen):
            x = i_scr[c]
            eq = jnp.where(x == thr, 1.0, 0.0)
            rank = jnp.dot(tri, eq.astype(BF16), preferred_element_type=F32) + seen
            keep = jnp.logical_or(x > thr, jnp.logical_and(x == thr, rank <= room))
            i_scr[c] = jnp.where(keep, shift, NEG_BIG)
            return seen + jnp.sum(eq, axis=0, keepdims=True)
        lax.fori_loop(0, n_chunks, mask_chunk, jnp.zeros((1, tq), F32))

    def logits(c, h):
        r0 = pl.multiple_of(c * kc, kc)
        kk = k_ref[pl.ds(r0, kc), (h // 2) * LANES:(h // 2 + 1) * LANES]
        return jnp.dot(kk, head_rows(qt_ref, h), preferred_element_type=F32) + i_scr[c]

    acc_scr[...] = jnp.zeros(acc_scr.shape, F32)

    def attend_chunks(chunks, carry):
        ahead = 4
        items = [(c, h) for c in chunks for h in range(n_heads)]
        probs = [jnp.exp2(logits(c, h)).astype(BF16) for c, h in items[:ahead]]
        for n, (c, h) in enumerate(items):
            if n + ahead < len(items):
                probs.append(jnp.exp2(logits(*items[n + ahead])).astype(BF16))
            acc_scr[h] += jnp.dot(vt_ref[c, h], probs[n], preferred_element_type=F32)
        return carry

    chunk_loop(attend_chunks, 0)

    l_min = jnp.full((1, tq), jnp.inf, F32)
    for h in range(n_heads):
        denom = acc_scr[h, dh:dh + 1, :]
        l_min = jnp.minimum(l_min, denom)
        o_ref[h * dh:(h + 1) * dh, :] = acc_scr[h, 0:dh, :] / denom

    @pl.when(jnp.min(l_min) < 2.0 ** -64)
    def _():
        m_scr[...] = jnp.full(m_scr.shape, NEG_BIG, F32)
        l_scr[...] = jnp.zeros(l_scr.shape, F32)
        acc_scr[...] = jnp.zeros(acc_scr.shape, F32)

        def attend_chunk_online(c, carry):
            for h in range(n_heads):
                s = logits(c, h)
                m_old = m_scr[h:h + 1, :]
                m_new = jnp.maximum(m_old, jnp.max(s, axis=0, keepdims=True))
                alpha = jnp.exp2(m_old - m_new)
                p = jnp.exp2(s - m_new)
                l_scr[h:h + 1, :] = alpha * l_scr[h:h + 1, :] + jnp.sum(p, axis=0, keepdims=True)
                m_scr[h:h + 1, :] = m_new
                pv = jnp.dot(vt_ref[c, h, 0:dh, :], p.astype(BF16), preferred_element_type=F32)
                acc_scr[h, 0:dh, :] = alpha * acc_scr[h, 0:dh, :] + pv
            return carry

        lax.fori_loop(0, n_chunks, attend_chunk_online, 0)
        for h in range(n_heads):
            o_ref[h * dh:(h + 1) * dh, :] = acc_scr[h, 0:dh, :] / l_scr[h:h + 1, :]


def _dsa(qt, qit, wit, k, ksq, ki, vt4, seq, tq, dh):
    n, d_attn = k.shape
    bsz = n // seq
    kc = vt4.shape[3]
    n_heads = d_attn // dh
    n_qi = IDX_HEADS * IDX_DIM
    assert 2 * dh == LANES and 2 * IDX_DIM == LANES and n_heads % 2 == 0 and IDX_HEADS % 2 == 0
    assert kc % PART_ROWS == 0
    topk = min(TOPK_MAX, seq // 4)
    blocks_per_batch = seq // tq
    chunks_per_batch = seq // kc
    bisect_passes = 14
    qcol = lambda b, i: (0, b * blocks_per_batch + i)
    per_batch = lambda b, i: (b, 0)
    return pl.pallas_call(
        functools.partial(_dsa_kernel, topk, seq, tq, kc, dh, bisect_passes),
        grid=(bsz, blocks_per_batch),
        in_specs=[pl.BlockSpec((2 * d_attn, tq), qcol),
                  pl.BlockSpec((2 * n_qi, tq), qcol),
                  pl.BlockSpec((IDX_HEADS, tq), qcol),
                  pl.BlockSpec((seq, d_attn), per_batch),
                  pl.BlockSpec((ksq.shape[0] // bsz, LANES), per_batch),
                  pl.BlockSpec((seq, 2 * IDX_DIM), per_batch),
                  pl.BlockSpec((chunks_per_batch, n_heads, dh + BF16_SUBLANES, kc), lambda b, i: (b, 0, 0, 0))],
        out_specs=pl.BlockSpec((d_attn, tq), qcol),
        out_shape=jax.ShapeDtypeStruct((d_attn, n), F32),
        scratch_shapes=[pltpu.VMEM((chunks_per_batch, kc, tq), F32),
                        pltpu.VMEM((n_heads, tq), F32),
                        pltpu.VMEM((n_heads, tq), F32),
                        pltpu.VMEM((n_heads, dh + BF16_SUBLANES, tq), F32)],
        compiler_params=_params("parallel", "arbitrary"),
        name="dsa",
    )(qt, qit, wit, k, ksq, ki, vt4)


def _mix_out_kernel(yt_ref, yat_ref, x_ref, g1_ref, sh_ref, sc_ref, gs_ref, ga_ref, wglu_ref, bglu_ref,
                    wo_ref, gf_ref, h_ref, hn_ref):
    def norm_cols(v, g):
        return v * lax.rsqrt(jnp.mean(v * v, axis=0, keepdims=True) + EPS) * g

    d_ssm, chunks, steps = yt_ref.shape
    y = _gelu(yt_ref[...].reshape(d_ssm, chunks * steps))
    z = jnp.dot(wglu_ref[...], y.astype(BF16), preferred_element_type=F32) + bglu_ref[...]
    y = y * jax.nn.sigmoid(z)
    mix_t = jnp.concatenate([norm_cols(y, gs_ref[...]), norm_cols(yat_ref[...], ga_ref[...])], axis=0)
    mix = jnp.dot(mix_t.T.astype(BF16), wo_ref[...], preferred_element_type=F32)
    h = x_ref[...] + g1_ref[0] * mix
    h_ref[...] = h
    hn_ref[...] = (_rmsnorm_rows(h, gf_ref[...]) * (1.0 + sc_ref[0]) + sh_ref[0]).astype(BF16)


def _mix_out(yt3, yat, x2, g1, sh2, sc2, gs, ga, w_glu, b_glu, w_out, gf, seq, tm):
    d_ssm, _, steps = yt3.shape
    n, d = x2.shape
    d_attn = yat.shape[0]
    bsz = n // seq
    tiles_per_batch = seq // tm
    row = lambda i: (i, 0)
    col = lambda i: (0, i)
    const = lambda i: (0, 0)
    per_batch = lambda i: (i // tiles_per_batch, 0, 0)
    mod_spec = pl.BlockSpec((1, 1, d), per_batch)
    return pl.pallas_call(
        _mix_out_kernel,
        grid=(n // tm,),
        in_specs=[pl.BlockSpec((d_ssm, tm // steps, steps), lambda i: (0, i, 0)),
                  pl.BlockSpec((d_attn, tm), col),
                  pl.BlockSpec((tm, d), row),
                  mod_spec, mod_spec, mod_spec,
                  pl.BlockSpec((d_ssm, 1), const),
                  pl.BlockSpec((d_attn, 1), const),
                  pl.BlockSpec((d_ssm, d_ssm), const),
                  pl.BlockSpec((d_ssm, 1), const),
                  pl.BlockSpec((d_ssm + d_attn, d), const),
                  pl.BlockSpec((1, d), const)],
        out_specs=[pl.BlockSpec((tm, d), row), pl.BlockSpec((tm, d), row)],
        out_shape=[jax.ShapeDtypeStruct((n, d), F32), jax.ShapeDtypeStruct((n, d), BF16)],
        compiler_params=_params("parallel"),
        name="mix_out",
    )(yt3, yat, x2, g1.reshape(bsz, 1, d), sh2.reshape(bsz, 1, d), sc2.reshape(bsz, 1, d),
      gs.reshape(d_ssm, 1), ga.reshape(d_attn, 1), w_glu.T.astype(BF16), b_glu.reshape(d_ssm, 1),
      w_out.astype(BF16), gf.reshape(1, d))


def _ffn_kernel(tiles_per_batch, tf,
                hn_ref, halo_ref, h_ref, g2_ref, wu_ref, cw_ref, cb_ref, wd_ref,
                o_ref, lhs_scr, act_scr):
    i = pl.program_id(0)
    halo_rows = halo_ref.shape[0]
    lhs_scr[0:halo_rows, :] = jnp.where(i % tiles_per_batch == 0, 0, halo_ref[...]).astype(BF16)
    lhs_scr[halo_rows:, :] = hn_ref[...]

    d_ff = act_scr.shape[1]

    def conv_up(c0):
        up = jnp.dot(lhs_scr[...], wu_ref[:, c0:c0 + tf], preferred_element_type=F32)
        conv = cb_ref[:, c0:c0 + tf] + cw_ref[CONV_W - 1:CONV_W, c0:c0 + tf] * up
        for back in range(1, CONV_W):
            tap = CONV_W - 1 - back
            conv = conv + cw_ref[tap:tap + 1, c0:c0 + tf] * pltpu.roll(up, back, 0)
        return conv[halo_rows:]

    for c0 in range(0, d_ff, tf):
        act_scr[:, c0:c0 + tf] = (_gelu(conv_up(c0)) * conv_up(d_ff + c0)).astype(BF16)

    down = jnp.dot(act_scr[...], wd_ref[...], preferred_element_type=F32)
    o_ref[...] = h_ref[...] + g2_ref[0] * down


def _ffn(hn2, h1, g2, w_up, conv_w, conv_b, w_down, seq, tm, tf):
    n, d = h1.shape
    d_ff = w_down.shape[0]
    bsz = n // seq
    tiles_per_batch = seq // tm
    halo_rows = BF16_SUBLANES
    assert halo_rows >= CONV_W - 1 and tm % halo_rows == 0 and d_ff % tf == 0
    halo_per_tile = tm // halo_rows
    row = lambda i: (i, 0)
    const = lambda i: (0, 0)
    resident = pl.Buffered(1)

    return pl.pallas_call(
        functools.partial(_ffn_kernel, tiles_per_batch, tf),
        grid=(n // tm,),
        in_specs=[pl.BlockSpec((tm, d), row),
                  pl.BlockSpec((halo_rows, d), lambda i: (jnp.maximum(i * halo_per_tile - 1, 0), 0)),
                  pl.BlockSpec((tm, d), row),
                  pl.BlockSpec((1, 1, d), lambda i: (i // tiles_per_batch, 0, 0)),
                  pl.BlockSpec((d, 2 * d_ff), const, pipeline_mode=resident),
                  pl.BlockSpec((CONV_W, 2 * d_ff), const, pipeline_mode=resident),
                  pl.BlockSpec((1, 2 * d_ff), const, pipeline_mode=resident),
                  pl.BlockSpec((d_ff, d), const, pipeline_mode=resident)],
        out_specs=pl.BlockSpec((tm, d), row),
        out_shape=jax.ShapeDtypeStruct((n, d), F32),
        scratch_shapes=[pltpu.VMEM((halo_rows + tm, d), BF16),
                        pltpu.VMEM((tm, d_ff), BF16)],
        compiler_params=_params("parallel"),
        name="ffn",
    )(hn2, hn2, h1, g2.reshape(bsz, 1, d), w_up.astype(BF16), conv_w, conv_b.reshape(1, 2 * d_ff),
      w_down.astype(BF16))


def _tile(total, want):
    t = min(total, want)
    assert total % t == 0
    return t


def kernel(x, c, positions, w_ada, b_ada, norm_mix, w_in, ssm_log_dt, ssm_a_re, ssm_a_im, ssm_b_re, ssm_b_im, ssm_c_re, ssm_c_im, ssm_d, ssm_w_glu, ssm_b_glu, q_norm, k_norm, out_norm_ssm, out_norm_attn, w_out, norm_ffn, w_up, conv_w, conv_b, w_down):
    bsz, seq, d = x.shape
    depth = w_ada.shape[0]
    n = bsz * seq
    d_ssm = out_norm_ssm.shape[1]
    d_attn = out_norm_attn.shape[1]
    dh = q_norm.shape[1]
    d_ff = w_down.shape[1]
    assert w_in.shape[2] == d_ssm + 3 * d_attn + IDX_HEADS * IDX_DIM + IDX_DIM + IDX_HEADS
    assert seq % SSM_CHUNK == 0

    t_proj = _tile(seq, 1024)
    t_key = _tile(seq, 512)
    t_q = _tile(seq, 256)
    t_mix = _tile(seq, 1024)
    t_ffn = _tile(seq, 512)
    t_ff = _tile(d_ff, 256)

    h = x.astype(F32).reshape(n, d)
    pos_row = positions.reshape(1, n)
    for l in range(depth):
        mod = _adaln(c.astype(F32), w_ada[l].astype(F32), b_ada[l].astype(F32))
        sh1, sc1, g1, sh2, sc2, g2 = jnp.split(mod, 6, axis=-1)
        ut, qt, k, ksq, vt4, qit, ki, wit = _in_proj(h, norm_mix[l], sh1, sc1, w_in[l], pos_row, q_norm[l],
                                                     k_norm[l], seq, t_proj, t_key, d_ssm, d_attn, dh)
        yt = _s5(ut, seq, ssm_log_dt[l], ssm_a_re[l], ssm_a_im[l], ssm_b_re[l], ssm_b_im[l],
                 ssm_c_re[l], ssm_c_im[l], ssm_d[l])
        yat = _dsa(qt, qit, wit, k, ksq, ki, vt4, seq, t_q, dh)
        h1, hn2 = _mix_out(yt, yat, h, g1, sh2, sc2, out_norm_ssm[l], out_norm_attn[l], ssm_w_glu[l],
                           ssm_b_glu[l], w_out[l], norm_ffn[l], seq, t_mix)
        h = _ffn(hn2, h1, g2, w_up[l], conv_w[l], conv_b[l], w_down[l], seq, t_ffn, t_ff)
    return h.reshape(bsz, seq, d).astype(x.dtype)
```

```python
import functools
import math

import jax
import jax.numpy as jnp
from jax import lax
from jax.experimental import pallas as pl
from jax.experimental.pallas import tpu as pltpu

EPS = 1e-6
ROPE_THETA = 10000.0
IDX_HEADS = 8
IDX_DIM = 64
TOPK_MAX = 256
CONV_W = 3

LANES = 128
BF16_SUBLANES = 16
SSM_CHUNK = LANES
TOEPLITZ_BLOCK = 2
PART_ROWS = 32
VMEM_LIMIT = 56 * 1024 * 1024
NEG_BIG = -1e30
LOG2E = math.log2(math.e)

F32 = jnp.float32
BF16 = jnp.bfloat16
HIGHEST = lax.Precision.HIGHEST


def _params(*semantics):
    return pltpu.CompilerParams(dimension_semantics=semantics, vmem_limit_bytes=VMEM_LIMIT)


def _gelu(x):
    return 0.5 * x * (1.0 + jnp.tanh(math.sqrt(2.0 / math.pi) * (x + 0.044715 * (x * x * x))))


def _adaln_kernel(ct_ref, w_ref, b_ref, o_ref):
    ct = ct_ref[...]
    cs = ct * jax.nn.sigmoid(ct)
    w = w_ref[...]
    for b in range(ct.shape[1]):
        o_ref[b:b + 1, :] = jnp.sum(w * cs[:, b:b + 1], axis=0, keepdims=True) + b_ref[...]


def _adaln(c, w, bias):
    bsz, d = c.shape
    n = w.shape[1]
    tn = 1024 if n % 1024 == 0 else n
    return pl.pallas_call(
        _adaln_kernel,
        grid=(n // tn,),
        in_specs=[pl.BlockSpec((d, bsz), lambda j: (0, 0)),
                  pl.BlockSpec((d, tn), lambda j: (0, j)),
                  pl.BlockSpec((1, tn), lambda j: (0, j))],
        out_specs=pl.BlockSpec((bsz, tn), lambda j: (0, j)),
        out_shape=jax.ShapeDtypeStruct((bsz, n), F32),
        compiler_params=_params("parallel"),
        name="adaln",
    )(c.T, w, bias.reshape(1, n))


def _rmsnorm_rows(x, g):
    return x * lax.rsqrt(jnp.mean(x * x, axis=-1, keepdims=True) + EPS) * g


def _rope_t(x3, cos, sin):
    half = x3.shape[1] // 2
    x1, x2 = x3[:, :half, :], x3[:, half:, :]
    return jnp.concatenate([x1 * cos - x2 * sin, x1 * sin + x2 * cos], axis=1)


def _pair_rows(x3):
    heads, dh, t = x3.shape
    zero = jnp.zeros_like(x3)
    even = lax.broadcasted_iota(jnp.int32, (heads, 1, 1), 0) % 2 == 0
    return jnp.concatenate([jnp.where(even, x3, zero), jnp.where(even, zero, x3)], axis=1).reshape(heads * 2 * dh, t)


def _in_proj_kernel(d_ssm, d_attn, dh, wi_scale,
                    x_ref, g_ref, sh_ref, sc_ref, w_ref, pos_ref, inv_ref, qg_ref, kg_ref,
                    ut_ref, qt_ref, qsq_ref, k_ref, ksq_ref, vt_ref, qit_ref, ki_ref, wit_ref):
    hn = _rmsnorm_rows(x_ref[...], g_ref[...]) * (1.0 + sc_ref[0]) + sh_ref[0]
    hb = hn.astype(BF16)
    tm = hb.shape[0]

    def proj_t(r0, rows):
        return lax.dot_general(w_ref[:, r0:r0 + rows], hb, (((0,), (1,)), ((), ())),
                               preferred_element_type=F32)

    ang = inv_ref[...] * pos_ref[...].astype(F32)
    cos, sin = jnp.cos(ang)[None], jnp.sin(ang)[None]

    def head_norm(x3, gcol):
        return x3 * lax.rsqrt(jnp.mean(x3 * x3, axis=1, keepdims=True) + EPS) * gcol[None]

    ut_ref[...] = proj_t(0, d_ssm).reshape(d_ssm, tm // LANES, LANES)
    o = d_ssm
    n_heads = d_attn // dh
    q3 = proj_t(o, d_attn).reshape(n_heads, dh, tm)
    q3 = _rope_t(head_norm(q3, qg_ref[...]), cos, sin) * (dh ** -0.5 * LOG2E)
    qt_ref[...] = _pair_rows(q3).astype(BF16)
    qsq_ref[...] = jnp.max(jnp.sum(q3 * q3, axis=1), axis=0, keepdims=True)
    o += d_attn
    k3 = _rope_t(head_norm(proj_t(o, d_attn).reshape(n_heads, dh, tm), kg_ref[...]), cos, sin)
    k_ref[...] = k3.reshape(d_attn, tm).T.astype(BF16)
    ksq = jnp.max(jnp.sum(k3 * k3, axis=1), axis=1, keepdims=True)
    ksq_ref[...] = jnp.broadcast_to(ksq, ksq_ref.shape)
    o += d_attn
    v3 = proj_t(o, d_attn).reshape(n_heads, dh, tm)
    vt = jnp.concatenate([v3, jnp.ones((n_heads, BF16_SUBLANES, tm), F32)], axis=1).astype(BF16)
    kc = vt_ref.shape[3]
    for j in range(tm // kc):
        vt_ref[j] = vt[:, :, j * kc:(j + 1) * kc]
    o += d_attn
    n_qi = IDX_HEADS * IDX_DIM
    qi3 = _rope_t(proj_t(o, n_qi).reshape(IDX_HEADS, IDX_DIM, tm), cos, sin)
    qit_ref[...] = _pair_rows(qi3).astype(BF16)
    o += n_qi
    tail = proj_t(o, IDX_DIM + BF16_SUBLANES)
    ki = _rope_t(tail[None, :IDX_DIM], cos, sin)[0]
    ki_ref[...] = jnp.concatenate([ki, ki], axis=0).T.astype(BF16)
    wit_ref[...] = tail[IDX_DIM:IDX_DIM + IDX_HEADS] * wi_scale


def _in_proj(x2, g, sh, sc, w_in, pos_row, q_norm, k_norm, seq, tm, kc, d_ssm, d_attn, dh):
    n, d = x2.shape
    bsz = n // seq
    tiles_per_batch = seq // tm
    n_qi = IDX_HEADS * IDX_DIM
    n_heads = d_attn // dh
    w = jnp.concatenate([w_in.astype(BF16), jnp.zeros((d, BF16_SUBLANES - IDX_HEADS), BF16)], axis=1)
    rt = w.shape[1]
    inv = (ROPE_THETA ** (-jnp.arange(0, dh, 2, dtype=F32) / dh)).reshape(dh // 2, 1)
    wi_scale = IDX_HEADS ** -0.5 * IDX_DIM ** -0.5

    row = lambda i: (i, 0)
    col = lambda i: (0, i)
    const = lambda i: (0, 0)
    per_batch = lambda i: (i // tiles_per_batch, 0, 0)
    return pl.pallas_call(
        functools.partial(_in_proj_kernel, d_ssm, d_attn, dh, wi_scale),
        grid=(n // tm,),
        in_specs=[pl.BlockSpec((tm, d), row),
                  pl.BlockSpec((1, d), const),
                  pl.BlockSpec((1, 1, d), per_batch),
                  pl.BlockSpec((1, 1, d), per_batch),
                  pl.BlockSpec((d, rt), const),
                  pl.BlockSpec((1, tm), col),
                  pl.BlockSpec((dh // 2, 1), const),
                  pl.BlockSpec((dh, 1), const),
                  pl.BlockSpec((dh, 1), const)],
        out_specs=[pl.BlockSpec((d_ssm, tm // LANES, LANES), lambda i: (0, i, 0)),
                   pl.BlockSpec((2 * d_attn, tm), col),
                   pl.BlockSpec((1, tm), col),
                   pl.BlockSpec((tm, d_attn), row),
                   pl.BlockSpec((n_heads, LANES), row),
                   pl.BlockSpec((tm // kc, n_heads, dh + BF16_SUBLANES, kc), lambda i: (i, 0, 0, 0)),
                   pl.BlockSpec((2 * n_qi, tm), col),
                   pl.BlockSpec((tm, 2 * IDX_DIM), row),
                   pl.BlockSpec((IDX_HEADS, tm), col)],
        out_shape=[jax.ShapeDtypeStruct((d_ssm, n // LANES, LANES), F32),
                   jax.ShapeDtypeStruct((2 * d_attn, n), BF16),
                   jax.ShapeDtypeStruct((1, n), F32),
                   jax.ShapeDtypeStruct((n, d_attn), BF16),
                   jax.ShapeDtypeStruct((n // tm * n_heads, LANES), F32),
                   jax.ShapeDtypeStruct((n // kc, n_heads, dh + BF16_SUBLANES, kc), BF16),
                   jax.ShapeDtypeStruct((2 * n_qi, n), BF16),
                   jax.ShapeDtypeStruct((n, 2 * IDX_DIM), BF16),
                   jax.ShapeDtypeStruct((IDX_HEADS, n), F32)],
        compiler_params=_params("parallel"),
        name="in_proj",
    )(x2, g.reshape(1, d), sh.reshape(bsz, 1, d), sc.reshape(bsz, 1, d), w, pos_row, inv,
      q_norm.reshape(dh, 1), k_norm.reshape(dh, 1))


def _cmul(ar, ai, br, bi):
    return ar * br - ai * bi, ar * bi + ai * br


def _s5_kernel(chunks_per_batch,
               ut_ref, ldt_ref, ar_row_ref, ai_row_ref, ar_col_ref, ai_col_ref,
               btr_ref, bti_ref, cr_ref, ci_ref, ctr_ref, cti_ref, d_ref,
               y_ref, k2_scr, tm_scr, w_scr, v_scr, yi_scr):
    nc, p, t = btr_ref.shape[1], btr_ref.shape[2], SSM_CHUNK
    dt = jnp.exp(ldt_ref[0])
    lr, li = ar_row_ref[0], ai_row_ref[0]
    mag = jnp.exp(lr * dt)
    lbr, lbi = mag * jnp.cos(li * dt), mag * jnp.sin(li * dt)
    den = lr * lr + li * li
    nr, ni = lbr - 1.0, lbi
    coef_r, coef_i = (nr * lr + ni * li) / den, (ni * lr - nr * li) / den
    bbr, bbi = _cmul(coef_r, coef_i, btr_ref[0], bti_ref[0])

    def powers(rate, freq, n):
        m = jnp.exp(rate * n)
        return m * jnp.cos(freq * n), m * jnp.sin(freq * n)

    rate_c, freq_c = ar_col_ref[0] * dt, ai_col_ref[0] * dt
    rate_r, freq_r = lr * dt, li * dt
    tau = lax.broadcasted_iota(jnp.int32, (1, t), 1).astype(F32)
    pr0, pi0 = powers(rate_c, freq_c, tau)
    pr1, pi1 = _cmul(pr0, pi0, *powers(rate_c, freq_c, 1.0))
    back = (t - 1) - lax.broadcasted_iota(jnp.int32, (t, 1), 0)
    qr, qi = powers(rate_r, freq_r, back.astype(F32))

    cr, ci = cr_ref[0], ci_ref[0]
    rows = []
    for c in range(nc):
        gr, gi = _cmul(cr[c:c + 1], ci[c:c + 1], bbr, bbi)
        rows.append(jnp.concatenate([gr, -gi], axis=1))
    k2_scr[...] = jnp.dot(jnp.concatenate(rows, axis=0), jnp.concatenate([pr0, pi0], axis=0),
                          precision=HIGHEST, preferred_element_type=F32)

    causal = (lax.broadcasted_iota(jnp.int32, (t, t), 1) >= lax.broadcasted_iota(jnp.int32, (t, t), 0))

    width = tm_scr.shape[2] // t
    n_blocks = nc // width

    def build_block(q):
        for e in range(width):
            c = q * width + e
            for cp in range(nc):
                krow = jnp.broadcast_to(k2_scr[pl.ds(c * nc + cp, 1), :], (t, t))
                tile = pltpu.roll(krow, 0, 1, stride=1, stride_axis=0)
                tm_scr[q, cp * t:(cp + 1) * t, e * t:(e + 1) * t] = jnp.where(causal, tile, 0.0).astype(BF16)

    a = jnp.concatenate([ut_ref[c] for c in range(nc)], axis=1).astype(BF16)

    q2r, q2i = jnp.concatenate([qr, qr], axis=1), jnp.concatenate([qi, qi], axis=1)
    b_a = jnp.concatenate([bbr, bbi], axis=1)
    b_b = jnp.concatenate([-bbi, bbr], axis=1)
    for c in range(nc):
        w_scr[c * t:(c + 1) * t, :] = (q2r * b_a[c:c + 1] + q2i * b_b[c:c + 1]).astype(BF16)
    s = jnp.dot(a, w_scr[...], preferred_element_type=F32)

    r = s.shape[0]
    kidx = lax.broadcasted_iota(jnp.int32, (r, 1), 0) % chunks_per_batch

    def times_lam_pow(z, n):
        zr, zi = powers(rate_r, freq_r, float(n))
        return (jnp.concatenate([zr, zr], axis=1) * z
                + jnp.concatenate([-zi, zi], axis=1) * pltpu.roll(z, p, 1))

    step = 1
    while step < chunks_per_batch:
        prev = pltpu.roll(s, step, 0)
        s = s + jnp.where(kidx >= step, times_lam_pow(prev, t * step), 0.0)
        step *= 2
    e = jnp.where(kidx >= 1, pltpu.roll(s, 1, 0), 0.0)

    ctr, cti = ctr_ref[0], cti_ref[0]
    p2r, p2i = jnp.concatenate([pr1, pr1], axis=0), jnp.concatenate([pi1, pi1], axis=0)
    c_a = jnp.concatenate([ctr, -cti], axis=0)
    c_b = jnp.concatenate([-cti, -ctr], axis=0)
    for c in range(nc):
        v_scr[:, c * t:(c + 1) * t] = (p2r * c_a[:, c:c + 1] + p2i * c_b[:, c:c + 1]).astype(BF16)
    y_inter = jnp.dot(e.astype(BF16), v_scr[...], preferred_element_type=F32)
    for q in range(n_blocks):
        yi_scr[q] = y_inter[:, q * width * t:(q + 1) * width * t]

    def emit_block(q):
        y = jnp.dot(a, tm_scr[q], preferred_element_type=F32) + yi_scr[q]
        for e in range(width):
            c = q * width + e
            y_ref[c] = y[:, e * t:(e + 1) * t] + d_ref[c] * ut_ref[c]

    build_block(0)

    def trip(q, carry):
        build_block(q + 1)
        emit_block(q)
        return carry

    lax.fori_loop(0, n_blocks - 1, trip, 0)
    emit_block(n_blocks - 1)


def _s5(ut3, seq, log_dt, a_re, a_im, b_re, b_im, c_re, c_im, d_skip):
    d_ssm, r, t = ut3.shape
    assert t == SSM_CHUNK
    n = r * t
    g, p = a_re.shape
    nc = b_re.shape[2]
    grp = lambda i: (i, 0, 0)
    row_spec = pl.BlockSpec((1, 1, p), grp)
    col_spec = pl.BlockSpec((1, p, 1), grp)
    cp_spec = pl.BlockSpec((1, nc, p), grp)
    pc_spec = pl.BlockSpec((1, p, nc), grp)
    return pl.pallas_call(
        functools.partial(_s5_kernel, seq // t),
        grid=(g,),
        in_specs=[pl.BlockSpec((nc, r, t), grp),
                  pl.BlockSpec((1, 1, 1), grp),
                  row_spec, row_spec, col_spec, col_spec,
                  cp_spec, cp_spec, cp_spec, cp_spec, pc_spec, pc_spec,
                  pl.BlockSpec((nc, 1, 1), grp)],
        out_specs=pl.BlockSpec((nc, r, t), grp),
        out_shape=jax.ShapeDtypeStruct((d_ssm, r, t), F32),
        scratch_shapes=[pltpu.VMEM((nc * nc, t), F32),
                        pltpu.VMEM((nc // TOEPLITZ_BLOCK, nc * t, TOEPLITZ_BLOCK * t), BF16),
                        pltpu.VMEM((nc * t, 2 * p), BF16),
                        pltpu.VMEM((2 * p, nc * t), BF16),
                        pltpu.VMEM((nc // TOEPLITZ_BLOCK, r, TOEPLITZ_BLOCK * t), F32)],
        compiler_params=_params("parallel"),
        name="s5",
    )(ut3, log_dt.reshape(g, 1, 1),
      a_re.reshape(g, 1, p), a_im.reshape(g, 1, p), a_re.reshape(g, p, 1), a_im.reshape(g, p, 1),
      jnp.swapaxes(b_re, 1, 2), jnp.swapaxes(b_im, 1, 2), c_re, c_im,
      jnp.swapaxes(c_re, 1, 2), jnp.swapaxes(c_im, 1, 2), d_skip.reshape(d_ssm, 1, 1))


def _dsa_kernel(topk, seq, tq, kc, dh, bisect_passes,
                qt_ref, qsq_ref, qit_ref, wit_ref, k_ref, ksq_ref, ki_ref, vt_ref, o_ref,
                i_scr, m_scr, l_scr, acc_scr):
    i = pl.program_id(1)
    n_heads = acc_scr.shape[0]
    n_chunks = (i * tq + tq + kc - 1) // kc

    def head_rows(ref, h):
        return ref[h * LANES:(h + 1) * LANES, :]

    qpos = i * tq + lax.broadcasted_iota(jnp.int32, (1, tq), 1)
    krow = lax.broadcasted_iota(jnp.int32, (kc, 1), 0)
    wi = wit_ref[...]
    qi_all = jnp.concatenate([head_rows(qit_ref, h) for h in range(IDX_HEADS)], axis=1)

    def chunk_loop(body, carry):
        done = 0
        for width, shift in ((4, 2), (2, 1), (1, 0)):
            trips = (n_chunks - done) >> shift
            carry = lax.fori_loop(
                0, trips, lambda j, cr, w=width, d=done: body(tuple(d + w * j + u for u in range(w)), cr), carry)
            done = done + trips * width
        return carry

    def index_chunks(chunks, carry):
        lo, hi = carry
        starts = [pl.multiple_of(c * kc, kc) for c in chunks]
        dots = [jnp.dot(ki_ref[pl.ds(r0, kc), :], qi_all, preferred_element_type=F32) for r0 in starts]
        for c, r0, s_all in zip(chunks, starts, dots):
            score = jnp.zeros((kc, tq), F32)
            for h in range(IDX_HEADS):
                score = score + wi[h:h + 1, :] * jnp.maximum(s_all[:, h * tq:(h + 1) * tq], 0.0)
            i_scr[c] = jnp.where(r0 + krow <= qpos, score, -jnp.inf)
            lo = jnp.minimum(lo, jnp.min(score, axis=0, keepdims=True))
            hi = jnp.maximum(hi, jnp.max(score, axis=0, keepdims=True))
        return lo, hi

    lo, hi = chunk_loop(index_chunks, (jnp.full((1, tq), jnp.inf, F32), jnp.full((1, tq), -jnp.inf, F32)))

    def over_chunks(per_chunk, combine, init):
        def body(chunks, acc):
            for c in chunks:
                acc = combine(acc, per_chunk(i_scr[c]).reshape(kc // PART_ROWS, PART_ROWS, tq))
            return acc
        return chunk_loop(body, jnp.full((PART_ROWS, tq), init, F32))

    def count(pred):
        parts = over_chunks(lambda x: jnp.where(pred(x), 1.0, 0.0),
                            lambda acc, part: acc + jnp.sum(part, axis=0), 0.0)
        return jnp.sum(parts, axis=0, keepdims=True)

    def max_below(bound):
        parts = over_chunks(lambda x: jnp.where(x < bound, x, -jnp.inf),
                            lambda acc, part: jnp.maximum(acc, jnp.max(part, axis=0)), -jnp.inf)
        return jnp.max(parts, axis=0, keepdims=True)

    kf = float(topk)
    n_causal = (qpos + 1).astype(F32)
    hi = hi + jnp.abs(hi) * 2.0 ** -20 + 1e-30

    def bisect(_, st):
        lo, hi, c_lo, active = st
        mid = 0.5 * lo + 0.5 * hi
        c = count(lambda x: x >= mid)
        upd = jnp.logical_and(active > 0.0, jnp.logical_and(mid > lo, mid < hi))
        up_lo = jnp.logical_and(upd, c >= kf)
        up_hi = jnp.logical_and(upd, c < kf)
        lo = jnp.where(up_lo, mid, lo)
        c_lo = jnp.where(up_lo, c, c_lo)
        hi = jnp.where(up_hi, mid, hi)
        return lo, hi, c_lo, jnp.where(c_lo > kf, active, 0.0)

    active0 = jnp.where(n_causal > kf, 1.0, 0.0)
    lo, hi, c_lo, active = lax.fori_loop(0, bisect_passes, bisect, (lo, hi, n_causal, active0))

    def stepping(st):
        return jnp.max(st[3]) > 0.0

    def step_down(st):
        lo, hi, c_lo, active = st
        v = max_below(hi)
        c = count(lambda x: x >= v)
        on = active > 0.0
        found = jnp.logical_and(on, c >= kf)
        lo = jnp.where(found, v, lo)
        c_lo = jnp.where(found, c, c_lo)
        hi = jnp.where(jnp.logical_and(on, c < kf), v, hi)
        return lo, hi, c_lo, jnp.where(found, 0.0, active)

    thr, _, c_thr, _ = lax.while_loop(stepping, step_down, (lo, hi, c_lo, active))
    tied = jnp.max(jnp.where(c_thr > kf, 1.0, 0.0)) > 0.0

    ksq = jnp.max(jnp.max(ksq_ref[...], axis=0, keepdims=True), axis=1, keepdims=True)
    shift = -1.02 * jnp.sqrt(qsq_ref[...] * ksq)

    @pl.when(jnp.logical_not(tied))
    def _():
        def mask_chunk(c, carry):
            i_scr[c] = jnp.where(i_scr[c] >= thr, shift, NEG_BIG)
            return carry
        lax.fori_loop(0, n_chunks, mask_chunk, 0)

    @pl.when(tied)
    def _():
        above = count(lambda x: x > thr)
        room = jnp.where(c_thr > kf, kf - above, float(seq))
        tri = (lax.broadcasted_iota(jnp.int32, (kc, kc), 0)
               >= lax.broadcasted_iota(jnp.int32, (kc, kc), 1)).astype(BF16)

        def mask_chunk(c, seen):
            x = i_scr[c]
            eq = jnp.where(x == thr, 1.0, 0.0)
            rank = jnp.dot(tri, eq.astype(BF16), preferred_element_type=F32) + seen
            keep = jnp.logical_or(x > thr, jnp.logical_and(x == thr, rank <= room))
            i_scr[c] = jnp.where(keep, shift, NEG_BIG)
            return seen + jnp.sum(eq, axis=0, keepdims=True)
        lax.fori_loop(0, n_chunks, mask_chunk, jnp.zeros((1, tq), F32))

    def logits(c, h):
        r0 = pl.multiple_of(c * kc, kc)
        kk = k_ref[pl.ds(r0, kc), (h // 2) * LANES:(h // 2 + 1) * LANES]
        return jnp.dot(kk, head_rows(qt_ref, h), preferred_element_type=F32) + i_scr[c]

    acc_scr[...] = jnp.zeros(acc_scr.shape, F32)

    def attend_chunks(chunks, carry):
        ahead = 4
        items = [(c, h) for c in chunks for h in range(n_heads)]
        probs = [jnp.exp2(logits(c, h)).astype(BF16) for c, h in items[:ahead]]
        for n, (c, h) in enumerate(items):
            if n + ahead < len(items):
                probs.append(jnp.exp2(logits(*items[n + ahead])).astype(BF16))
            acc_scr[h] += jnp.dot(vt_ref[c, h], probs[n], preferred_element_type=F32)
        return carry

    chunk_loop(attend_chunks, 0)

    l_min = jnp.full((1, tq), jnp.inf, F32)
    for h in range(n_heads):
        denom = acc_scr[h, dh:dh + 1, :]
        l_min = jnp.minimum(l_min, denom)
        o_ref[h * dh:(h + 1) * dh, :] = acc_scr[h, 0:dh, :] / denom

    @pl.when(jnp.min(l_min) < 2.0 ** -64)
    def _():
        m_scr[...] = jnp.full(m_scr.shape, NEG_BIG, F32)
        l_scr[...] = jnp.zeros(l_scr.shape, F32)
        acc_scr[...] = jnp.zeros(acc_scr.shape, F32)

        def attend_chunk_online(c, carry):
            for h in range(n_heads):
                s = logits(c, h)
                m_old = m_scr[h:h + 1, :]
                m_new = jnp.maximum(m_old, jnp.max(s, axis=0, keepdims=True))
                alpha = jnp.exp2(m_old - m_new)
                p = jnp.exp2(s - m_new)
                l_scr[h:h + 1, :] = alpha * l_scr[h:h + 1, :] + jnp.sum(p, axis=0, keepdims=True)
                m_scr[h:h + 1, :] = m_new
                pv = jnp.dot(vt_ref[c, h, 0:dh, :], p.astype(BF16), preferred_element_type=F32)
                acc_scr[h, 0:dh, :] = alpha * acc_scr[h, 0:dh, :] + pv
            return carry

        lax.fori_loop(0, n_chunks, attend_chunk_online, 0)
        for h in range(n_heads):
            o_ref[h * dh:(h + 1) * dh, :] = acc_scr[h, 0:dh, :] / l_scr[h:h + 1, :]


def _dsa(qt, qsq, qit, wit, k, ksq, ki, vt4, seq, tq, dh):
    n, d_attn = k.shape
    bsz = n // seq
    kc = vt4.shape[3]
    n_heads = d_attn // dh
    n_qi = IDX_HEADS * IDX_DIM
    assert 2 * dh == LANES and 2 * IDX_DIM == LANES and n_heads % 2 == 0 and IDX_HEADS % 2 == 0
    assert kc % PART_ROWS == 0
    topk = min(TOPK_MAX, seq // 4)
    blocks_per_batch = seq // tq
    chunks_per_batch = seq // kc
    bisect_passes = 14
    qcol = lambda b, i: (0, b * blocks_per_batch + i)
    per_batch = lambda b, i: (b, 0)
    return pl.pallas_call(
        functools.partial(_dsa_kernel, topk, seq, tq, kc, dh, bisect_passes),
        grid=(bsz, blocks_per_batch),
        in_specs=[pl.BlockSpec((2 * d_attn, tq), qcol),
                  pl.BlockSpec((1, tq), qcol),
                  pl.BlockSpec((2 * n_qi, tq), qcol),
                  pl.BlockSpec((IDX_HEADS, tq), qcol),
                  pl.BlockSpec((seq, d_attn), per_batch),
                  pl.BlockSpec((ksq.shape[0] // bsz, LANES), per_batch),
                  pl.BlockSpec((seq, 2 * IDX_DIM), per_batch),
                  pl.BlockSpec((chunks_per_batch, n_heads, dh + BF16_SUBLANES, kc), lambda b, i: (b, 0, 0, 0))],
        out_specs=pl.BlockSpec((d_attn, tq), qcol),
        out_shape=jax.ShapeDtypeStruct((d_attn, n), F32),
        scratch_shapes=[pltpu.VMEM((chunks_per_batch, kc, tq), F32),
                        pltpu.VMEM((n_heads, tq), F32),
                        pltpu.VMEM((n_heads, tq), F32),
                        pltpu.VMEM((n_heads, dh + BF16_SUBLANES, tq), F32)],
        compiler_params=_params("parallel", "arbitrary"),
        name="dsa",
    )(qt, qsq, qit, wit, k, ksq, ki, vt4)


def _mix_out_kernel(yt_ref, yat_ref, x_ref, g1_ref, sh_ref, sc_ref, gs_ref, ga_ref, wglu_ref, bglu_ref,
                    wo_ref, gf_ref, h_ref, hn_ref):
    def norm_cols(v, g):
        return v * lax.rsqrt(jnp.mean(v * v, axis=0, keepdims=True) + EPS) * g

    d_ssm, chunks, steps = yt_ref.shape
    y = _gelu(yt_ref[...].reshape(d_ssm, chunks * steps))
    z = jnp.dot(wglu_ref[...], y.astype(BF16), preferred_element_type=F32) + bglu_ref[...]
    y = y * jax.nn.sigmoid(z)
    mix_t = jnp.concatenate([norm_cols(y, gs_ref[...]), norm_cols(yat_ref[...], ga_ref[...])], axis=0)
    mix = jnp.dot(mix_t.T.astype(BF16), wo_ref[...], preferred_element_type=F32)
    h = x_ref[...] + g1_ref[0] * mix
    h_ref[...] = h
    hn_ref[...] = (_rmsnorm_rows(h, gf_ref[...]) * (1.0 + sc_ref[0]) + sh_ref[0]).astype(BF16)


def _mix_out(yt3, yat, x2, g1, sh2, sc2, gs, ga, w_glu, b_glu, w_out, gf, seq, tm):
    d_ssm, _, steps = yt3.shape
    n, d = x2.shape
    d_attn = yat.shape[0]
    bsz = n // seq
    tiles_per_batch = seq // tm
    row = lambda i: (i, 0)
    col = lambda i: (0, i)
    const = lambda i: (0, 0)
    per_batch = lambda i: (i // tiles_per_batch, 0, 0)
    mod_spec = pl.BlockSpec((1, 1, d), per_batch)
    return pl.pallas_call(
        _mix_out_kernel,
        grid=(n // tm,),
        in_specs=[pl.BlockSpec((d_ssm, tm // steps, steps), lambda i: (0, i, 0)),
                  pl.BlockSpec((d_attn, tm), col),
                  pl.BlockSpec((tm, d), row),
                  mod_spec, mod_spec, mod_spec,
                  pl.BlockSpec((d_ssm, 1), const),
                  pl.BlockSpec((d_attn, 1), const),
                  pl.BlockSpec((d_ssm, d_ssm), const),
                  pl.BlockSpec((d_ssm, 1), const),
                  pl.BlockSpec((d_ssm + d_attn, d), const),
                  pl.BlockSpec((1, d), const)],
        out_specs=[pl.BlockSpec((tm, d), row), pl.BlockSpec((tm, d), row)],
        out_shape=[jax.ShapeDtypeStruct((n, d), F32), jax.ShapeDtypeStruct((n, d), BF16)],
        compiler_params=_params("parallel"),
        name="mix_out",
    )(yt3, yat, x2, g1.reshape(bsz, 1, d), sh2.reshape(bsz, 1, d), sc2.reshape(bsz, 1, d),
      gs.reshape(d_ssm, 1), ga.reshape(d_attn, 1), w_glu.T.astype(BF16), b_glu.reshape(d_ssm, 1),
      w_out.astype(BF16), gf.reshape(1, d))


def _ffn_kernel(tiles_per_batch, tf,
                hn_ref, halo_ref, h_ref, g2_ref, wu_ref, cw_ref, cb_ref, wd_ref,
                o_ref, lhs_scr, act_scr):
    i = pl.program_id(0)
    halo_rows = halo_ref.shape[0]
    lhs_scr[0:halo_rows, :] = jnp.where(i % tiles_per_batch == 0, 0, halo_ref[...]).astype(BF16)
    lhs_scr[halo_rows:, :] = hn_ref[...]

    d_ff = act_scr.shape[1]

    def conv_up(c0):
        up = jnp.dot(lhs_scr[...], wu_ref[:, c0:c0 + tf], preferred_element_type=F32)
        conv = cb_ref[:, c0:c0 + tf] + cw_ref[CONV_W - 1:CONV_W, c0:c0 + tf] * up
        for back in range(1, CONV_W):
            tap = CONV_W - 1 - back
            conv = conv + cw_ref[tap:tap + 1, c0:c0 + tf] * pltpu.roll(up, back, 0)
        return conv[halo_rows:]

    for c0 in range(0, d_ff, tf):
        act_scr[:, c0:c0 + tf] = (_gelu(conv_up(c0)) * conv_up(d_ff + c0)).astype(BF16)

    down = jnp.dot(act_scr[...], wd_ref[...], preferred_element_type=F32)
    o_ref[...] = h_ref[...] + g2_ref[0] * down


def _ffn(hn2, h1, g2, w_up, conv_w, conv_b, w_down, seq, tm, tf):
    n, d = h1.shape
    d_ff = w_down.shape[0]
    bsz = n // seq
    tiles_per_batch = seq // tm
    halo_rows = BF16_SUBLANES
    assert halo_rows >= CONV_W - 1 and tm % halo_rows == 0 and d_ff % tf == 0
    halo_per_tile = tm // halo_rows
    row = lambda i: (i, 0)
    const = lambda i: (0, 0)
    resident = pl.Buffered(1)

    return pl.pallas_call(
        functools.partial(_ffn_kernel, tiles_per_batch, tf),
        grid=(n // tm,),
        in_specs=[pl.BlockSpec((tm, d), row),
                  pl.BlockSpec((halo_rows, d), lambda i: (jnp.maximum(i * halo_per_tile - 1, 0), 0)),
                  pl.BlockSpec((tm, d), row),
                  pl.BlockSpec((1, 1, d), lambda i: (i // tiles_per_batch, 0, 0)),
                  pl.BlockSpec((d, 2 * d_ff), const, pipeline_mode=resident),
                  pl.BlockSpec((CONV_W, 2 * d_ff), const, pipeline_mode=resident),
                  pl.BlockSpec((1, 2 * d_ff), const, pipeline_mode=resident),
                  pl.BlockSpec((d_ff, d), const, pipeline_mode=resident)],
        out_specs=pl.BlockSpec((tm, d), row),
        out_shape=jax.ShapeDtypeStruct((n, d), F32),
        scratch_shapes=[pltpu.VMEM((halo_rows + tm, d), BF16),
                        pltpu.VMEM((tm, d_ff), BF16)],
        compiler_params=_params("parallel"),
        name="ffn",
    )(hn2, hn2, h1, g2.reshape(bsz, 1, d), w_up.astype(BF16), conv_w, conv_b.reshape(1, 2 * d_ff),
      w_down.astype(BF16))


def _tile(total, want):
    t = min(total, want)
    assert total % t == 0
    return t


def kernel(x, c, positions, w_ada, b_ada, norm_mix, w_in, ssm_log_dt, ssm_a_re, ssm_a_im, ssm_b_re, ssm_b_im, ssm_c_re, ssm_c_im, ssm_d, ssm_w_glu, ssm_b_glu, q_norm, k_norm, out_norm_ssm, out_norm_attn, w_out, norm_ffn, w_up, conv_w, conv_b, w_down):
    bsz, seq, d = x.shape
    depth = w_ada.shape[0]
    n = bsz * seq
    d_ssm = out_norm_ssm.shape[1]
    d_attn = out_norm_attn.shape[1]
    dh = q_norm.shape[1]
    d_ff = w_down.shape[1]
    assert w_in.shape[2] == d_ssm + 3 * d_attn + IDX_HEADS * IDX_DIM + IDX_DIM + IDX_HEADS
    assert seq % SSM_CHUNK == 0

    t_proj = _tile(seq, 1024)
    t_key = _tile(seq, 512)
    t_q = _tile(seq, 256)
    t_mix = _tile(seq, 1024)
    t_ffn = _tile(seq, 512)
    t_ff = _tile(d_ff, 256)

    h = x.astype(F32).reshape(n, d)
    pos_row = positions.reshape(1, n)
    for l in range(depth):
        mod = _adaln(c.astype(F32), w_ada[l].astype(F32), b_ada[l].astype(F32))
        sh1, sc1, g1, sh2, sc2, g2 = jnp.split(mod, 6, axis=-1)
        ut, qt, qsq, k, ksq, vt4, qit, ki, wit = _in_proj(h, norm_mix[l], sh1, sc1, w_in[l], pos_row, q_norm[l],
                                                          k_norm[l], seq, t_proj, t_key, d_ssm, d_attn, dh)
        yt = _s5(ut, seq, ssm_log_dt[l], ssm_a_re[l], ssm_a_im[l], ssm_b_re[l], ssm_b_im[l],
                 ssm_c_re[l], ssm_c_im[l], ssm_d[l])
        yat = _dsa(qt, qsq, qit, wit, k, ksq, ki, vt4, seq, t_q, dh)
        h1, hn2 = _mix_out(yt, yat, h, g1, sh2, sc2, out_norm_ssm[l], out_norm_attn[l], ssm_w_glu[l],
                           ssm_b_glu[l], w_out[l], norm_ffn[l], seq, t_mix)
        h = _ffn(hn2, h1, g2, w_up[l], conv_w[l], conv_b[l], w_down[l], seq, t_ffn, t_ff)
    return h.reshape(bsz, seq, d).astype(x.dtype)
```

```python
import functools
import math

import jax
import jax.numpy as jnp
from jax import lax
from jax.experimental import pallas as pl
from jax.experimental.pallas import tpu as pltpu

EPS = 1e-6
ROPE_THETA = 10000.0
IDX_HEADS = 8
IDX_DIM = 64
TOPK_MAX = 256
CONV_W = 3

LANES = 128
BF16_SUBLANES = 16
SSM_CHUNK = LANES
TOEPLITZ_BLOCK = 2
PART_ROWS = 32
VMEM_LIMIT = 56 * 1024 * 1024
NEG_BIG = -1e30
LOG2E = math.log2(math.e)

F32 = jnp.float32
BF16 = jnp.bfloat16
HIGHEST = lax.Precision.HIGHEST


def _params(*semantics):
    return pltpu.CompilerParams(dimension_semantics=semantics, vmem_limit_bytes=VMEM_LIMIT)


def _gelu(x):
    return 0.5 * x * (1.0 + jnp.tanh(math.sqrt(2.0 / math.pi) * (x + 0.044715 * (x * x * x))))


def _adaln_kernel(ct_ref, w_ref, b_ref, o_ref):
    ct = ct_ref[...]
    cs = ct * jax.nn.sigmoid(ct)
    w = w_ref[...]
    for b in range(ct.shape[1]):
        o_ref[b:b + 1, :] = jnp.sum(w * cs[:, b:b + 1], axis=0, keepdims=True) + b_ref[...]


def _adaln(c, w, bias):
    bsz, d = c.shape
    n = w.shape[1]
    tn = 1024 if n % 1024 == 0 else n
    return pl.pallas_call(
        _adaln_kernel,
        grid=(n // tn,),
        in_specs=[pl.BlockSpec((d, bsz), lambda j: (0, 0)),
                  pl.BlockSpec((d, tn), lambda j: (0, j)),
                  pl.BlockSpec((1, tn), lambda j: (0, j))],
        out_specs=pl.BlockSpec((bsz, tn), lambda j: (0, j)),
        out_shape=jax.ShapeDtypeStruct((bsz, n), F32),
        compiler_params=_params("parallel"),
        name="adaln",
    )(c.T, w, bias.reshape(1, n))


def _rmsnorm_rows(x, g):
    return x * lax.rsqrt(jnp.mean(x * x, axis=-1, keepdims=True) + EPS) * g


def _rope_t(x3, cos, sin):
    half = x3.shape[1] // 2
    x1, x2 = x3[:, :half, :], x3[:, half:, :]
    return jnp.concatenate([x1 * cos - x2 * sin, x1 * sin + x2 * cos], axis=1)


def _pair_rows(x3):
    heads, dh, t = x3.shape
    zero = jnp.zeros_like(x3)
    even = lax.broadcasted_iota(jnp.int32, (heads, 1, 1), 0) % 2 == 0
    return jnp.concatenate([jnp.where(even, x3, zero), jnp.where(even, zero, x3)], axis=1).reshape(heads * 2 * dh, t)


def _in_proj_kernel(d_ssm, d_attn, dh, wi_scale,
                    x_ref, g_ref, sh_ref, sc_ref, w_ref, pos_ref, inv_ref, qg_ref, kg_ref,
                    ut_ref, qt_ref, k_ref, ksq_ref, vt_ref, qit_ref, ki_ref, wit_ref):
    hn = _rmsnorm_rows(x_ref[...], g_ref[...]) * (1.0 + sc_ref[0]) + sh_ref[0]
    hb = hn.astype(BF16)
    tm = hb.shape[0]

    def proj_t(r0, rows):
        return lax.dot_general(w_ref[:, r0:r0 + rows], hb, (((0,), (1,)), ((), ())),
                               preferred_element_type=F32)

    ang = inv_ref[...] * pos_ref[...].astype(F32)
    cos, sin = jnp.cos(ang)[None], jnp.sin(ang)[None]

    def head_norm(x3, gcol):
        return x3 * lax.rsqrt(jnp.mean(x3 * x3, axis=1, keepdims=True) + EPS) * gcol[None]

    ut_ref[...] = proj_t(0, d_ssm).reshape(d_ssm, tm // LANES, LANES)
    o = d_ssm
    n_heads = d_attn // dh
    q3 = proj_t(o, d_attn).reshape(n_heads, dh, tm)
    q3 = _rope_t(head_norm(q3, qg_ref[...]), cos, sin) * (dh ** -0.5 * LOG2E)
    qt_ref[...] = _pair_rows(q3).astype(BF16)
    o += d_attn
    k3 = _rope_t(head_norm(proj_t(o, d_attn).reshape(n_heads, dh, tm), kg_ref[...]), cos, sin)
    k_ref[...] = k3.reshape(d_attn, tm).T.astype(BF16)
    ksq = jnp.max(jnp.sum(k3 * k3, axis=1), axis=1, keepdims=True)
    ksq_ref[...] = jnp.broadcast_to(ksq, ksq_ref.shape)
    o += d_attn
    v3 = proj_t(o, d_attn).reshape(n_heads, dh, tm)
    vt = jnp.concatenate([v3, jnp.ones((n_heads, BF16_SUBLANES, tm), F32)], axis=1).astype(BF16)
    kc = vt_ref.shape[3]
    for j in range(tm // kc):
        vt_ref[j] = vt[:, :, j * kc:(j + 1) * kc]
    o += d_attn
    n_qi = IDX_HEADS * IDX_DIM
    qi3 = _rope_t(proj_t(o, n_qi).reshape(IDX_HEADS, IDX_DIM, tm), cos, sin)
    qit_ref[...] = _pair_rows(qi3).astype(BF16)
    o += n_qi
    tail = proj_t(o, IDX_DIM + BF16_SUBLANES)
    ki = _rope_t(tail[None, :IDX_DIM], cos, sin)[0]
    ki_ref[...] = jnp.concatenate([ki, ki], axis=0).T.astype(BF16)
    wit_ref[...] = tail[IDX_DIM:IDX_DIM + IDX_HEADS] * wi_scale


def _in_proj(x2, g, sh, sc, w_in, pos_row, q_norm, k_norm, seq, tm, kc, d_ssm, d_attn, dh):
    n, d = x2.shape
    bsz = n // seq
    tiles_per_batch = seq // tm
    n_qi = IDX_HEADS * IDX_DIM
    n_heads = d_attn // dh
    w = jnp.concatenate([w_in.astype(BF16), jnp.zeros((d, BF16_SUBLANES - IDX_HEADS), BF16)], axis=1)
    rt = w.shape[1]
    inv = (ROPE_THETA ** (-jnp.arange(0, dh, 2, dtype=F32) / dh)).reshape(dh // 2, 1)
    wi_scale = IDX_HEADS ** -0.5 * IDX_DIM ** -0.5

    row = lambda i: (i, 0)
    col = lambda i: (0, i)
    const = lambda i: (0, 0)
    per_batch = lambda i: (i // tiles_per_batch, 0, 0)
    return pl.pallas_call(
        functools.partial(_in_proj_kernel, d_ssm, d_attn, dh, wi_scale),
        grid=(n // tm,),
        in_specs=[pl.BlockSpec((tm, d), row),
                  pl.BlockSpec((1, d), const),
                  pl.BlockSpec((1, 1, d), per_batch),
                  pl.BlockSpec((1, 1, d), per_batch),
                  pl.BlockSpec((d, rt), const),
                  pl.BlockSpec((1, tm), col),
                  pl.BlockSpec((dh // 2, 1), const),
                  pl.BlockSpec((dh, 1), const),
                  pl.BlockSpec((dh, 1), const)],
        out_specs=[pl.BlockSpec((d_ssm, tm // LANES, LANES), lambda i: (0, i, 0)),
                   pl.BlockSpec((2 * d_attn, tm), col),
                   pl.BlockSpec((tm, d_attn), row),
                   pl.BlockSpec((n_heads, LANES), row),
                   pl.BlockSpec((tm // kc, n_heads, dh + BF16_SUBLANES, kc), lambda i: (i, 0, 0, 0)),
                   pl.BlockSpec((2 * n_qi, tm), col),
                   pl.BlockSpec((tm, 2 * IDX_DIM), row),
                   pl.BlockSpec((IDX_HEADS, tm), col)],
        out_shape=[jax.ShapeDtypeStruct((d_ssm, n // LANES, LANES), F32),
                   jax.ShapeDtypeStruct((2 * d_attn, n), BF16),
                   jax.ShapeDtypeStruct((n, d_attn), BF16),
                   jax.ShapeDtypeStruct((n // tm * n_heads, LANES), F32),
                   jax.ShapeDtypeStruct((n // kc, n_heads, dh + BF16_SUBLANES, kc), BF16),
                   jax.ShapeDtypeStruct((2 * n_qi, n), BF16),
                   jax.ShapeDtypeStruct((n, 2 * IDX_DIM), BF16),
                   jax.ShapeDtypeStruct((IDX_HEADS, n), F32)],
        compiler_params=_params("parallel"),
        name="in_proj",
    )(x2, g.reshape(1, d), sh.reshape(bsz, 1, d), sc.reshape(bsz, 1, d), w, pos_row, inv,
      q_norm.reshape(dh, 1), k_norm.reshape(dh, 1))


def _cmul(ar, ai, br, bi):
    return ar * br - ai * bi, ar * bi + ai * br


def _s5_kernel(chunks_per_batch,
               ut_ref, ldt_ref, ar_row_ref, ai_row_ref, ar_col_ref, ai_col_ref,
               btr_ref, bti_ref, cr_ref, ci_ref, ctr_ref, cti_ref, d_ref,
               y_ref, k2_scr, tm_scr, w_scr, v_scr, yi_scr):
    nc, p, t = btr_ref.shape[1], btr_ref.shape[2], SSM_CHUNK
    dt = jnp.exp(ldt_ref[0])
    lr, li = ar_row_ref[0], ai_row_ref[0]
    mag = jnp.exp(lr * dt)
    lbr, lbi = mag * jnp.cos(li * dt), mag * jnp.sin(li * dt)
    den = lr * lr + li * li
    nr, ni = lbr - 1.0, lbi
    coef_r, coef_i = (nr * lr + ni * li) / den, (ni * lr - nr * li) / den
    bbr, bbi = _cmul(coef_r, coef_i, btr_ref[0], bti_ref[0])

    def powers(rate, freq, n):
        m = jnp.exp(rate * n)
        return m * jnp.cos(freq * n), m * jnp.sin(freq * n)

    rate_c, freq_c = ar_col_ref[0] * dt, ai_col_ref[0] * dt
    rate_r, freq_r = lr * dt, li * dt
    tau = lax.broadcasted_iota(jnp.int32, (1, t), 1).astype(F32)
    pr0, pi0 = powers(rate_c, freq_c, tau)
    pr1, pi1 = _cmul(pr0, pi0, *powers(rate_c, freq_c, 1.0))
    back = (t - 1) - lax.broadcasted_iota(jnp.int32, (t, 1), 0)
    qr, qi = powers(rate_r, freq_r, back.astype(F32))

    cr, ci = cr_ref[0], ci_ref[0]
    rows = []
    for c in range(nc):
        gr, gi = _cmul(cr[c:c + 1], ci[c:c + 1], bbr, bbi)
        rows.append(jnp.concatenate([gr, -gi], axis=1))
    k2_scr[...] = jnp.dot(jnp.concatenate(rows, axis=0), jnp.concatenate([pr0, pi0], axis=0),
                          precision=HIGHEST, preferred_element_type=F32)

    causal = (lax.broadcasted_iota(jnp.int32, (t, t), 1) >= lax.broadcasted_iota(jnp.int32, (t, t), 0))

    width = tm_scr.shape[2] // t
    n_blocks = nc // width

    def build_block(q):
        for e in range(width):
            c = q * width + e
            for cp in range(nc):
                krow = jnp.broadcast_to(k2_scr[pl.ds(c * nc + cp, 1), :], (t, t))
                tile = pltpu.roll(krow, 0, 1, stride=1, stride_axis=0)
                tm_scr[q, cp * t:(cp + 1) * t, e * t:(e + 1) * t] = jnp.where(causal, tile, 0.0).astype(BF16)

    a = jnp.concatenate([ut_ref[c] for c in range(nc)], axis=1).astype(BF16)

    q2r, q2i = jnp.concatenate([qr, qr], axis=1), jnp.concatenate([qi, qi], axis=1)
    b_a = jnp.concatenate([bbr, bbi], axis=1)
    b_b = jnp.concatenate([-bbi, bbr], axis=1)
    for c in range(nc):
        w_scr[c * t:(c + 1) * t, :] = (q2r * b_a[c:c + 1] + q2i * b_b[c:c + 1]).astype(BF16)
    s = jnp.dot(a, w_scr[...], preferred_element_type=F32)

    r = s.shape[0]
    kidx = lax.broadcasted_iota(jnp.int32, (r, 1), 0) % chunks_per_batch

    def times_lam_pow(z, n):
        zr, zi = powers(rate_r, freq_r, float(n))
        return (jnp.concatenate([zr, zr], axis=1) * z
                + jnp.concatenate([-zi, zi], axis=1) * pltpu.roll(z, p, 1))

    step = 1
    while step < chunks_per_batch:
        prev = pltpu.roll(s, step, 0)
        s = s + jnp.where(kidx >= step, times_lam_pow(prev, t * step), 0.0)
        step *= 2
    e = jnp.where(kidx >= 1, pltpu.roll(s, 1, 0), 0.0)

    ctr, cti = ctr_ref[0], cti_ref[0]
    p2r, p2i = jnp.concatenate([pr1, pr1], axis=0), jnp.concatenate([pi1, pi1], axis=0)
    c_a = jnp.concatenate([ctr, -cti], axis=0)
    c_b = jnp.concatenate([-cti, -ctr], axis=0)
    for c in range(nc):
        v_scr[:, c * t:(c + 1) * t] = (p2r * c_a[:, c:c + 1] + p2i * c_b[:, c:c + 1]).astype(BF16)
    y_inter = jnp.dot(e.astype(BF16), v_scr[...], preferred_element_type=F32)
    for q in range(n_blocks):
        yi_scr[q] = y_inter[:, q * width * t:(q + 1) * width * t]

    def emit_block(q):
        y = jnp.dot(a, tm_scr[q], preferred_element_type=F32) + yi_scr[q]
        for e in range(width):
            c = q * width + e
            y_ref[c] = y[:, e * t:(e + 1) * t] + d_ref[c] * ut_ref[c]

    build_block(0)

    def trip(q, carry):
        build_block(q + 1)
        emit_block(q)
        return carry

    lax.fori_loop(0, n_blocks - 1, trip, 0)
    emit_block(n_blocks - 1)


def _s5(ut3, seq, log_dt, a_re, a_im, b_re, b_im, c_re, c_im, d_skip):
    d_ssm, r, t = ut3.shape
    assert t == SSM_CHUNK
    n = r * t
    g, p = a_re.shape
    nc = b_re.shape[2]
    grp = lambda i: (i, 0, 0)
    row_spec = pl.BlockSpec((1, 1, p), grp)
    col_spec = pl.BlockSpec((1, p, 1), grp)
    cp_spec = pl.BlockSpec((1, nc, p), grp)
    pc_spec = pl.BlockSpec((1, p, nc), grp)
    return pl.pallas_call(
        functools.partial(_s5_kernel, seq // t),
        grid=(g,),
        in_specs=[pl.BlockSpec((nc, r, t), grp),
                  pl.BlockSpec((1, 1, 1), grp),
                  row_spec, row_spec, col_spec, col_spec,
                  cp_spec, cp_spec, cp_spec, cp_spec, pc_spec, pc_spec,
                  pl.BlockSpec((nc, 1, 1), grp)],
        out_specs=pl.BlockSpec((nc, r, t), grp),
        out_shape=jax.ShapeDtypeStruct((d_ssm, r, t), F32),
        scratch_shapes=[pltpu.VMEM((nc * nc, t), F32),
                        pltpu.VMEM((nc // TOEPLITZ_BLOCK, nc * t, TOEPLITZ_BLOCK * t), BF16),
                        pltpu.VMEM((nc * t, 2 * p), BF16),
                        pltpu.VMEM((2 * p, nc * t), BF16),
                        pltpu.VMEM((nc // TOEPLITZ_BLOCK, r, TOEPLITZ_BLOCK * t), F32)],
        compiler_params=_params("parallel"),
        name="s5",
    )(ut3, log_dt.reshape(g, 1, 1),
      a_re.reshape(g, 1, p), a_im.reshape(g, 1, p), a_re.reshape(g, p, 1), a_im.reshape(g, p, 1),
      jnp.swapaxes(b_re, 1, 2), jnp.swapaxes(b_im, 1, 2), c_re, c_im,
      jnp.swapaxes(c_re, 1, 2), jnp.swapaxes(c_im, 1, 2), d_skip.reshape(d_ssm, 1, 1))


def _dsa_kernel(topk, seq, tq, kc, dh, bisect_passes,
                qt_ref, qit_ref, wit_ref, k_ref, ksq_ref, ki_ref, vt_ref, o_ref,
                i_scr, m_scr, l_scr, acc_scr):
    i = pl.program_id(1)
    n_heads = acc_scr.shape[0]
    n_chunks = (i * tq + tq + kc - 1) // kc

    def head_rows(ref, h):
        return ref[h * LANES:(h + 1) * LANES, :]

    qpos = i * tq + lax.broadcasted_iota(jnp.int32, (1, tq), 1)
    krow = lax.broadcasted_iota(jnp.int32, (kc, 1), 0)
    wi = wit_ref[...]
    qi_all = jnp.concatenate([head_rows(qit_ref, h) for h in range(IDX_HEADS)], axis=1)

    def chunk_loop(body, carry):
        done = 0
        for width, shift in ((4, 2), (2, 1), (1, 0)):
            trips = (n_chunks - done) >> shift
            carry = lax.fori_loop(
                0, trips, lambda j, cr, w=width, d=done: body(tuple(d + w * j + u for u in range(w)), cr), carry)
            done = done + trips * width
        return carry

    def index_chunks(chunks, carry):
        lo, hi = carry
        starts = [pl.multiple_of(c * kc, kc) for c in chunks]
        dots = [jnp.dot(ki_ref[pl.ds(r0, kc), :], qi_all, preferred_element_type=F32) for r0 in starts]
        for c, r0, s_all in zip(chunks, starts, dots):
            score = jnp.zeros((kc, tq), F32)
            for h in range(IDX_HEADS):
                score = score + wi[h:h + 1, :] * jnp.maximum(s_all[:, h * tq:(h + 1) * tq], 0.0)
            i_scr[c] = jnp.where(r0 + krow <= qpos, score, -jnp.inf)
            lo = jnp.minimum(lo, jnp.min(score, axis=0, keepdims=True))
            hi = jnp.maximum(hi, jnp.max(score, axis=0, keepdims=True))
        return lo, hi

    lo, hi = chunk_loop(index_chunks, (jnp.full((1, tq), jnp.inf, F32), jnp.full((1, tq), -jnp.inf, F32)))

    def over_chunks(per_chunk, combine, init):
        def body(chunks, acc):
            for c in chunks:
                acc = combine(acc, per_chunk(i_scr[c]).reshape(kc // PART_ROWS, PART_ROWS, tq))
            return acc
        return chunk_loop(body, jnp.full((PART_ROWS, tq), init, F32))

    def count(pred):
        parts = over_chunks(lambda x: jnp.where(pred(x), 1.0, 0.0),
                            lambda acc, part: acc + jnp.sum(part, axis=0), 0.0)
        return jnp.sum(parts, axis=0, keepdims=True)

    def max_below(bound):
        parts = over_chunks(lambda x: jnp.where(x < bound, x, -jnp.inf),
                            lambda acc, part: jnp.maximum(acc, jnp.max(part, axis=0)), -jnp.inf)
        return jnp.max(parts, axis=0, keepdims=True)

    kf = float(topk)
    n_causal = (qpos + 1).astype(F32)
    hi = hi + jnp.abs(hi) * 2.0 ** -20 + 1e-30

    def bisect(_, st):
        lo, hi, c_lo, active = st
        mid = 0.5 * lo + 0.5 * hi
        c = count(lambda x: x >= mid)
        upd = jnp.logical_and(active > 0.0, jnp.logical_and(mid > lo, mid < hi))
        up_lo = jnp.logical_and(upd, c >= kf)
        up_hi = jnp.logical_and(upd, c < kf)
        lo = jnp.where(up_lo, mid, lo)
        c_lo = jnp.where(up_lo, c, c_lo)
        hi = jnp.where(up_hi, mid, hi)
        return lo, hi, c_lo, jnp.where(c_lo > kf, active, 0.0)

    active0 = jnp.where(n_causal > kf, 1.0, 0.0)
    lo, hi, c_lo, active = lax.fori_loop(0, bisect_passes, bisect, (lo, hi, n_causal, active0))

    def stepping(st):
        return jnp.max(st[3]) > 0.0

    def step_down(st):
        lo, hi, c_lo, active = st
        v = max_below(hi)
        c = count(lambda x: x >= v)
        on = active > 0.0
        found = jnp.logical_and(on, c >= kf)
        lo = jnp.where(found, v, lo)
        c_lo = jnp.where(found, c, c_lo)
        hi = jnp.where(jnp.logical_and(on, c < kf), v, hi)
        return lo, hi, c_lo, jnp.where(found, 0.0, active)

    thr, _, c_thr, _ = lax.while_loop(stepping, step_down, (lo, hi, c_lo, active))
    tied = jnp.max(jnp.where(c_thr > kf, 1.0, 0.0)) > 0.0

    qsq = jnp.zeros((1, tq), F32)
    for h in range(n_heads):
        qh = head_rows(qt_ref, h).astype(F32)
        qsq = jnp.maximum(qsq, jnp.sum(qh * qh, axis=0, keepdims=True))
    ksq = jnp.max(jnp.max(ksq_ref[...], axis=0, keepdims=True), axis=1, keepdims=True)
    shift = -1.02 * jnp.sqrt(qsq * ksq)

    @pl.when(jnp.logical_not(tied))
    def _():
        def mask_chunk(c, carry):
            i_scr[c] = jnp.where(i_scr[c] >= thr, shift, NEG_BIG)
            return carry
        lax.fori_loop(0, n_chunks, mask_chunk, 0)

    @pl.when(tied)
    def _():
        above = count(lambda x: x > thr)
        room = jnp.where(c_thr > kf, kf - above, float(seq))
        tri = (lax.broadcasted_iota(jnp.int32, (kc, kc), 0)
               >= lax.broadcasted_iota(jnp.int32, (kc, kc), 1)).astype(BF16)

        def mask_chunk(c, seen):
            x = i_scr[c]
            eq = jnp.where(x == thr, 1.0, 0.0)
            rank = jnp.dot(tri, eq.astype(BF16), preferred_element_type=F32) + seen
            keep = jnp.logical_or(x > thr, jnp.logical_and(x == thr, rank <= room))
            i_scr[c] = jnp.where(keep, shift, NEG_BIG)
            return seen + jnp.sum(eq, axis=0, keepdims=True)
        lax.fori_loop(0, n_chunks, mask_chunk, jnp.zeros((1, tq), F32))

    def logits(c, h):
        r0 = pl.multiple_of(c * kc, kc)
        kk = k_ref[pl.ds(r0, kc), (h // 2) * LANES:(h // 2 + 1) * LANES]
        return jnp.dot(kk, head_rows(qt_ref, h), preferred_element_type=F32) + i_scr[c]

    acc_scr[...] = jnp.zeros(acc_scr.shape, F32)

    def attend_chunks(chunks, carry):
        ahead = 4
        items = [(c, h) for c in chunks for h in range(n_heads)]
        probs = [jnp.exp2(logits(c, h)).astype(BF16) for c, h in items[:ahead]]
        for n, (c, h) in enumerate(items):
            if n + ahead < len(items):
                probs.append(jnp.exp2(logits(*items[n + ahead])).astype(BF16))
            acc_scr[h] += jnp.dot(vt_ref[c, h], probs[n], preferred_element_type=F32)
        return carry

    chunk_loop(attend_chunks, 0)

    l_min = jnp.full((1, tq), jnp.inf, F32)
    for h in range(n_heads):
        denom = acc_scr[h, dh:dh + 1, :]
        l_min = jnp.minimum(l_min, denom)
        o_ref[h * dh:(h + 1) * dh, :] = acc_scr[h, 0:dh, :] / denom

    @pl.when(jnp.min(l_min) < 2.0 ** -64)
    def _():
        m_scr[...] = jnp.full(m_scr.shape, NEG_BIG, F32)
        l_scr[...] = jnp.zeros(l_scr.shape, F32)
        acc_scr[...] = jnp.zeros(acc_scr.shape, F32)

        def attend_chunk_online(c, carry):
            for h in range(n_heads):
                s = logits(c, h)
                m_old = m_scr[h:h + 1, :]
                m_new = jnp.maximum(m_old, jnp.max(s, axis=0, keepdims=True))
                alpha = jnp.exp2(m_old - m_new)
                p = jnp.exp2(s - m_new)
                l_scr[h:h + 1, :] = alpha * l_scr[h:h + 1, :] + jnp.sum(p, axis=0, keepdims=True)
                m_scr[h:h + 1, :] = m_new
                pv = jnp.dot(vt_ref[c, h, 0:dh, :], p.astype(BF16), preferred_element_type=F32)
                acc_scr[h, 0:dh, :] = alpha * acc_scr[h, 0:dh, :] + pv
            return carry

        lax.fori_loop(0, n_chunks, attend_chunk_online, 0)
        for h in range(n_heads):
            o_ref[h * dh:(h + 1) * dh, :] = acc_scr[h, 0:dh, :] / l_scr[h:h + 1, :]


def _dsa(qt, qit, wit, k, ksq, ki, vt4, seq, tq, dh):
    n, d_attn = k.shape
    bsz = n // seq
    kc = vt4.shape[3]
    n_heads = d_attn // dh
    n_qi = IDX_HEADS * IDX_DIM
    assert 2 * dh == LANES and 2 * IDX_DIM == LANES and n_heads % 2 == 0 and IDX_HEADS % 2 == 0
    assert kc % PART_ROWS == 0
    topk = min(TOPK_MAX, seq // 4)
    blocks_per_batch = seq // tq
    chunks_per_batch = seq // kc
    bisect_passes = 14
    qcol = lambda b, i: (0, b * blocks_per_batch + i)
    per_batch = lambda b, i: (b, 0)
    return pl.pallas_call(
        functools.partial(_dsa_kernel, topk, seq, tq, kc, dh, bisect_passes),
        grid=(bsz, blocks_per_batch),
        in_specs=[pl.BlockSpec((2 * d_attn, tq), qcol),
                  pl.BlockSpec((2 * n_qi, tq), qcol),
                  pl.BlockSpec((IDX_HEADS, tq), qcol),
                  pl.BlockSpec((seq, d_attn), per_batch),
                  pl.BlockSpec((ksq.shape[0] // bsz, LANES), per_batch),
                  pl.BlockSpec((seq, 2 * IDX_DIM), per_batch),
                  pl.BlockSpec((chunks_per_batch, n_heads, dh + BF16_SUBLANES, kc), lambda b, i: (b, 0, 0, 0))],
        out_specs=pl.BlockSpec((d_attn, tq), qcol),
        out_shape=jax.ShapeDtypeStruct((d_attn, n), F32),
        scratch_shapes=[pltpu.VMEM((chunks_per_batch, kc, tq), F32),
                        pltpu.VMEM((n_heads, tq), F32),
                        pltpu.VMEM((n_heads, tq), F32),
                        pltpu.VMEM((n_heads, dh + BF16_SUBLANES, tq), F32)],
        compiler_params=_params("parallel", "arbitrary"),
        name="dsa",
    )(qt, qit, wit, k, ksq, ki, vt4)


def _mix_out_kernel(yt_ref, yat_ref, x_ref, g1_ref, sh_ref, sc_ref, gs_ref, ga_ref, wglu_ref, bglu_ref,
                    wo_ref, gf_ref, h_ref, hn_ref):
    def norm_cols(v, g):
        return v * lax.rsqrt(jnp.mean(v * v, axis=0, keepdims=True) + EPS) * g

    d_ssm, chunks, steps = yt_ref.shape
    y = _gelu(yt_ref[...].reshape(d_ssm, chunks * steps))
    z = jnp.dot(wglu_ref[...], y.astype(BF16), preferred_element_type=F32) + bglu_ref[...]
    y = y * jax.nn.sigmoid(z)
    mix_t = jnp.concatenate([norm_cols(y, gs_ref[...]), norm_cols(yat_ref[...], ga_ref[...])], axis=0)
    mix = jnp.dot(mix_t.T.astype(BF16), wo_ref[...], preferred_element_type=F32)
    h = x_ref[...] + g1_ref[0] * mix
    h_ref[...] = h
    hn_ref[...] = (_rmsnorm_rows(h, gf_ref[...]) * (1.0 + sc_ref[0]) + sh_ref[0]).astype(BF16)


def _mix_out(yt3, yat, x2, g1, sh2, sc2, gs, ga, w_glu, b_glu, w_out, gf, seq, tm):
    d_ssm, _, steps = yt3.shape
    n, d = x2.shape
    d_attn = yat.shape[0]
    bsz = n // seq
    tiles_per_batch = seq // tm
    row = lambda i: (i, 0)
    col = lambda i: (0, i)
    const = lambda i: (0, 0)
    per_batch = lambda i: (i // tiles_per_batch, 0, 0)
    mod_spec = pl.BlockSpec((1, 1, d), per_batch)
    return pl.pallas_call(
        _mix_out_kernel,
        grid=(n // tm,),
        in_specs=[pl.BlockSpec((d_ssm, tm // steps, steps), lambda i: (0, i, 0)),
                  pl.BlockSpec((d_attn, tm), col),
                  pl.BlockSpec((tm, d), row),
                  mod_spec, mod_spec, mod_spec,
                  pl.BlockSpec((d_ssm, 1), const),
                  pl.BlockSpec((d_attn, 1), const),
                  pl.BlockSpec((d_ssm, d_ssm), const),
                  pl.BlockSpec((d_ssm, 1), const),
                  pl.BlockSpec((d_ssm + d_attn, d), const),
                  pl.BlockSpec((1, d), const)],
        out_specs=[pl.BlockSpec((tm, d), row), pl.BlockSpec((tm, d), row)],
        out_shape=[jax.ShapeDtypeStruct((n, d), F32), jax.ShapeDtypeStruct((n, d), BF16)],
        compiler_params=_params("parallel"),
        name="mix_out",
    )(yt3, yat, x2, g1.reshape(bsz, 1, d), sh2.reshape(bsz, 1, d), sc2.reshape(bsz, 1, d),
      gs.reshape(d_ssm, 1), ga.reshape(d_attn, 1), w_glu.T.astype(BF16), b_glu.reshape(d_ssm, 1),
      w_out.astype(BF16), gf.reshape(1, d))


def _ffn_kernel(tiles_per_batch, tf,
                hn_ref, halo_ref, h_ref, g2_ref, wu_ref, cw_ref, cb_ref, wd_ref,
                o_ref, lhs_scr, act_scr):
    i = pl.program_id(0)
    halo_rows = halo_ref.shape[0]
    lhs_scr[0:halo_rows, :] = jnp.where(i % tiles_per_batch == 0, 0, halo_ref[...]).astype(BF16)
    lhs_scr[halo_rows:, :] = hn_ref[...]

    d_ff = act_scr.shape[1]

    def conv_up(c0):
        up = jnp.dot(lhs_scr[...], wu_ref[:, c0:c0 + tf], preferred_element_type=F32)
        conv = cb_ref[:, c0:c0 + tf] + cw_ref[CONV_W - 1:CONV_W, c0:c0 + tf] * up
        for back in range(1, CONV_W):
            tap = CONV_W - 1 - back
            conv = conv + cw_ref[tap:tap + 1, c0:c0 + tf] * pltpu.roll(up, back, 0)
        return conv[halo_rows:]

    for c0 in range(0, d_ff, tf):
        act_scr[:, c0:c0 + tf] = (_gelu(conv_up(c0)) * conv_up(d_ff + c0)).astype(BF16)

    down = jnp.dot(act_scr[...], wd_ref[...], preferred_element_type=F32)
    o_ref[...] = h_ref[...] + g2_ref[0] * down


def _ffn(hn2, h1, g2, w_up, conv_w, conv_b, w_down, seq, tm, tf):
    n, d = h1.shape
    d_ff = w_down.shape[0]
    bsz = n // seq
    tiles_per_batch = seq // tm
    halo_rows = BF16_SUBLANES
    assert halo_rows >= CONV_W - 1 and tm % halo_rows == 0 and d_ff % tf == 0
    halo_per_tile = tm // halo_rows
    row = lambda i: (i, 0)
    const = lambda i: (0, 0)
    resident = pl.Buffered(1)

    return pl.pallas_call(
        functools.partial(_ffn_kernel, tiles_per_batch, tf),
        grid=(n // tm,),
        in_specs=[pl.BlockSpec((tm, d), row),
                  pl.BlockSpec((halo_rows, d), lambda i: (jnp.maximum(i * halo_per_tile - 1, 0), 0)),
                  pl.BlockSpec((tm, d), row),
                  pl.BlockSpec((1, 1, d), lambda i: (i // tiles_per_batch, 0, 0)),
                  pl.BlockSpec((d, 2 * d_ff), const, pipeline_mode=resident),
                  pl.BlockSpec((CONV_W, 2 * d_ff), const, pipeline_mode=resident),
                  pl.BlockSpec((1, 2 * d_ff), const, pipeline_mode=resident),
                  pl.BlockSpec((d_ff, d), const, pipeline_mode=resident)],
        out_specs=pl.BlockSpec((tm, d), row),
        out_shape=jax.ShapeDtypeStruct((n, d), F32),
        scratch_shapes=[pltpu.VMEM((halo_rows + tm, d), BF16),
                        pltpu.VMEM((tm, d_ff), BF16)],
        compiler_params=_params("parallel"),
        name="ffn",
    )(hn2, hn2, h1, g2.reshape(bsz, 1, d), w_up.astype(BF16), conv_w, conv_b.reshape(1, 2 * d_ff),
      w_down.astype(BF16))


def _tile(total, want):
    t = min(total, want)
    assert total % t == 0
    return t


def kernel(x, c, positions, w_ada, b_ada, norm_mix, w_in, ssm_log_dt, ssm_a_re, ssm_a_im, ssm_b_re, ssm_b_im, ssm_c_re, ssm_c_im, ssm_d, ssm_w_glu, ssm_b_glu, q_norm, k_norm, out_norm_ssm, out_norm_attn, w_out, norm_ffn, w_up, conv_w, conv_b, w_down):
    bsz, seq, d = x.shape
    depth = w_ada.shape[0]
    n = bsz * seq
    d_ssm = out_norm_ssm.shape[1]
    d_attn = out_norm_attn.shape[1]
    dh = q_norm.shape[1]
    d_ff = w_down.shape[1]
    assert w_in.shape[2] == d_ssm + 3 * d_attn + IDX_HEADS * IDX_DIM + IDX_DIM + IDX_HEADS
    assert seq % SSM_CHUNK == 0

    t_proj = _tile(seq, 1024)
    t_key = _tile(seq, 512)
    t_q = _tile(seq, 256)
    t_mix = _tile(seq, 1024)
    t_ffn = _tile(seq, 512)
    t_ff = _tile(d_ff, 256)

    h = x.astype(F32).reshape(n, d)
    pos_row = positions.reshape(1, n)
    for l in range(depth):
        mod = _adaln(c.astype(F32), w_ada[l].astype(F32), b_ada[l].astype(F32))
        sh1, sc1, g1, sh2, sc2, g2 = jnp.split(mod, 6, axis=-1)
        ut, qt, k, ksq, vt4, qit, ki, wit = _in_proj(h, norm_mix[l], sh1, sc1, w_in[l], pos_row, q_norm[l],
                                                     k_norm[l], seq, t_proj, t_key, d_ssm, d_attn, dh)
        yt = _s5(ut, seq, ssm_log_dt[l], ssm_a_re[l], ssm_a_im[l], ssm_b_re[l], ssm_b_im[l],
                 ssm_c_re[l], ssm_c_im[l], ssm_d[l])
        yat = _dsa(qt, qit, wit, k, ksq, ki, vt4, seq, t_q, dh)
        h1, hn2 = _mix_out(yt, yat, h, g1, sh2, sc2, out_norm_ssm[l], out_norm_attn[l], ssm_w_glu[l],
                           ssm_b_glu[l], w_out[l], norm_ffn[l], seq, t_mix)
        h = _ffn(hn2, h1, g2, w_up[l], conv_w[l], conv_b[l], w_down[l], seq, t_ffn, t_ff)
    return h.reshape(bsz, seq, d).astype(x.dtype)
```

```python
import functools
import math

import jax
import jax.numpy as jnp
from jax import lax
from jax.experimental import pallas as pl
from jax.experimental.pallas import tpu as pltpu

EPS = 1e-6
ROPE_THETA = 10000.0
IDX_HEADS = 8
IDX_DIM = 64
TOPK_MAX = 256
CONV_W = 3

LANES = 128
BF16_SUBLANES = 16
SSM_CHUNK = LANES
TOEPLITZ_BLOCK = 2
PART_ROWS = 32
VMEM_LIMIT = 56 * 1024 * 1024
NEG_BIG = -1e30
LOG2E = math.log2(math.e)

F32 = jnp.float32
BF16 = jnp.bfloat16
HIGHEST = lax.Precision.HIGHEST


def _params(*semantics):
    return pltpu.CompilerParams(dimension_semantics=semantics, vmem_limit_bytes=VMEM_LIMIT)


def _gelu(x):
    return 0.5 * x * (1.0 + jnp.tanh(math.sqrt(2.0 / math.pi) * (x + 0.044715 * (x * x * x))))


def _adaln_kernel(ct_ref, w_ref, b_ref, o_ref):
    ct = ct_ref[...]
    cs = ct * jax.nn.sigmoid(ct)
    w = w_ref[...]
    for b in range(ct.shape[1]):
        o_ref[b:b + 1, :] = jnp.sum(w * cs[:, b:b + 1], axis=0, keepdims=True) + b_ref[...]


def _adaln(c, w, bias):
    bsz, d = c.shape
    n = w.shape[1]
    tn = 1024 if n % 1024 == 0 else n
    return pl.pallas_call(
        _adaln_kernel,
        grid=(n // tn,),
        in_specs=[pl.BlockSpec((d, bsz), lambda j: (0, 0)),
                  pl.BlockSpec((d, tn), lambda j: (0, j)),
                  pl.BlockSpec((1, tn), lambda j: (0, j))],
        out_specs=pl.BlockSpec((bsz, tn), lambda j: (0, j)),
        out_shape=jax.ShapeDtypeStruct((bsz, n), F32),
        compiler_params=_params("parallel"),
        name="adaln",
    )(c.T, w, bias.reshape(1, n))


def _rmsnorm_rows(x, g):
    return x * lax.rsqrt(jnp.mean(x * x, axis=-1, keepdims=True) + EPS) * g


def _rope_t(x3, cos, sin):
    half = x3.shape[1] // 2
    x1, x2 = x3[:, :half, :], x3[:, half:, :]
    return jnp.concatenate([x1 * cos - x2 * sin, x1 * sin + x2 * cos], axis=1)


def _pair_rows(x3):
    heads, dh, t = x3.shape
    zero = jnp.zeros_like(x3)
    even = lax.broadcasted_iota(jnp.int32, (heads, 1, 1), 0) % 2 == 0
    return jnp.concatenate([jnp.where(even, x3, zero), jnp.where(even, zero, x3)], axis=1).reshape(heads * 2 * dh, t)


def _in_proj_kernel(d_ssm, d_attn, dh, wi_scale,
                    x_ref, g_ref, sh_ref, sc_ref, w_ref, pos_ref, inv_ref, qg_ref, kg_ref,
                    ut_ref, qq_ref, k_ref, ksq_ref, vt_ref, ki_ref, wit_ref):
    hn = _rmsnorm_rows(x_ref[...], g_ref[...]) * (1.0 + sc_ref[0]) + sh_ref[0]
    hb = hn.astype(BF16)
    tm = hb.shape[0]

    def proj_t(r0, rows):
        return lax.dot_general(w_ref[:, r0:r0 + rows], hb, (((0,), (1,)), ((), ())),
                               preferred_element_type=F32)

    ang = inv_ref[...] * pos_ref[...].astype(F32)
    cos, sin = jnp.cos(ang)[None], jnp.sin(ang)[None]

    def head_norm(x3, gcol):
        return x3 * lax.rsqrt(jnp.mean(x3 * x3, axis=1, keepdims=True) + EPS) * gcol[None]

    ut_ref[...] = proj_t(0, d_ssm).reshape(d_ssm, tm // LANES, LANES)
    o = d_ssm
    n_heads = d_attn // dh
    q3 = proj_t(o, d_attn).reshape(n_heads, dh, tm)
    q3 = _rope_t(head_norm(q3, qg_ref[...]), cos, sin) * (dh ** -0.5 * LOG2E)
    qq_ref[0:2 * d_attn, :] = _pair_rows(q3).astype(BF16)
    o += d_attn
    k3 = _rope_t(head_norm(proj_t(o, d_attn).reshape(n_heads, dh, tm), kg_ref[...]), cos, sin)
    k_ref[...] = k3.reshape(d_attn, tm).T.astype(BF16)
    ksq = jnp.max(jnp.sum(k3 * k3, axis=1), axis=1, keepdims=True)
    ksq_ref[...] = jnp.broadcast_to(ksq, ksq_ref.shape)
    o += d_attn
    v3 = proj_t(o, d_attn).reshape(n_heads, dh, tm)
    vt = jnp.concatenate([v3, jnp.ones((n_heads, BF16_SUBLANES, tm), F32)], axis=1).astype(BF16)
    kc = vt_ref.shape[3]
    for j in range(tm // kc):
        vt_ref[j] = vt[:, :, j * kc:(j + 1) * kc]
    o += d_attn
    n_qi = IDX_HEADS * IDX_DIM
    qi3 = _rope_t(proj_t(o, n_qi).reshape(IDX_HEADS, IDX_DIM, tm), cos, sin)
    qq_ref[2 * d_attn:, :] = _pair_rows(qi3).astype(BF16)
    o += n_qi
    tail = proj_t(o, IDX_DIM + BF16_SUBLANES)
    ki = _rope_t(tail[None, :IDX_DIM], cos, sin)[0]
    ki_ref[...] = jnp.concatenate([ki, ki], axis=0).T.astype(BF16)
    wit_ref[...] = tail[IDX_DIM:IDX_DIM + IDX_HEADS] * wi_scale


def _in_proj(x2, g, sh, sc, w_in, pos_row, q_norm, k_norm, seq, tm, kc, d_ssm, d_attn, dh):
    n, d = x2.shape
    bsz = n // seq
    tiles_per_batch = seq // tm
    n_qi = IDX_HEADS * IDX_DIM
    n_heads = d_attn // dh
    w = jnp.concatenate([w_in.astype(BF16), jnp.zeros((d, BF16_SUBLANES - IDX_HEADS), BF16)], axis=1)
    rt = w.shape[1]
    inv = (ROPE_THETA ** (-jnp.arange(0, dh, 2, dtype=F32) / dh)).reshape(dh // 2, 1)
    wi_scale = IDX_HEADS ** -0.5 * IDX_DIM ** -0.5

    row = lambda i: (i, 0)
    col = lambda i: (0, i)
    const = lambda i: (0, 0)
    per_batch = lambda i: (i // tiles_per_batch, 0, 0)
    return pl.pallas_call(
        functools.partial(_in_proj_kernel, d_ssm, d_attn, dh, wi_scale),
        grid=(n // tm,),
        in_specs=[pl.BlockSpec((tm, d), row),
                  pl.BlockSpec((1, d), const),
                  pl.BlockSpec((1, 1, d), per_batch),
                  pl.BlockSpec((1, 1, d), per_batch),
                  pl.BlockSpec((d, rt), const),
                  pl.BlockSpec((1, tm), col),
                  pl.BlockSpec((dh // 2, 1), const),
                  pl.BlockSpec((dh, 1), const),
                  pl.BlockSpec((dh, 1), const)],
        out_specs=[pl.BlockSpec((d_ssm, tm // LANES, LANES), lambda i: (0, i, 0)),
                   pl.BlockSpec((2 * d_attn + 2 * n_qi, tm), col),
                   pl.BlockSpec((tm, d_attn), row),
                   pl.BlockSpec((n_heads, LANES), row),
                   pl.BlockSpec((tm // kc, n_heads, dh + BF16_SUBLANES, kc), lambda i: (i, 0, 0, 0)),
                   pl.BlockSpec((tm, 2 * IDX_DIM), row),
                   pl.BlockSpec((IDX_HEADS, tm), col)],
        out_shape=[jax.ShapeDtypeStruct((d_ssm, n // LANES, LANES), F32),
                   jax.ShapeDtypeStruct((2 * d_attn + 2 * n_qi, n), BF16),
                   jax.ShapeDtypeStruct((n, d_attn), BF16),
                   jax.ShapeDtypeStruct((n // tm * n_heads, LANES), F32),
                   jax.ShapeDtypeStruct((n // kc, n_heads, dh + BF16_SUBLANES, kc), BF16),
                   jax.ShapeDtypeStruct((n, 2 * IDX_DIM), BF16),
                   jax.ShapeDtypeStruct((IDX_HEADS, n), F32)],
        compiler_params=_params("parallel"),
        name="in_proj",
    )(x2, g.reshape(1, d), sh.reshape(bsz, 1, d), sc.reshape(bsz, 1, d), w, pos_row, inv,
      q_norm.reshape(dh, 1), k_norm.reshape(dh, 1))


def _cmul(ar, ai, br, bi):
    return ar * br - ai * bi, ar * bi + ai * br


def _s5_kernel(chunks_per_batch,
               ut_ref, ldt_ref, ar_row_ref, ai_row_ref, ar_col_ref, ai_col_ref,
               btr_ref, bti_ref, cr_ref, ci_ref, ctr_ref, cti_ref, d_ref,
               y_ref, k2_scr, tm_scr, w_scr, v_scr, yi_scr):
    nc, p, t = btr_ref.shape[1], btr_ref.shape[2], SSM_CHUNK
    dt = jnp.exp(ldt_ref[0])
    lr, li = ar_row_ref[0], ai_row_ref[0]
    mag = jnp.exp(lr * dt)
    lbr, lbi = mag * jnp.cos(li * dt), mag * jnp.sin(li * dt)
    den = lr * lr + li * li
    nr, ni = lbr - 1.0, lbi
    coef_r, coef_i = (nr * lr + ni * li) / den, (ni * lr - nr * li) / den
    bbr, bbi = _cmul(coef_r, coef_i, btr_ref[0], bti_ref[0])

    def powers(rate, freq, n):
        m = jnp.exp(rate * n)
        return m * jnp.cos(freq * n), m * jnp.sin(freq * n)

    rate_c, freq_c = ar_col_ref[0] * dt, ai_col_ref[0] * dt
    rate_r, freq_r = lr * dt, li * dt
    tau = lax.broadcasted_iota(jnp.int32, (1, t), 1).astype(F32)
    pr0, pi0 = powers(rate_c, freq_c, tau)
    pr1, pi1 = _cmul(pr0, pi0, *powers(rate_c, freq_c, 1.0))
    back = (t - 1) - lax.broadcasted_iota(jnp.int32, (t, 1), 0)
    qr, qi = powers(rate_r, freq_r, back.astype(F32))

    cr, ci = cr_ref[0], ci_ref[0]
    rows = []
    for c in range(nc):
        gr, gi = _cmul(cr[c:c + 1], ci[c:c + 1], bbr, bbi)
        rows.append(jnp.concatenate([gr, -gi], axis=1))
    k2_scr[...] = jnp.dot(jnp.concatenate(rows, axis=0), jnp.concatenate([pr0, pi0], axis=0),
                          precision=HIGHEST, preferred_element_type=F32)

    causal = (lax.broadcasted_iota(jnp.int32, (t, t), 1) >= lax.broadcasted_iota(jnp.int32, (t, t), 0))

    width = tm_scr.shape[2] // t
    n_blocks = nc // width

    def build_block(q):
        for e in range(width):
            c = q * width + e
            for cp in range(nc):
                krow = jnp.broadcast_to(k2_scr[pl.ds(c * nc + cp, 1), :], (t, t))
                tile = pltpu.roll(krow, 0, 1, stride=1, stride_axis=0)
                tm_scr[q, cp * t:(cp + 1) * t, e * t:(e + 1) * t] = jnp.where(causal, tile, 0.0).astype(BF16)

    a = jnp.concatenate([ut_ref[c] for c in range(nc)], axis=1).astype(BF16)

    q2r, q2i = jnp.concatenate([qr, qr], axis=1), jnp.concatenate([qi, qi], axis=1)
    b_a = jnp.concatenate([bbr, bbi], axis=1)
    b_b = jnp.concatenate([-bbi, bbr], axis=1)
    for c in range(nc):
        w_scr[c * t:(c + 1) * t, :] = (q2r * b_a[c:c + 1] + q2i * b_b[c:c + 1]).astype(BF16)
    s = jnp.dot(a, w_scr[...], preferred_element_type=F32)

    r = s.shape[0]
    kidx = lax.broadcasted_iota(jnp.int32, (r, 1), 0) % chunks_per_batch

    def times_lam_pow(z, n):
        zr, zi = powers(rate_r, freq_r, float(n))
        return (jnp.concatenate([zr, zr], axis=1) * z
                + jnp.concatenate([-zi, zi], axis=1) * pltpu.roll(z, p, 1))

    step = 1
    while step < chunks_per_batch:
        prev = pltpu.roll(s, step, 0)
        s = s + jnp.where(kidx >= step, times_lam_pow(prev, t * step), 0.0)
        step *= 2
    e = jnp.where(kidx >= 1, pltpu.roll(s, 1, 0), 0.0)

    ctr, cti = ctr_ref[0], cti_ref[0]
    p2r, p2i = jnp.concatenate([pr1, pr1], axis=0), jnp.concatenate([pi1, pi1], axis=0)
    c_a = jnp.concatenate([ctr, -cti], axis=0)
    c_b = jnp.concatenate([-cti, -ctr], axis=0)
    for c in range(nc):
        v_scr[:, c * t:(c + 1) * t] = (p2r * c_a[:, c:c + 1] + p2i * c_b[:, c:c + 1]).astype(BF16)
    y_inter = jnp.dot(e.astype(BF16), v_scr[...], preferred_element_type=F32)
    for q in range(n_blocks):
        yi_scr[q] = y_inter[:, q * width * t:(q + 1) * width * t]

    def emit_block(q):
        y = jnp.dot(a, tm_scr[q], preferred_element_type=F32) + yi_scr[q]
        for e in range(width):
            c = q * width + e
            y_ref[c] = y[:, e * t:(e + 1) * t] + d_ref[c] * ut_ref[c]

    build_block(0)

    def trip(q, carry):
        build_block(q + 1)
        emit_block(q)
        return carry

    lax.fori_loop(0, n_blocks - 1, trip, 0)
    emit_block(n_blocks - 1)


def _s5(ut3, seq, log_dt, a_re, a_im, b_re, b_im, c_re, c_im, d_skip):
    d_ssm, r, t = ut3.shape
    assert t == SSM_CHUNK
    n = r * t
    g, p = a_re.shape
    nc = b_re.shape[2]
    grp = lambda i: (i, 0, 0)
    row_spec = pl.BlockSpec((1, 1, p), grp)
    col_spec = pl.BlockSpec((1, p, 1), grp)
    cp_spec = pl.BlockSpec((1, nc, p), grp)
    pc_spec = pl.BlockSpec((1, p, nc), grp)
    return pl.pallas_call(
        functools.partial(_s5_kernel, seq // t),
        grid=(g,),
        in_specs=[pl.BlockSpec((nc, r, t), grp),
                  pl.BlockSpec((1, 1, 1), grp),
                  row_spec, row_spec, col_spec, col_spec,
                  cp_spec, cp_spec, cp_spec, cp_spec, pc_spec, pc_spec,
                  pl.BlockSpec((nc, 1, 1), grp)],
        out_specs=pl.BlockSpec((nc, r, t), grp),
        out_shape=jax.ShapeDtypeStruct((d_ssm, r, t), F32),
        scratch_shapes=[pltpu.VMEM((nc * nc, t), F32),
                        pltpu.VMEM((nc // TOEPLITZ_BLOCK, nc * t, TOEPLITZ_BLOCK * t), BF16),
                        pltpu.VMEM((nc * t, 2 * p), BF16),
                        pltpu.VMEM((2 * p, nc * t), BF16),
                        pltpu.VMEM((nc // TOEPLITZ_BLOCK, r, TOEPLITZ_BLOCK * t), F32)],
        compiler_params=_params("parallel"),
        name="s5",
    )(ut3, log_dt.reshape(g, 1, 1),
      a_re.reshape(g, 1, p), a_im.reshape(g, 1, p), a_re.reshape(g, p, 1), a_im.reshape(g, p, 1),
      jnp.swapaxes(b_re, 1, 2), jnp.swapaxes(b_im, 1, 2), c_re, c_im,
      jnp.swapaxes(c_re, 1, 2), jnp.swapaxes(c_im, 1, 2), d_skip.reshape(d_ssm, 1, 1))


def _dsa_kernel(topk, seq, tq, kc, dh, bisect_passes,
                qq_ref, wit_ref, k_ref, ksq_ref, ki_ref, vt_ref, o_ref,
                i_scr, m_scr, l_scr, acc_scr):
    i = pl.program_id(1)
    n_heads = acc_scr.shape[0]
    n_chunks = (i * tq + tq + kc - 1) // kc

    def q_rows(h):
        return qq_ref[h * LANES:(h + 1) * LANES, :]

    def qi_rows(h):
        return q_rows(n_heads + h)

    qpos = i * tq + lax.broadcasted_iota(jnp.int32, (1, tq), 1)
    krow = lax.broadcasted_iota(jnp.int32, (kc, 1), 0)
    wi = wit_ref[...]
    qi_all = jnp.concatenate([qi_rows(h) for h in range(IDX_HEADS)], axis=1)

    def chunk_loop(body, carry):
        done = 0
        for width, shift in ((4, 2), (2, 1), (1, 0)):
            trips = (n_chunks - done) >> shift
            carry = lax.fori_loop(
                0, trips, lambda j, cr, w=width, d=done: body(tuple(d + w * j + u for u in range(w)), cr), carry)
            done = done + trips * width
        return carry

    def index_chunks(chunks, carry):
        lo, hi = carry
        starts = [pl.multiple_of(c * kc, kc) for c in chunks]
        dots = [jnp.dot(ki_ref[pl.ds(r0, kc), :], qi_all, preferred_element_type=F32) for r0 in starts]
        for c, r0, s_all in zip(chunks, starts, dots):
            score = jnp.zeros((kc, tq), F32)
            for h in range(IDX_HEADS):
                score = score + wi[h:h + 1, :] * jnp.maximum(s_all[:, h * tq:(h + 1) * tq], 0.0)
            i_scr[c] = jnp.where(r0 + krow <= qpos, score, -jnp.inf)
            lo = jnp.minimum(lo, jnp.min(score, axis=0, keepdims=True))
            hi = jnp.maximum(hi, jnp.max(score, axis=0, keepdims=True))
        return lo, hi

    lo, hi = chunk_loop(index_chunks, (jnp.full((1, tq), jnp.inf, F32), jnp.full((1, tq), -jnp.inf, F32)))

    def over_chunks(per_chunk, combine, init):
        def body(chunks, acc):
            for c in chunks:
                acc = combine(acc, per_chunk(i_scr[c]).reshape(kc // PART_ROWS, PART_ROWS, tq))
            return acc
        return chunk_loop(body, jnp.full((PART_ROWS, tq), init, F32))

    def count(pred):
        parts = over_chunks(lambda x: jnp.where(pred(x), 1.0, 0.0),
                            lambda acc, part: acc + jnp.sum(part, axis=0), 0.0)
        return jnp.sum(parts, axis=0, keepdims=True)

    def max_below(bound):
        parts = over_chunks(lambda x: jnp.where(x < bound, x, -jnp.inf),
                            lambda acc, part: jnp.maximum(acc, jnp.max(part, axis=0)), -jnp.inf)
        return jnp.max(parts, axis=0, keepdims=True)

    kf = float(topk)
    n_causal = (qpos + 1).astype(F32)
    hi = hi + jnp.abs(hi) * 2.0 ** -20 + 1e-30

    def bisect(_, st):
        lo, hi, c_lo, active = st
        mid = 0.5 * lo + 0.5 * hi
        c = count(lambda x: x >= mid)
        upd = jnp.logical_and(active > 0.0, jnp.logical_and(mid > lo, mid < hi))
        up_lo = jnp.logical_and(upd, c >= kf)
        up_hi = jnp.logical_and(upd, c < kf)
        lo = jnp.where(up_lo, mid, lo)
        c_lo = jnp.where(up_lo, c, c_lo)
        hi = jnp.where(up_hi, mid, hi)
        return lo, hi, c_lo, jnp.where(c_lo > kf, active, 0.0)

    active0 = jnp.where(n_causal > kf, 1.0, 0.0)
    lo, hi, c_lo, active = lax.fori_loop(0, bisect_passes, bisect, (lo, hi, n_causal, active0))

    def stepping(st):
        return jnp.max(st[3]) > 0.0

    def step_down(st):
        lo, hi, c_lo, active = st
        v = max_below(hi)
        c = count(lambda x: x >= v)
        on = active > 0.0
        found = jnp.logical_and(on, c >= kf)
        lo = jnp.where(found, v, lo)
        c_lo = jnp.where(found, c, c_lo)
        hi = jnp.where(jnp.logical_and(on, c < kf), v, hi)
        return lo, hi, c_lo, jnp.where(found, 0.0, active)

    thr, _, c_thr, _ = lax.while_loop(stepping, step_down, (lo, hi, c_lo, active))
    tied = jnp.max(jnp.where(c_thr > kf, 1.0, 0.0)) > 0.0

    qsq = jnp.zeros((1, tq), F32)
    for h in range(n_heads):
        qh = q_rows(h).astype(F32)
        qsq = jnp.maximum(qsq, jnp.sum(qh * qh, axis=0, keepdims=True))
    ksq = jnp.max(jnp.max(ksq_ref[...], axis=0, keepdims=True), axis=1, keepdims=True)
    shift = -1.02 * jnp.sqrt(qsq * ksq)

    @pl.when(jnp.logical_not(tied))
    def _():
        def mask_chunk(c, carry):
            i_scr[c] = jnp.where(i_scr[c] >= thr, shift, NEG_BIG)
            return carry
        lax.fori_loop(0, n_chunks, mask_chunk, 0)

    @pl.when(tied)
    def _():
        above = count(lambda x: x > thr)
        room = jnp.where(c_thr > kf, kf - above, float(seq))
        tri = (lax.broadcasted_iota(jnp.int32, (kc, kc), 0)
               >= lax.broadcasted_iota(jnp.int32, (kc, kc), 1)).astype(BF16)

        def mask_chunk(c, seen):
            x = i_scr[c]
            eq = jnp.where(x == thr, 1.0, 0.0)
            rank = jnp.dot(tri, eq.astype(BF16), preferred_element_type=F32) + seen
            keep = jnp.logical_or(x > thr, jnp.logical_and(x == thr, rank <= room))
            i_scr[c] = jnp.where(keep, shift, NEG_BIG)
            return seen + jnp.sum(eq, axis=0, keepdims=True)
        lax.fori_loop(0, n_chunks, mask_chunk, jnp.zeros((1, tq), F32))

    def logits(c, h):
        r0 = pl.multiple_of(c * kc, kc)
        kk = k_ref[pl.ds(r0, kc), (h // 2) * LANES:(h // 2 + 1) * LANES]
        return jnp.dot(kk, q_rows(h), preferred_element_type=F32) + i_scr[c]

    acc_scr[...] = jnp.zeros(acc_scr.shape, F32)

    def attend_chunks(chunks, carry):
        ahead = 4
        items = [(c, h) for c in chunks for h in range(n_heads)]
        probs = [jnp.exp2(logits(c, h)).astype(BF16) for c, h in items[:ahead]]
        for n, (c, h) in enumerate(items):
            if n + ahead < len(items):
                probs.append(jnp.exp2(logits(*items[n + ahead])).astype(BF16))
            acc_scr[h] += jnp.dot(vt_ref[c, h], probs[n], preferred_element_type=F32)
        return carry

    chunk_loop(attend_chunks, 0)

    l_min = jnp.full((1, tq), jnp.inf, F32)
    for h in range(n_heads):
        denom = acc_scr[h, dh:dh + 1, :]
        l_min = jnp.minimum(l_min, denom)
        o_ref[h * dh:(h + 1) * dh, :] = acc_scr[h, 0:dh, :] / denom

    @pl.when(jnp.min(l_min) < 2.0 ** -64)
    def _():
        m_scr[...] = jnp.full(m_scr.shape, NEG_BIG, F32)
        l_scr[...] = jnp.zeros(l_scr.shape, F32)
        acc_scr[...] = jnp.zeros(acc_scr.shape, F32)

        def attend_chunk_online(c, carry):
            for h in range(n_heads):
                s = logits(c, h)
                m_old = m_scr[h:h + 1, :]
                m_new = jnp.maximum(m_old, jnp.max(s, axis=0, keepdims=True))
                alpha = jnp.exp2(m_old - m_new)
                p = jnp.exp2(s - m_new)
                l_scr[h:h + 1, :] = alpha * l_scr[h:h + 1, :] + jnp.sum(p, axis=0, keepdims=True)
                m_scr[h:h + 1, :] = m_new
                pv = jnp.dot(vt_ref[c, h, 0:dh, :], p.astype(BF16), preferred_element_type=F32)
                acc_scr[h, 0:dh, :] = alpha * acc_scr[h, 0:dh, :] + pv
            return carry

        lax.fori_loop(0, n_chunks, attend_chunk_online, 0)
        for h in range(n_heads):
            o_ref[h * dh:(h + 1) * dh, :] = acc_scr[h, 0:dh, :] / l_scr[h:h + 1, :]


def _dsa(qq, wit, k, ksq, ki, vt4, seq, tq, dh):
    n, d_attn = k.shape
    bsz = n // seq
    kc = vt4.shape[3]
    n_heads = d_attn // dh
    n_qi = IDX_HEADS * IDX_DIM
    assert 2 * dh == LANES and 2 * IDX_DIM == LANES and n_heads % 2 == 0 and IDX_HEADS % 2 == 0
    assert kc % PART_ROWS == 0
    topk = min(TOPK_MAX, seq // 4)
    blocks_per_batch = seq // tq
    chunks_per_batch = seq // kc
    bisect_passes = 14
    qcol = lambda b, i: (0, b * blocks_per_batch + i)
    per_batch = lambda b, i: (b, 0)
    return pl.pallas_call(
        functools.partial(_dsa_kernel, topk, seq, tq, kc, dh, bisect_passes),
        grid=(bsz, blocks_per_batch),
        in_specs=[pl.BlockSpec((2 * d_attn + 2 * n_qi, tq), qcol),
                  pl.BlockSpec((IDX_HEADS, tq), qcol),
                  pl.BlockSpec((seq, d_attn), per_batch),
                  pl.BlockSpec((ksq.shape[0] // bsz, LANES), per_batch),
                  pl.BlockSpec((seq, 2 * IDX_DIM), per_batch),
                  pl.BlockSpec((chunks_per_batch, n_heads, dh + BF16_SUBLANES, kc), lambda b, i: (b, 0, 0, 0))],
        out_specs=pl.BlockSpec((d_attn, tq), qcol),
        out_shape=jax.ShapeDtypeStruct((d_attn, n), F32),
        scratch_shapes=[pltpu.VMEM((chunks_per_batch, kc, tq), F32),
                        pltpu.VMEM((n_heads, tq), F32),
                        pltpu.VMEM((n_heads, tq), F32),
                        pltpu.VMEM((n_heads, dh + BF16_SUBLANES, tq), F32)],
        compiler_params=_params("parallel", "arbitrary"),
        name="dsa",
    )(qq, wit, k, ksq, ki, vt4)


def _mix_out_kernel(yt_ref, yat_ref, x_ref, g1_ref, sh_ref, sc_ref, gs_ref, ga_ref, wglu_ref, bglu_ref,
                    wo_ref, gf_ref, h_ref, hn_ref):
    def norm_cols(v, g):
        return v * lax.rsqrt(jnp.mean(v * v, axis=0, keepdims=True) + EPS) * g

    d_ssm, chunks, steps = yt_ref.shape
    y = _gelu(yt_ref[...].reshape(d_ssm, chunks * steps))
    z = jnp.dot(wglu_ref[...], y.astype(BF16), preferred_element_type=F32) + bglu_ref[...]
    y = y * jax.nn.sigmoid(z)
    mix_t = jnp.concatenate([norm_cols(y, gs_ref[...]), norm_cols(yat_ref[...], ga_ref[...])], axis=0)
    mix = jnp.dot(mix_t.T.astype(BF16), wo_ref[...], preferred_element_type=F32)
    h = x_ref[...] + g1_ref[0] * mix
    h_ref[...] = h
    hn_ref[...] = (_rmsnorm_rows(h, gf_ref[...]) * (1.0 + sc_ref[0]) + sh_ref[0]).astype(BF16)


def _mix_out(yt3, yat, x2, g1, sh2, sc2, gs, ga, w_glu, b_glu, w_out, gf, seq, tm):
    d_ssm, _, steps = yt3.shape
    n, d = x2.shape
    d_attn = yat.shape[0]
    bsz = n // seq
    tiles_per_batch = seq // tm
    row = lambda i: (i, 0)
    col = lambda i: (0, i)
    const = lambda i: (0, 0)
    per_batch = lambda i: (i // tiles_per_batch, 0, 0)
    mod_spec = pl.BlockSpec((1, 1, d), per_batch)
    return pl.pallas_call(
        _mix_out_kernel,
        grid=(n // tm,),
        in_specs=[pl.BlockSpec((d_ssm, tm // steps, steps), lambda i: (0, i, 0)),
                  pl.BlockSpec((d_attn, tm), col),
                  pl.BlockSpec((tm, d), row),
                  mod_spec, mod_spec, mod_spec,
                  pl.BlockSpec((d_ssm, 1), const),
                  pl.BlockSpec((d_attn, 1), const),
                  pl.BlockSpec((d_ssm, d_ssm), const),
                  pl.BlockSpec((d_ssm, 1), const),
                  pl.BlockSpec((d_ssm + d_attn, d), const),
                  pl.BlockSpec((1, d), const)],
        out_specs=[pl.BlockSpec((tm, d), row), pl.BlockSpec((tm, d), row)],
        out_shape=[jax.ShapeDtypeStruct((n, d), F32), jax.ShapeDtypeStruct((n, d), BF16)],
        compiler_params=_params("parallel"),
        name="mix_out",
    )(yt3, yat, x2, g1.reshape(bsz, 1, d), sh2.reshape(bsz, 1, d), sc2.reshape(bsz, 1, d),
      gs.reshape(d_ssm, 1), ga.reshape(d_attn, 1), w_glu.T.astype(BF16), b_glu.reshape(d_ssm, 1),
      w_out.astype(BF16), gf.reshape(1, d))


def _ffn_kernel(tiles_per_batch, tf,
                hn_ref, halo_ref, h_ref, g2_ref, wu_ref, cw_ref, cb_ref, wd_ref,
                o_ref, lhs_scr, act_scr):
    i = pl.program_id(0)
    halo_rows = halo_ref.shape[0]
    lhs_scr[0:halo_rows, :] = jnp.where(i % tiles_per_batch == 0, 0, halo_ref[...]).astype(BF16)
    lhs_scr[halo_rows:, :] = hn_ref[...]

    d_ff = act_scr.shape[1]

    def conv_up(c0):
        up = jnp.dot(lhs_scr[...], wu_ref[:, c0:c0 + tf], preferred_element_type=F32)
        conv = cb_ref[:, c0:c0 + tf] + cw_ref[CONV_W - 1:CONV_W, c0:c0 + tf] * up
        for back in range(1, CONV_W):
            tap = CONV_W - 1 - back
            conv = conv + cw_ref[tap:tap + 1, c0:c0 + tf] * pltpu.roll(up, back, 0)
        return conv[halo_rows:]

    for c0 in range(0, d_ff, tf):
        act_scr[:, c0:c0 + tf] = (_gelu(conv_up(c0)) * conv_up(d_ff + c0)).astype(BF16)

    down = jnp.dot(act_scr[...], wd_ref[...], preferred_element_type=F32)
    o_ref[...] = h_ref[...] + g2_ref[0] * down


def _ffn(hn2, h1, g2, w_up, conv_w, conv_b, w_down, seq, tm, tf):
    n, d = h1.shape
    d_ff = w_down.shape[0]
    bsz = n // seq
    tiles_per_batch = seq // tm
    halo_rows = BF16_SUBLANES
    assert halo_rows >= CONV_W - 1 and tm % halo_rows == 0 and d_ff % tf == 0
    halo_per_tile = tm // halo_rows
    row = lambda i: (i, 0)
    const = lambda i: (0, 0)
    resident = pl.Buffered(1)

    return pl.pallas_call(
        functools.partial(_ffn_kernel, tiles_per_batch, tf),
        grid=(n // tm,),
        in_specs=[pl.BlockSpec((tm, d), row),
                  pl.BlockSpec((halo_rows, d), lambda i: (jnp.maximum(i * halo_per_tile - 1, 0), 0)),
                  pl.BlockSpec((tm, d), row),
                  pl.BlockSpec((1, 1, d), lambda i: (i // tiles_per_batch, 0, 0)),
                  pl.BlockSpec((d, 2 * d_ff), const, pipeline_mode=resident),
                  pl.BlockSpec((CONV_W, 2 * d_ff), const, pipeline_mode=resident),
                  pl.BlockSpec((1, 2 * d_ff), const, pipeline_mode=resident),
                  pl.BlockSpec((d_ff, d), const, pipeline_mode=resident)],
        out_specs=pl.BlockSpec((tm, d), row),
        out_shape=jax.ShapeDtypeStruct((n, d), F32),
        scratch_shapes=[pltpu.VMEM((halo_rows + tm, d), BF16),
                        pltpu.VMEM((tm, d_ff), BF16)],
        compiler_params=_params("parallel"),
        name="ffn",
    )(hn2, hn2, h1, g2.reshape(bsz, 1, d), w_up.astype(BF16), conv_w, conv_b.reshape(1, 2 * d_ff),
      w_down.astype(BF16))


def _tile(total, want):
    t = min(total, want)
    assert total % t == 0
    return t


def kernel(x, c, positions, w_ada, b_ada, norm_mix, w_in, ssm_log_dt, ssm_a_re, ssm_a_im, ssm_b_re, ssm_b_im, ssm_c_re, ssm_c_im, ssm_d, ssm_w_glu, ssm_b_glu, q_norm, k_norm, out_norm_ssm, out_norm_attn, w_out, norm_ffn, w_up, conv_w, conv_b, w_down):
    bsz, seq, d = x.shape
    depth = w_ada.shape[0]
    n = bsz * seq
    d_ssm = out_norm_ssm.shape[1]
    d_attn = out_norm_attn.shape[1]
    dh = q_norm.shape[1]
    d_ff = w_down.shape[1]
    assert w_in.shape[2] == d_ssm + 3 * d_attn + IDX_HEADS * IDX_DIM + IDX_DIM + IDX_HEADS
    assert seq % SSM_CHUNK == 0

    t_proj = _tile(seq, 1024)
    t_key = _tile(seq, 512)
    t_q = _tile(seq, 256)
    t_mix = _tile(seq, 1024)
    t_ffn = _tile(seq, 512)
    t_ff = _tile(d_ff, 256)

    h = x.astype(F32).reshape(n, d)
    pos_row = positions.reshape(1, n)
    for l in range(depth):
        mod = _adaln(c.astype(F32), w_ada[l].astype(F32), b_ada[l].astype(F32))
        sh1, sc1, g1, sh2, sc2, g2 = jnp.split(mod, 6, axis=-1)
        ut, qq, k, ksq, vt4, ki, wit = _in_proj(h, norm_mix[l], sh1, sc1, w_in[l], pos_row, q_norm[l],
                                                k_norm[l], seq, t_proj, t_key, d_ssm, d_attn, dh)
        yt = _s5(ut, seq, ssm_log_dt[l], ssm_a_re[l], ssm_a_im[l], ssm_b_re[l], ssm_b_im[l],
                 ssm_c_re[l], ssm_c_im[l], ssm_d[l])
        yat = _dsa(qq, wit, k, ksq, ki, vt4, seq, t_q, dh)
        h1, hn2 = _mix_out(yt, yat, h, g1, sh2, sc2, out_norm_ssm[l], out_norm_attn[l], ssm_w_glu[l],
                           ssm_b_glu[l], w_out[l], norm_ffn[l], seq, t_mix)
        h = _ffn(hn2, h1, g2, w_up[l], conv_w[l], conv_b[l], w_down[l], seq, t_ffn, t_ff)
    return h.reshape(bsz, seq, d).astype(x.dtype)
```

```python
import functools
import math

import jax
import jax.numpy as jnp
from jax import lax
from jax.experimental import pallas as pl
from jax.experimental.pallas import tpu as pltpu

EPS = 1e-6
ROPE_THETA = 10000.0
IDX_HEADS = 8
IDX_DIM = 64
TOPK_MAX = 256
CONV_W = 3

LANES = 128
BF16_SUBLANES = 16
SSM_CHUNK = LANES
TOEPLITZ_BLOCK = 4
PART_ROWS = 32
VMEM_LIMIT = 56 * 1024 * 1024
NEG_BIG = -1e30
LOG2E = math.log2(math.e)

F32 = jnp.float32
BF16 = jnp.bfloat16
HIGHEST = lax.Precision.HIGHEST


def _params(*semantics):
    return pltpu.CompilerParams(dimension_semantics=semantics, vmem_limit_bytes=VMEM_LIMIT)


def _gelu(x):
    return 0.5 * x * (1.0 + jnp.tanh(math.sqrt(2.0 / math.pi) * (x + 0.044715 * (x * x * x))))


def _adaln_kernel(ct_ref, w_ref, b_ref, o_ref):
    ct = ct_ref[...]
    cs = ct * jax.nn.sigmoid(ct)
    w = w_ref[...]
    for b in range(ct.shape[1]):
        o_ref[b:b + 1, :] = jnp.sum(w * cs[:, b:b + 1], axis=0, keepdims=True) + b_ref[...]


def _adaln(c, w, bias):
    bsz, d = c.shape
    n = w.shape[1]
    tn = 1024 if n % 1024 == 0 else n
    return pl.pallas_call(
        _adaln_kernel,
        grid=(n // tn,),
        in_specs=[pl.BlockSpec((d, bsz), lambda j: (0, 0)),
                  pl.BlockSpec((d, tn), lambda j: (0, j)),
                  pl.BlockSpec((1, tn), lambda j: (0, j))],
        out_specs=pl.BlockSpec((bsz, tn), lambda j: (0, j)),
        out_shape=jax.ShapeDtypeStruct((bsz, n), F32),
        compiler_params=_params("parallel"),
        name="adaln",
    )(c.T, w, bias.reshape(1, n))


def _rmsnorm_rows(x, g):
    return x * lax.rsqrt(jnp.mean(x * x, axis=-1, keepdims=True) + EPS) * g


def _rope_t(x3, cos, sin):
    half = x3.shape[1] // 2
    x1, x2 = x3[:, :half, :], x3[:, half:, :]
    return jnp.concatenate([x1 * cos - x2 * sin, x1 * sin + x2 * cos], axis=1)


def _pair_rows(x3):
    heads, dh, t = x3.shape
    zero = jnp.zeros_like(x3)
    even = lax.broadcasted_iota(jnp.int32, (heads, 1, 1), 0) % 2 == 0
    return jnp.concatenate([jnp.where(even, x3, zero), jnp.where(even, zero, x3)], axis=1).reshape(heads * 2 * dh, t)


def _in_proj_kernel(d_ssm, d_attn, dh, wi_scale,
                    x_ref, g_ref, sh_ref, sc_ref, w_ref, pos_ref, inv_ref, qg_ref, kg_ref,
                    ut_ref, qt_ref, k_ref, ksq_ref, vt_ref, qit_ref, ki_ref, wit_ref):
    hn = _rmsnorm_rows(x_ref[...], g_ref[...]) * (1.0 + sc_ref[0]) + sh_ref[0]
    hb = hn.astype(BF16)
    tm = hb.shape[0]

    def proj_t(r0, rows):
        return lax.dot_general(w_ref[:, r0:r0 + rows], hb, (((0,), (1,)), ((), ())),
                               preferred_element_type=F32)

    ang = inv_ref[...] * pos_ref[...].astype(F32)
    cos, sin = jnp.cos(ang)[None], jnp.sin(ang)[None]

    def head_norm(x3, gcol):
        return x3 * lax.rsqrt(jnp.mean(x3 * x3, axis=1, keepdims=True) + EPS) * gcol[None]

    ut_ref[...] = proj_t(0, d_ssm).reshape(d_ssm, tm // LANES, LANES)
    o = d_ssm
    n_heads = d_attn // dh
    q3 = proj_t(o, d_attn).reshape(n_heads, dh, tm)
    q3 = _rope_t(head_norm(q3, qg_ref[...]), cos, sin) * (dh ** -0.5 * LOG2E)
    qt_ref[...] = _pair_rows(q3).astype(BF16)
    o += d_attn
    k3 = _rope_t(head_norm(proj_t(o, d_attn).reshape(n_heads, dh, tm), kg_ref[...]), cos, sin)
    k_ref[...] = k3.reshape(d_attn, tm).T.astype(BF16)
    ksq = jnp.max(jnp.sum(k3 * k3, axis=1), axis=1, keepdims=True)
    ksq_ref[...] = jnp.broadcast_to(ksq, ksq_ref.shape)
    o += d_attn
    v3 = proj_t(o, d_attn).reshape(n_heads, dh, tm)
    vt = jnp.concatenate([v3, jnp.ones((n_heads, BF16_SUBLANES, tm), F32)], axis=1).astype(BF16)
    kc = vt_ref.shape[3]
    for j in range(tm // kc):
        vt_ref[j] = vt[:, :, j * kc:(j + 1) * kc]
    o += d_attn
    n_qi = IDX_HEADS * IDX_DIM
    qi3 = _rope_t(proj_t(o, n_qi).reshape(IDX_HEADS, IDX_DIM, tm), cos, sin)
    qit_ref[...] = _pair_rows(qi3).astype(BF16)
    o += n_qi
    tail = proj_t(o, IDX_DIM + BF16_SUBLANES)
    ki = _rope_t(tail[None, :IDX_DIM], cos, sin)[0]
    ki_ref[...] = jnp.concatenate([ki, ki], axis=0).T.astype(BF16)
    wit_ref[...] = tail[IDX_DIM:IDX_DIM + IDX_HEADS] * wi_scale


def _in_proj(x2, g, sh, sc, w_in, pos_row, q_norm, k_norm, seq, tm, kc, d_ssm, d_attn, dh):
    n, d = x2.shape
    bsz = n // seq
    tiles_per_batch = seq // tm
    n_qi = IDX_HEADS * IDX_DIM
    n_heads = d_attn // dh
    w = jnp.concatenate([w_in.astype(BF16), jnp.zeros((d, BF16_SUBLANES - IDX_HEADS), BF16)], axis=1)
    rt = w.shape[1]
    inv = (ROPE_THETA ** (-jnp.arange(0, dh, 2, dtype=F32) / dh)).reshape(dh // 2, 1)
    wi_scale = IDX_HEADS ** -0.5 * IDX_DIM ** -0.5

    row = lambda i: (i, 0)
    col = lambda i: (0, i)
    const = lambda i: (0, 0)
    per_batch = lambda i: (i // tiles_per_batch, 0, 0)
    return pl.pallas_call(
        functools.partial(_in_proj_kernel, d_ssm, d_attn, dh, wi_scale),
        grid=(n // tm,),
        in_specs=[pl.BlockSpec((tm, d), row),
                  pl.BlockSpec((1, d), const),
                  pl.BlockSpec((1, 1, d), per_batch),
                  pl.BlockSpec((1, 1, d), per_batch),
                  pl.BlockSpec((d, rt), const),
                  pl.BlockSpec((1, tm), col),
                  pl.BlockSpec((dh // 2, 1), const),
                  pl.BlockSpec((dh, 1), const),
                  pl.BlockSpec((dh, 1), const)],
        out_specs=[pl.BlockSpec((d_ssm, tm // LANES, LANES), lambda i: (0, i, 0)),
                   pl.BlockSpec((2 * d_attn, tm), col),
                   pl.BlockSpec((tm, d_attn), row),
                   pl.BlockSpec((n_heads, LANES), row),
                   pl.BlockSpec((tm // kc, n_heads, dh + BF16_SUBLANES, kc), lambda i: (i, 0, 0, 0)),
                   pl.BlockSpec((2 * n_qi, tm), col),
                   pl.BlockSpec((tm, 2 * IDX_DIM), row),
                   pl.BlockSpec((IDX_HEADS, tm), col)],
        out_shape=[jax.ShapeDtypeStruct((d_ssm, n // LANES, LANES), F32),
                   jax.ShapeDtypeStruct((2 * d_attn, n), BF16),
                   jax.ShapeDtypeStruct((n, d_attn), BF16),
                   jax.ShapeDtypeStruct((n // tm * n_heads, LANES), F32),
                   jax.ShapeDtypeStruct((n // kc, n_heads, dh + BF16_SUBLANES, kc), BF16),
                   jax.ShapeDtypeStruct((2 * n_qi, n), BF16),
                   jax.ShapeDtypeStruct((n, 2 * IDX_DIM), BF16),
                   jax.ShapeDtypeStruct((IDX_HEADS, n), F32)],
        compiler_params=_params("parallel"),
        name="in_proj",
    )(x2, g.reshape(1, d), sh.reshape(bsz, 1, d), sc.reshape(bsz, 1, d), w, pos_row, inv,
      q_norm.reshape(dh, 1), k_norm.reshape(dh, 1))


def _cmul(ar, ai, br, bi):
    return ar * br - ai * bi, ar * bi + ai * br


def _s5_kernel(chunks_per_batch,
               ut_ref, ldt_ref, ar_row_ref, ai_row_ref, ar_col_ref, ai_col_ref,
               btr_ref, bti_ref, cr_ref, ci_ref, ctr_ref, cti_ref, d_ref,
               y_ref, k2_scr, tm_scr, w_scr, v_scr, yi_scr):
    nc, p, t = btr_ref.shape[1], btr_ref.shape[2], SSM_CHUNK
    dt = jnp.exp(ldt_ref[0])
    lr, li = ar_row_ref[0], ai_row_ref[0]
    mag = jnp.exp(lr * dt)
    lbr, lbi = mag * jnp.cos(li * dt), mag * jnp.sin(li * dt)
    den = lr * lr + li * li
    nr, ni = lbr - 1.0, lbi
    coef_r, coef_i = (nr * lr + ni * li) / den, (ni * lr - nr * li) / den
    bbr, bbi = _cmul(coef_r, coef_i, btr_ref[0], bti_ref[0])

    def powers(rate, freq, n):
        m = jnp.exp(rate * n)
        return m * jnp.cos(freq * n), m * jnp.sin(freq * n)

    rate_c, freq_c = ar_col_ref[0] * dt, ai_col_ref[0] * dt
    rate_r, freq_r = lr * dt, li * dt
    tau = lax.broadcasted_iota(jnp.int32, (1, t), 1).astype(F32)
    pr0, pi0 = powers(rate_c, freq_c, tau)
    pr1, pi1 = _cmul(pr0, pi0, *powers(rate_c, freq_c, 1.0))
    back = (t - 1) - lax.broadcasted_iota(jnp.int32, (t, 1), 0)
    qr, qi = powers(rate_r, freq_r, back.astype(F32))

    cr, ci = cr_ref[0], ci_ref[0]
    rows = []
    for c in range(nc):
        gr, gi = _cmul(cr[c:c + 1], ci[c:c + 1], bbr, bbi)
        rows.append(jnp.concatenate([gr, -gi], axis=1))
    k2_scr[...] = jnp.dot(jnp.concatenate(rows, axis=0), jnp.concatenate([pr0, pi0], axis=0),
                          precision=HIGHEST, preferred_element_type=F32)

    causal = (lax.broadcasted_iota(jnp.int32, (t, t), 1) >= lax.broadcasted_iota(jnp.int32, (t, t), 0))

    width = tm_scr.shape[2] // t
    n_blocks = nc // width

    def build_block(q):
        for e in range(width):
            c = q * width + e
            for cp in range(nc):
                krow = jnp.broadcast_to(k2_scr[pl.ds(c * nc + cp, 1), :], (t, t))
                tile = pltpu.roll(krow, 0, 1, stride=1, stride_axis=0)
                tm_scr[q, cp * t:(cp + 1) * t, e * t:(e + 1) * t] = jnp.where(causal, tile, 0.0).astype(BF16)

    a = jnp.concatenate([ut_ref[c] for c in range(nc)], axis=1).astype(BF16)

    q2r, q2i = jnp.concatenate([qr, qr], axis=1), jnp.concatenate([qi, qi], axis=1)
    b_a = jnp.concatenate([bbr, bbi], axis=1)
    b_b = jnp.concatenate([-bbi, bbr], axis=1)
    for c in range(nc):
        w_scr[c * t:(c + 1) * t, :] = (q2r * b_a[c:c + 1] + q2i * b_b[c:c + 1]).astype(BF16)
    s = jnp.dot(a, w_scr[...], preferred_element_type=F32)

    r = s.shape[0]
    kidx = lax.broadcasted_iota(jnp.int32, (r, 1), 0) % chunks_per_batch

    def times_lam_pow(z, n):
        zr, zi = powers(rate_r, freq_r, float(n))
        return (jnp.concatenate([zr, zr], axis=1) * z
                + jnp.concatenate([-zi, zi], axis=1) * pltpu.roll(z, p, 1))

    step = 1
    while step < chunks_per_batch:
        prev = pltpu.roll(s, step, 0)
        s = s + jnp.where(kidx >= step, times_lam_pow(prev, t * step), 0.0)
        step *= 2
    e = jnp.where(kidx >= 1, pltpu.roll(s, 1, 0), 0.0)

    ctr, cti = ctr_ref[0], cti_ref[0]
    p2r, p2i = jnp.concatenate([pr1, pr1], axis=0), jnp.concatenate([pi1, pi1], axis=0)
    c_a = jnp.concatenate([ctr, -cti], axis=0)
    c_b = jnp.concatenate([-cti, -ctr], axis=0)
    for c in range(nc):
        v_scr[:, c * t:(c + 1) * t] = (p2r * c_a[:, c:c + 1] + p2i * c_b[:, c:c + 1]).astype(BF16)
    y_inter = jnp.dot(e.astype(BF16), v_scr[...], preferred_element_type=F32)
    for q in range(n_blocks):
        yi_scr[q] = y_inter[:, q * width * t:(q + 1) * width * t]

    def emit_block(q):
        y = jnp.dot(a, tm_scr[q], preferred_element_type=F32) + yi_scr[q]
        for e in range(width):
            c = q * width + e
            y_ref[c] = y[:, e * t:(e + 1) * t] + d_ref[c] * ut_ref[c]

    build_block(0)

    def trip(q, carry):
        build_block(q + 1)
        emit_block(q)
        return carry

    lax.fori_loop(0, n_blocks - 1, trip, 0)
    emit_block(n_blocks - 1)


def _s5(ut3, seq, log_dt, a_re, a_im, b_re, b_im, c_re, c_im, d_skip):
    d_ssm, r, t = ut3.shape
    assert t == SSM_CHUNK
    n = r * t
    g, p = a_re.shape
    nc = b_re.shape[2]
    grp = lambda i: (i, 0, 0)
    row_spec = pl.BlockSpec((1, 1, p), grp)
    col_spec = pl.BlockSpec((1, p, 1), grp)
    cp_spec = pl.BlockSpec((1, nc, p), grp)
    pc_spec = pl.BlockSpec((1, p, nc), grp)
    return pl.pallas_call(
        functools.partial(_s5_kernel, seq // t),
        grid=(g,),
        in_specs=[pl.BlockSpec((nc, r, t), grp),
                  pl.BlockSpec((1, 1, 1), grp),
                  row_spec, row_spec, col_spec, col_spec,
                  cp_spec, cp_spec, cp_spec, cp_spec, pc_spec, pc_spec,
                  pl.BlockSpec((nc, 1, 1), grp)],
        out_specs=pl.BlockSpec((nc, r, t), grp),
        out_shape=jax.ShapeDtypeStruct((d_ssm, r, t), F32),
        scratch_shapes=[pltpu.VMEM((nc * nc, t), F32),
                        pltpu.VMEM((nc // TOEPLITZ_BLOCK, nc * t, TOEPLITZ_BLOCK * t), BF16),
                        pltpu.VMEM((nc * t, 2 * p), BF16),
                        pltpu.VMEM((2 * p, nc * t), BF16),
                        pltpu.VMEM((nc // TOEPLITZ_BLOCK, r, TOEPLITZ_BLOCK * t), F32)],
        compiler_params=_params("parallel"),
        name="s5",
    )(ut3, log_dt.reshape(g, 1, 1),
      a_re.reshape(g, 1, p), a_im.reshape(g, 1, p), a_re.reshape(g, p, 1), a_im.reshape(g, p, 1),
      jnp.swapaxes(b_re, 1, 2), jnp.swapaxes(b_im, 1, 2), c_re, c_im,
      jnp.swapaxes(c_re, 1, 2), jnp.swapaxes(c_im, 1, 2), d_skip.reshape(d_ssm, 1, 1))


def _dsa_kernel(topk, seq, tq, kc, dh, bisect_passes,
                qt_ref, qit_ref, wit_ref, k_ref, ksq_ref, ki_ref, vt_ref, o_ref,
                i_scr, m_scr, l_scr, acc_scr):
    i = pl.program_id(1)
    n_heads = acc_scr.shape[0]
    n_chunks = (i * tq + tq + kc - 1) // kc

    def head_rows(ref, h):
        return ref[h * LANES:(h + 1) * LANES, :]

    qpos = i * tq + lax.broadcasted_iota(jnp.int32, (1, tq), 1)
    krow = lax.broadcasted_iota(jnp.int32, (kc, 1), 0)
    wi = wit_ref[...]
    qi_all = jnp.concatenate([head_rows(qit_ref, h) for h in range(IDX_HEADS)], axis=1)

    def chunk_loop(body, carry):
        done = 0
        for width, shift in ((4, 2), (2, 1), (1, 0)):
            trips = (n_chunks - done) >> shift
            carry = lax.fori_loop(
                0, trips, lambda j, cr, w=width, d=done: body(tuple(d + w * j + u for u in range(w)), cr), carry)
            done = done + trips * width
        return carry

    def index_chunks(chunks, carry):
        lo, hi = carry
        starts = [pl.multiple_of(c * kc, kc) for c in chunks]
        dots = [jnp.dot(ki_ref[pl.ds(r0, kc), :], qi_all, preferred_element_type=F32) for r0 in starts]
        for c, r0, s_all in zip(chunks, starts, dots):
            score = jnp.zeros((kc, tq), F32)
            for h in range(IDX_HEADS):
                score = score + wi[h:h + 1, :] * jnp.maximum(s_all[:, h * tq:(h + 1) * tq], 0.0)
            i_scr[c] = jnp.where(r0 + krow <= qpos, score, -jnp.inf)
            lo = jnp.minimum(lo, jnp.min(score, axis=0, keepdims=True))
            hi = jnp.maximum(hi, jnp.max(score, axis=0, keepdims=True))
        return lo, hi

    lo, hi = chunk_loop(index_chunks, (jnp.full((1, tq), jnp.inf, F32), jnp.full((1, tq), -jnp.inf, F32)))

    def over_chunks(per_chunk, combine, init):
        def body(chunks, acc):
            for c in chunks:
                acc = combine(acc, per_chunk(i_scr[c]).reshape(kc // PART_ROWS, PART_ROWS, tq))
            return acc
        return chunk_loop(body, jnp.full((PART_ROWS, tq), init, F32))

    def count(pred):
        parts = over_chunks(lambda x: jnp.where(pred(x), 1.0, 0.0),
                            lambda acc, part: acc + jnp.sum(part, axis=0), 0.0)
        return jnp.sum(parts, axis=0, keepdims=True)

    def max_below(bound):
        parts = over_chunks(lambda x: jnp.where(x < bound, x, -jnp.inf),
                            lambda acc, part: jnp.maximum(acc, jnp.max(part, axis=0)), -jnp.inf)
        return jnp.max(parts, axis=0, keepdims=True)

    kf = float(topk)
    n_causal = (qpos + 1).astype(F32)
    hi = hi + jnp.abs(hi) * 2.0 ** -20 + 1e-30

    def bisect(_, st):
        lo, hi, c_lo, active = st
        mid = 0.5 * lo + 0.5 * hi
        c = count(lambda x: x >= mid)
        upd = jnp.logical_and(active > 0.0, jnp.logical_and(mid > lo, mid < hi))
        up_lo = jnp.logical_and(upd, c >= kf)
        up_hi = jnp.logical_and(upd, c < kf)
        lo = jnp.where(up_lo, mid, lo)
        c_lo = jnp.where(up_lo, c, c_lo)
        hi = jnp.where(up_hi, mid, hi)
        return lo, hi, c_lo, jnp.where(c_lo > kf, active, 0.0)

    active0 = jnp.where(n_causal > kf, 1.0, 0.0)
    lo, hi, c_lo, active = lax.fori_loop(0, bisect_passes, bisect, (lo, hi, n_causal, active0))

    def stepping(st):
        return jnp.max(st[3]) > 0.0

    def step_down(st):
        lo, hi, c_lo, active = st
        v = max_below(hi)
        c = count(lambda x: x >= v)
        on = active > 0.0
        found = jnp.logical_and(on, c >= kf)
        lo = jnp.where(found, v, lo)
        c_lo = jnp.where(found, c, c_lo)
        hi = jnp.where(jnp.logical_and(on, c < kf), v, hi)
        return lo, hi, c_lo, jnp.where(found, 0.0, active)

    thr, _, c_thr, _ = lax.while_loop(stepping, step_down, (lo, hi, c_lo, active))
    tied = jnp.max(jnp.where(c_thr > kf, 1.0, 0.0)) > 0.0

    qsq = jnp.zeros((1, tq), F32)
    for h in range(n_heads):
        qh = head_rows(qt_ref, h).astype(F32)
        qsq = jnp.maximum(qsq, jnp.sum(qh * qh, axis=0, keepdims=True))
    ksq = jnp.max(jnp.max(ksq_ref[...], axis=0, keepdims=True), axis=1, keepdims=True)
    shift = -1.02 * jnp.sqrt(qsq * ksq)

    @pl.when(jnp.logical_not(tied))
    def _():
        def mask_chunk(c, carry):
            i_scr[c] = jnp.where(i_scr[c] >= thr, shift, NEG_BIG)
            return carry
        lax.fori_loop(0, n_chunks, mask_chunk, 0)

    @pl.when(tied)
    def _():
        above = count(lambda x: x > thr)
        room = jnp.where(c_thr > kf, kf - above, float(seq))
        tri = (lax.broadcasted_iota(jnp.int32, (kc, kc), 0)
               >= lax.broadcasted_iota(jnp.int32, (kc, kc), 1)).astype(BF16)

        def mask_chunk(c, seen):
            x = i_scr[c]
            eq = jnp.where(x == thr, 1.0, 0.0)
            rank = jnp.dot(tri, eq.astype(BF16), preferred_element_type=F32) + seen
            keep = jnp.logical_or(x > thr, jnp.logical_and(x == thr, rank <= room))
            i_scr[c] = jnp.where(keep, shift, NEG_BIG)
            return seen + jnp.sum(eq, axis=0, keepdims=True)
        lax.fori_loop(0, n_chunks, mask_chunk, jnp.zeros((1, tq), F32))

    def logits(c, h):
        r0 = pl.multiple_of(c * kc, kc)
        kk = k_ref[pl.ds(r0, kc), (h // 2) * LANES:(h // 2 + 1) * LANES]
        return jnp.dot(kk, head_rows(qt_ref, h), preferred_element_type=F32) + i_scr[c]

    acc_scr[...] = jnp.zeros(acc_scr.shape, F32)

    def attend_chunks(chunks, carry):
        ahead = 4
        items = [(c, h) for c in chunks for h in range(n_heads)]
        probs = [jnp.exp2(logits(c, h)).astype(BF16) for c, h in items[:ahead]]
        for n, (c, h) in enumerate(items):
            if n + ahead < len(items):
                probs.append(jnp.exp2(logits(*items[n + ahead])).astype(BF16))
            acc_scr[h] += jnp.dot(vt_ref[c, h], probs[n], preferred_element_type=F32)
        return carry

    chunk_loop(attend_chunks, 0)

    l_min = jnp.full((1, tq), jnp.inf, F32)
    for h in range(n_heads):
        denom = acc_scr[h, dh:dh + 1, :]
        l_min = jnp.minimum(l_min, denom)
        o_ref[h * dh:(h + 1) * dh, :] = acc_scr[h, 0:dh, :] / denom

    @pl.when(jnp.min(l_min) < 2.0 ** -64)
    def _():
        m_scr[...] = jnp.full(m_scr.shape, NEG_BIG, F32)
        l_scr[...] = jnp.zeros(l_scr.shape, F32)
        acc_scr[...] = jnp.zeros(acc_scr.shape, F32)

        def attend_chunk_online(c, carry):
            for h in range(n_heads):
                s = logits(c, h)
                m_old = m_scr[h:h + 1, :]
                m_new = jnp.maximum(m_old, jnp.max(s, axis=0, keepdims=True))
                alpha = jnp.exp2(m_old - m_new)
                p = jnp.exp2(s - m_new)
                l_scr[h:h + 1, :] = alpha * l_scr[h:h + 1, :] + jnp.sum(p, axis=0, keepdims=True)
                m_scr[h:h + 1, :] = m_new
                pv = jnp.dot(vt_ref[c, h, 0:dh, :], p.astype(BF16), preferred_element_type=F32)
                acc_scr[h, 0:dh, :] = alpha * acc_scr[h, 0:dh, :] + pv
            return carry

        lax.fori_loop(0, n_chunks, attend_chunk_online, 0)
        for h in range(n_heads):
            o_ref[h * dh:(h + 1) * dh, :] = acc_scr[h, 0:dh, :] / l_scr[h:h + 1, :]


def _dsa(qt, qit, wit, k, ksq, ki, vt4, seq, tq, dh):
    n, d_attn = k.shape
    bsz = n // seq
    kc = vt4.shape[3]
    n_heads = d_attn // dh
    n_qi = IDX_HEADS * IDX_DIM
    assert 2 * dh == LANES and 2 * IDX_DIM == LANES and n_heads % 2 == 0 and IDX_HEADS % 2 == 0
    assert kc % PART_ROWS == 0
    topk = min(TOPK_MAX, seq // 4)
    blocks_per_batch = seq // tq
    chunks_per_batch = seq // kc
    bisect_passes = 14
    qcol = lambda b, i: (0, b * blocks_per_batch + i)
    per_batch = lambda b, i: (b, 0)
    return pl.pallas_call(
        functools.partial(_dsa_kernel, topk, seq, tq, kc, dh, bisect_passes),
        grid=(bsz, blocks_per_batch),
        in_specs=[pl.BlockSpec((2 * d_attn, tq), qcol),
                  pl.BlockSpec((2 * n_qi, tq), qcol),
                  pl.BlockSpec((IDX_HEADS, tq), qcol),
                  pl.BlockSpec((seq, d_attn), per_batch),
                  pl.BlockSpec((ksq.shape[0] // bsz, LANES), per_batch),
                  pl.BlockSpec((seq, 2 * IDX_DIM), per_batch),
                  pl.BlockSpec((chunks_per_batch, n_heads, dh + BF16_SUBLANES, kc), lambda b, i: (b, 0, 0, 0))],
        out_specs=pl.BlockSpec((d_attn, tq), qcol),
        out_shape=jax.ShapeDtypeStruct((d_attn, n), F32),
        scratch_shapes=[pltpu.VMEM((chunks_per_batch, kc, tq), F32),
                        pltpu.VMEM((n_heads, tq), F32),
                        pltpu.VMEM((n_heads, tq), F32),
                        pltpu.VMEM((n_heads, dh + BF16_SUBLANES, tq), F32)],
        compiler_params=_params("parallel", "arbitrary"),
        name="dsa",
    )(qt, qit, wit, k, ksq, ki, vt4)


def _mix_out_kernel(yt_ref, yat_ref, x_ref, g1_ref, sh_ref, sc_ref, gs_ref, ga_ref, wglu_ref, bglu_ref,
                    wo_ref, gf_ref, h_ref, hn_ref):
    def norm_cols(v, g):
        return v * lax.rsqrt(jnp.mean(v * v, axis=0, keepdims=True) + EPS) * g

    d_ssm, chunks, steps = yt_ref.shape
    y = _gelu(yt_ref[...].reshape(d_ssm, chunks * steps))
    z = jnp.dot(wglu_ref[...], y.astype(BF16), preferred_element_type=F32) + bglu_ref[...]
    y = y * jax.nn.sigmoid(z)
    mix_t = jnp.concatenate([norm_cols(y, gs_ref[...]), norm_cols(yat_ref[...], ga_ref[...])], axis=0)
    mix = jnp.dot(mix_t.T.astype(BF16), wo_ref[...], preferred_element_type=F32)
    h = x_ref[...] + g1_ref[0] * mix
    h_ref[...] = h
    hn_ref[...] = (_rmsnorm_rows(h, gf_ref[...]) * (1.0 + sc_ref[0]) + sh_ref[0]).astype(BF16)


def _mix_out(yt3, yat, x2, g1, sh2, sc2, gs, ga, w_glu, b_glu, w_out, gf, seq, tm):
    d_ssm, _, steps = yt3.shape
    n, d = x2.shape
    d_attn = yat.shape[0]
    bsz = n // seq
    tiles_per_batch = seq // tm
    row = lambda i: (i, 0)
    col = lambda i: (0, i)
    const = lambda i: (0, 0)
    per_batch = lambda i: (i // tiles_per_batch, 0, 0)
    mod_spec = pl.BlockSpec((1, 1, d), per_batch)
    return pl.pallas_call(
        _mix_out_kernel,
        grid=(n // tm,),
        in_specs=[pl.BlockSpec((d_ssm, tm // steps, steps), lambda i: (0, i, 0)),
                  pl.BlockSpec((d_attn, tm), col),
                  pl.BlockSpec((tm, d), row),
                  mod_spec, mod_spec, mod_spec,
                  pl.BlockSpec((d_ssm, 1), const),
                  pl.BlockSpec((d_attn, 1), const),
                  pl.BlockSpec((d_ssm, d_ssm), const),
                  pl.BlockSpec((d_ssm, 1), const),
                  pl.BlockSpec((d_ssm + d_attn, d), const),
                  pl.BlockSpec((1, d), const)],
        out_specs=[pl.BlockSpec((tm, d), row), pl.BlockSpec((tm, d), row)],
        out_shape=[jax.ShapeDtypeStruct((n, d), F32), jax.ShapeDtypeStruct((n, d), BF16)],
        compiler_params=_params("parallel"),
        name="mix_out",
    )(yt3, yat, x2, g1.reshape(bsz, 1, d), sh2.reshape(bsz, 1, d), sc2.reshape(bsz, 1, d),
      gs.reshape(d_ssm, 1), ga.reshape(d_attn, 1), w_glu.T.astype(BF16), b_glu.reshape(d_ssm, 1),
      w_out.astype(BF16), gf.reshape(1, d))


def _ffn_kernel(tiles_per_batch, tf,
                hn_ref, halo_ref, h_ref, g2_ref, wu_ref, cw_ref, cb_ref, wd_ref,
                o_ref, lhs_scr, act_scr):
    i = pl.program_id(0)
    halo_rows = halo_ref.shape[0]
    lhs_scr[0:halo_rows, :] = jnp.where(i % tiles_per_batch == 0, 0, halo_ref[...]).astype(BF16)
    lhs_scr[halo_rows:, :] = hn_ref[...]

    d_ff = act_scr.shape[1]

    def conv_up(c0):
        up = jnp.dot(lhs_scr[...], wu_ref[:, c0:c0 + tf], preferred_element_type=F32)
        conv = cb_ref[:, c0:c0 + tf] + cw_ref[CONV_W - 1:CONV_W, c0:c0 + tf] * up
        for back in range(1, CONV_W):
            tap = CONV_W - 1 - back
            conv = conv + cw_ref[tap:tap + 1, c0:c0 + tf] * pltpu.roll(up, back, 0)
        return conv[halo_rows:]

    for c0 in range(0, d_ff, tf):
        act_scr[:, c0:c0 + tf] = (_gelu(conv_up(c0)) * conv_up(d_ff + c0)).astype(BF16)

    down = jnp.dot(act_scr[...], wd_ref[...], preferred_element_type=F32)
    o_ref[...] = h_ref[...] + g2_ref[0] * down


def _ffn(hn2, h1, g2, w_up, conv_w, conv_b, w_down, seq, tm, tf):
    n, d = h1.shape
    d_ff = w_down.shape[0]
    bsz = n // seq
    tiles_per_batch = seq // tm
    halo_rows = BF16_SUBLANES
    assert halo_rows >= CONV_W - 1 and tm % halo_rows == 0 and d_ff % tf == 0
    halo_per_tile = tm // halo_rows
    row = lambda i: (i, 0)
    const = lambda i: (0, 0)
    resident = pl.Buffered(1)

    return pl.pallas_call(
        functools.partial(_ffn_kernel, tiles_per_batch, tf),
        grid=(n // tm,),
        in_specs=[pl.BlockSpec((tm, d), row),
                  pl.BlockSpec((halo_rows, d), lambda i: (jnp.maximum(i * halo_per_tile - 1, 0), 0)),
                  pl.BlockSpec((tm, d), row),
                  pl.BlockSpec((1, 1, d), lambda i: (i // tiles_per_batch, 0, 0)),
                  pl.BlockSpec((d, 2 * d_ff), const, pipeline_mode=resident),
                  pl.BlockSpec((CONV_W, 2 * d_ff), const, pipeline_mode=resident),
                  pl.BlockSpec((1, 2 * d_ff), const, pipeline_mode=resident),
                  pl.BlockSpec((d_ff, d), const, pipeline_mode=resident)],
        out_specs=pl.BlockSpec((tm, d), row),
        out_shape=jax.ShapeDtypeStruct((n, d), F32),
        scratch_shapes=[pltpu.VMEM((halo_rows + tm, d), BF16),
                        pltpu.VMEM((tm, d_ff), BF16)],
        compiler_params=_params("parallel"),
        name="ffn",
    )(hn2, hn2, h1, g2.reshape(bsz, 1, d), w_up.astype(BF16), conv_w, conv_b.reshape(1, 2 * d_ff),
      w_down.astype(BF16))


def _tile(total, want):
    t = min(total, want)
    assert total % t == 0
    return t


def kernel(x, c, positions, w_ada, b_ada, norm_mix, w_in, ssm_log_dt, ssm_a_re, ssm_a_im, ssm_b_re, ssm_b_im, ssm_c_re, ssm_c_im, ssm_d, ssm_w_glu, ssm_b_glu, q_norm, k_norm, out_norm_ssm, out_norm_attn, w_out, norm_ffn, w_up, conv_w, conv_b, w_down):
    bsz, seq, d = x.shape
    depth = w_ada.shape[0]
    n = bsz * seq
    d_ssm = out_norm_ssm.shape[1]
    d_attn = out_norm_attn.shape[1]
    dh = q_norm.shape[1]
    d_ff = w_down.shape[1]
    assert w_in.shape[2] == d_ssm + 3 * d_attn + IDX_HEADS * IDX_DIM + IDX_DIM + IDX_HEADS
    assert seq % SSM_CHUNK == 0

    t_proj = _tile(seq, 1024)
    t_key = _tile(seq, 512)
    t_q = _tile(seq, 256)
    t_mix = _tile(seq, 1024)
    t_ffn = _tile(seq, 512)
    t_ff = _tile(d_ff, 256)

    h = x.astype(F32).reshape(n, d)
    pos_row = positions.reshape(1, n)
    for l in range(depth):
        mod = _adaln(c.astype(F32), w_ada[l].astype(F32), b_ada[l].astype(F32))
        sh1, sc1, g1, sh2, sc2, g2 = jnp.split(mod, 6, axis=-1)
        ut, qt, k, ksq, vt4, qit, ki, wit = _in_proj(h, norm_mix[l], sh1, sc1, w_in[l], pos_row, q_norm[l],
                                                     k_norm[l], seq, t_proj, t_key, d_ssm, d_attn, dh)
        yt = _s5(ut, seq, ssm_log_dt[l], ssm_a_re[l], ssm_a_im[l], ssm_b_re[l], ssm_b_im[l],
                 ssm_c_re[l], ssm_c_im[l], ssm_d[l])
        yat = _dsa(qt, qit, wit, k, ksq, ki, vt4, seq, t_q, dh)
        h1, hn2 = _mix_out(yt, yat, h, g1, sh2, sc2, out_norm_ssm[l], out_norm_attn[l], ssm_w_glu[l],
                           ssm_b_glu[l], w_out[l], norm_ffn[l], seq, t_mix)
        h = _ffn(hn2, h1, g2, w_up[l], conv_w[l], conv_b[l], w_down[l], seq, t_ffn, t_ff)
    return h.reshape(bsz, seq, d).astype(x.dtype)
```
